```python
import math
import jax
import jax.numpy as jnp
from jax import lax
import numpy as np

D_MODEL = 1024
BATCH = 2
SEQ = 8192
DEPTH = 1

CTX_LEN = 256
GRID_W = 64

S5_GROUP_CH = 16
S5_STATE = 64
S5_WIDTH = D_MODEL // 2
S5_GROUPS = S5_WIDTH // S5_GROUP_CH
S5_DT_MIN = 1e-3
S5_DT_MAX = 1e-1

HEAD_DIM = 128
N_HEADS = D_MODEL // HEAD_DIM
N_KV_HEADS = 2
GQA_GROUP = N_HEADS // N_KV_HEADS
ATTN_WIDTH = N_HEADS * HEAD_DIM
KV_WIDTH = N_KV_HEADS * HEAD_DIM
Q_BLOCK = 128
ROPE_THETA = 10000.0
ROPE_AXIS_DIM = HEAD_DIM // 2
ROPE_AXIS_FREQS = ROPE_AXIS_DIM // 2

N_BRANCHES = 2
IN_SPLITS = (S5_WIDTH, S5_WIDTH + ATTN_WIDTH, S5_WIDTH + ATTN_WIDTH + KV_WIDTH,
             S5_WIDTH + ATTN_WIDTH + 2 * KV_WIDTH)
IN_WIDTH = S5_WIDTH + ATTN_WIDTH + 2 * KV_WIDTH + N_BRANCHES * D_MODEL

N_EXPERT_GROUPS = 4
EXPERTS_PER_GROUP = 8
N_EXPERTS = N_EXPERT_GROUPS * EXPERTS_PER_GROUP
EXPERT_TOP_K = 2
D_EXPERT = D_MODEL // 2

N_MOD = 6
NORM_EPS = 1e-6

kernel_name = "hybrid_s5_gqa_hmoe_flow_block"


def _layer_norm(x):
    xf = x.astype(jnp.float32)
    mu = jnp.mean(xf, axis=-1, keepdims=True)
    var = jnp.mean(jnp.square(xf - mu), axis=-1, keepdims=True)
    return ((xf - mu) * lax.rsqrt(var + NORM_EPS)).astype(x.dtype)


def _post_norm(x, g, b):
    return _layer_norm(x) * g + b


def _modulate(x, shift, scale):
    return _layer_norm(x) * (1.0 + scale) + shift


def _adaln(cond, w, b):
    return jax.nn.silu(cond) @ w + b


def _rms_norm(x, g):
    xf = x.astype(jnp.float32)
    return (xf * lax.rsqrt(jnp.mean(xf * xf, axis=-1, keepdims=True) + NORM_EPS)).astype(x.dtype) * g


def _axial_rope_tables(rows, dtype):
    row = jnp.repeat(jnp.arange(rows, dtype=jnp.float32), GRID_W)
    col = jnp.tile(jnp.arange(GRID_W, dtype=jnp.float32), rows)
    inv = ROPE_THETA ** (-jnp.arange(0, ROPE_AXIS_DIM, 2, dtype=jnp.float32) / ROPE_AXIS_DIM)
    ang = jnp.stack([row[:, None] * inv, col[:, None] * inv], axis=1)
    return jnp.cos(ang)[:, None].astype(dtype), jnp.sin(ang)[:, None].astype(dtype)


def _rope(x, cos, sin):
    xs = x.reshape(x.shape[:-1] + (2, 2, ROPE_AXIS_FREQS))
    x1, x2 = xs[..., 0, :], xs[..., 1, :]
    out = jnp.stack([x1 * cos - x2 * sin, x1 * sin + x2 * cos], axis=-2)
    return out.reshape(x.shape)


def _block_attention(q, k, v):
    bsz, n_q = q.shape[:2]
    n_blk = n_q // Q_BLOCK
    scale = HEAD_DIM ** -0.5
    qb = q.reshape(bsz, n_blk, Q_BLOCK, N_KV_HEADS, GQA_GROUP, HEAD_DIM).transpose(1, 0, 2, 3, 4, 5)

    def attend(q_blk):
        s = jnp.einsum('bqhgd,bkhd->bhgqk', q_blk, k).astype(jnp.float32) * scale
        p = jax.nn.softmax(s, axis=-1).astype(v.dtype)
        return jnp.einsum('bhgqk,bkhd->bqhgd', p, v)

    o = lax.map(attend, qb)
    return o.transpose(1, 0, 2, 3, 4, 5).reshape(bsz, n_q, N_HEADS * HEAD_DIM)


def _s5_discretise(a_re, a_im, log_dt, b_re, b_im):
    dt = jnp.exp(log_dt)[:, None]
    ea = jnp.exp(a_re * dt)
    ab_re, ab_im = ea * jnp.cos(a_im * dt), ea * jnp.sin(a_im * dt)
    den = a_re * a_re + a_im * a_im
    nr, ni = ab_re - 1.0, ab_im
    rr = (nr * a_re + ni * a_im) / den
    ri = (ni * a_re - nr * a_im) / den
    bb_re = rr[..., None] * b_re - ri[..., None] * b_im
    bb_im = rr[..., None] * b_im + ri[..., None] * b_re
    return ab_re, ab_im, bb_re, bb_im


def _complex_affine_combine(e1, e2):
    a1r, a1i, b1r, b1i = e1
    a2r, a2i, b2r, b2i = e2
    return (a2r * a1r - a2i * a1i, a2r * a1i + a2i * a1r,
            a2r * b1r - a2i * b1i + b2r, a2r * b1i + a2i * b1r + b2i)


def _s5_scan(u, ab_re, ab_im, bb_re, bb_im, reverse, init=None):
    bsz, n = u.shape[:2]
    ug = u.reshape(bsz, n, S5_GROUPS, S5_GROUP_CH)
    bu_re = jnp.einsum('gpc,blgc->blgp', bb_re, ug)
    bu_im = jnp.einsum('gpc,blgc->blgp', bb_im, ug)
    a_re = jnp.broadcast_to(ab_re, (1, n) + ab_re.shape)
    a_im = jnp.broadcast_to(ab_im, (1, n) + ab_im.shape)
    acum_re, acum_im, h_re, h_im = lax.associative_scan(
        _complex_affine_combine, (a_re, a_im, bu_re, bu_im), reverse=reverse, axis=1)
    if init is not None:
        s_re, s_im = init
        h_re = h_re + acum_re * s_re - acum_im * s_im
        h_im = h_im + acum_re * s_im + acum_im * s_re
    return h_re, h_im


def _s5_readout(h_re, h_im, c_re, c_im):
    bsz, n = h_re.shape[:2]
    y = jnp.einsum('gcp,blgp->blgc', c_re, h_re) - jnp.einsum('gcp,blgp->blgc', c_im, h_im)
    return y.reshape(bsz, n, S5_WIDTH)


def _merge(y_ssm, o, gate_logits, w_glu_a, w_glu_b, w_attn_o, w_out):
    g = jax.nn.gelu(y_ssm)
    ssm_branch = (g @ w_glu_a) * jax.nn.sigmoid(g @ w_glu_b)
    attn_branch = o @ w_attn_o
    gate_ssm, gate_attn = jnp.split(jax.nn.sigmoid(gate_logits), N_BRANCHES, axis=-1)
    return (gate_ssm * ssm_branch + gate_attn * attn_branch) @ w_out


def _hier_moe(h, w_rg, b_rg, w_re, b_re, w_gate, w_up, w_down):
    shape = h.shape
    hf = h.reshape(-1, shape[-1])
    n_tok = hf.shape[0]
    g_logits = (hf @ w_rg + b_rg).astype(jnp.float32)
    g_idx = jnp.argmax(g_logits, axis=-1)
    g_w = jnp.take_along_axis(jax.nn.softmax(g_logits, axis=-1), g_idx[:, None], axis=-1)
    e_logits = (hf @ w_re + b_re).astype(jnp.float32).reshape(n_tok, N_EXPERT_GROUPS, EXPERTS_PER_GROUP)
    e_in = jnp.take_along_axis(e_logits, g_idx[:, None, None], axis=1)[:, 0]
    top_v, top_i = lax.top_k(e_in, EXPERT_TOP_K)
    top_w = jax.nn.softmax(top_v, axis=-1) * g_w
    expert_id = g_idx[:, None] * EXPERTS_PER_GROUP + top_i
    combine = jnp.sum(jax.nn.one_hot(expert_id, N_EXPERTS, dtype=jnp.float32) * top_w[..., None],
                      axis=1).astype(h.dtype)
    y = jnp.zeros_like(hf)
    for gi in range(N_EXPERT_GROUPS):
        sl = slice(gi * EXPERTS_PER_GROUP, (gi + 1) * EXPERTS_PER_GROUP)
        a = jnp.einsum('td,edf->tef', hf, w_gate[sl])
        b = jnp.einsum('td,edf->tef', hf, w_up[sl])
        y = y + jnp.einsum('tef,efd->td', jax.nn.silu(a) * b * combine[:, sl, None], w_down[sl])
    return y.reshape(shape)


def setup_inputs(seed: int = 0) -> dict:
    key = jax.random.key(seed)
    ks = list(jax.random.split(key, 40))
    f32 = jnp.float32

    def nrm(shape):
        return jax.random.normal(ks.pop(), shape, f32)

    beta = (8.0 * DEPTH) ** -0.25
    L = DEPTH
    n_idx = jnp.arange(S5_STATE, dtype=f32)
    log_dt = jax.random.uniform(ks.pop(), (L, 2, S5_GROUPS), f32, math.log(S5_DT_MIN), math.log(S5_DT_MAX))
    return {
        "x": nrm((BATCH, SEQ, D_MODEL)),
        "c": nrm((BATCH, D_MODEL)),
        "ctx": nrm((BATCH, CTX_LEN, D_MODEL)),
        "c_ctx": nrm((D_MODEL,)),
        "w_mod": nrm((L, D_MODEL, N_MOD * D_MODEL)) * (0.5 * D_MODEL ** -0.5),
        "b_mod": 0.02 * nrm((L, N_MOD * D_MODEL)),
        "w_in": nrm((L, D_MODEL, IN_WIDTH)) * D_MODEL ** -0.5,
        "s5_a_re": -0.5 + 0.01 * nrm((L, 2, S5_GROUPS, S5_STATE)),
        "s5_a_im": math.pi * n_idx + 0.01 * nrm((L, 2, S5_GROUPS, S5_STATE)),
        "s5_log_dt": log_dt,
        "s5_b_re": nrm((L, 2, S5_GROUPS, S5_STATE, S5_GROUP_CH)) * (2.0 * S5_GROUP_CH) ** -0.5,
        "s5_b_im": nrm((L, 2, S5_GROUPS, S5_STATE, S5_GROUP_CH)) * (2.0 * S5_GROUP_CH) ** -0.5,
        "s5_c_re": nrm((L, 2, S5_GROUPS, S5_GROUP_CH, S5_STATE)) * S5_STATE ** -0.5,
        "s5_c_im": nrm((L, 2, S5_GROUPS, S5_GROUP_CH, S5_STATE)) * S5_STATE ** -0.5,
        "s5_d": nrm((L, S5_WIDTH)),
        "w_glu_a": nrm((L, S5_WIDTH, D_MODEL)) * S5_WIDTH ** -0.5,
        "w_glu_b": nrm((L, S5_WIDTH, D_MODEL)) * S5_WIDTH ** -0.5,
        "q_gain": 1.0 + 0.02 * nrm((L, HEAD_DIM)),
        "k_gain": 1.0 + 0.02 * nrm((L, HEAD_DIM)),
        "w_attn_o": nrm((L, ATTN_WIDTH, D_MODEL)) * ATTN_WIDTH ** -0.5,
        "w_out": nrm((L, D_MODEL, D_MODEL)) * (beta * D_MODEL ** -0.5),
        "ln1_g": 1.0 + 0.02 * nrm((L, D_MODEL)),
        "ln1_b": 0.02 * nrm((L, D_MODEL)),
        "w_router_group": nrm((L, D_MODEL, N_EXPERT_GROUPS)) * D_MODEL ** -0.5,
        "b_router_group": 0.01 * nrm((L, N_EXPERT_GROUPS)),
        "w_router_expert": nrm((L, D_MODEL, N_EXPERTS)) * D_MODEL ** -0.5,
        "b_router_expert": 0.01 * nrm((L, N_EXPERTS)),
        "w_exp_gate": nrm((L, N_EXPERTS, D_MODEL, D_EXPERT)) * D_MODEL ** -0.5,
        "w_exp_up": nrm((L, N_EXPERTS, D_MODEL, D_EXPERT)) * D_MODEL ** -0.5,
        "w_exp_down": nrm((L, N_EXPERTS, D_EXPERT, D_MODEL)) * (beta * D_EXPERT ** -0.5),
        "ln2_g": 1.0 + 0.02 * nrm((L, D_MODEL)),
        "ln2_b": 0.02 * nrm((L, D_MODEL)),
    }


def reference(x, c, ctx, c_ctx, w_mod, b_mod, w_in, s5_a_re, s5_a_im, s5_log_dt, s5_b_re, s5_b_im,
              s5_c_re, s5_c_im, s5_d, w_glu_a, w_glu_b, q_gain, k_gain, w_attn_o, w_out, ln1_g, ln1_b,
              w_router_group, b_router_group, w_router_expert, b_router_expert, w_exp_gate, w_exp_up,
              w_exp_down, ln2_g, ln2_b):
    bsz, n_lat, _ = x.shape
    n_ctx = ctx.shape[1]
    rows = n_lat // GRID_W
    cos, sin = _axial_rope_tables(rows, x.dtype)
    alpha = (2.0 * DEPTH) ** 0.25
    for i in range(DEPTH):
        is_last = i == DEPTH - 1
        sh1, sc1, g1, sh2, sc2, g2 = jnp.split(_adaln(c, w_mod[i], b_mod[i])[:, None, :], N_MOD, axis=-1)
        sh1c, sc1c, g1c, sh2c, sc2c, g2c = jnp.split(_adaln(c_ctx, w_mod[i], b_mod[i]), N_MOD, axis=-1)

        h = _modulate(x, sh1, sc1)
        hc = _modulate(ctx, sh1c, sc1c)
        u, q, k, v, gate_logits = jnp.split(h @ w_in[i], IN_SPLITS, axis=-1)
        uc, qc, kc, vc, gate_logits_c = jnp.split(hc @ w_in[i], IN_SPLITS, axis=-1)

        y_ssm = u * s5_d[i]
        ctx_states = []
        for d, reverse in enumerate((False, True)):
            disc = _s5_discretise(s5_a_re[i, d], s5_a_im[i, d], s5_log_dt[i, d], s5_b_re[i, d], s5_b_im[i, d])
            hc_re, hc_im = _s5_scan(uc, *disc, reverse=reverse)
            edge = slice(0, 1) if reverse else slice(n_ctx - 1, n_ctx)
            hl_re, hl_im = _s5_scan(u, *disc, reverse=reverse, init=(hc_re[:, edge], hc_im[:, edge]))
            y_ssm = y_ssm + _s5_readout(hl_re, hl_im, s5_c_re[i, d], s5_c_im[i, d])
            if not is_last:
                ctx_states.append((hc_re, hc_im))

        q = _rope(_rms_norm(q.reshape(bsz, n_lat, N_HEADS, HEAD_DIM), q_gain[i]), cos, sin)
        k = _rope(_rms_norm(k.reshape(bsz, n_lat, N_KV_HEADS, HEAD_DIM), k_gain[i]), cos, sin)
        v = v.reshape(bsz, n_lat, N_KV_HEADS, HEAD_DIM)
        kc = _rms_norm(kc.reshape(bsz, n_ctx, N_KV_HEADS, HEAD_DIM), k_gain[i])
        vc = vc.reshape(bsz, n_ctx, N_KV_HEADS, HEAD_DIM)
        o = _block_attention(q, jnp.concatenate([k, kc], axis=1), jnp.concatenate([v, vc], axis=1))

        mix = _merge(y_ssm, o, gate_logits, w_glu_a[i], w_glu_b[i], w_attn_o[i], w_out[i])
        x = _post_norm(alpha * x + g1 * mix, ln1_g[i], ln1_b[i])

        if not is_last:
            y_ssm_c = uc * s5_d[i]
            for d, (hc_re, hc_im) in enumerate(ctx_states):
                y_ssm_c = y_ssm_c + _s5_readout(hc_re, hc_im, s5_c_re[i, d], s5_c_im[i, d])
            qc = _rms_norm(qc.reshape(bsz, n_ctx, N_HEADS, HEAD_DIM), q_gain[i])
            oc = _block_attention(qc, kc, vc)
            mix_c = _merge(y_ssm_c, oc, gate_logits_c, w_glu_a[i], w_glu_b[i], w_attn_o[i], w_out[i])
            ctx = _post_norm(alpha * ctx + g1c * mix_c, ln1_g[i], ln1_b[i])

        moe_w = (w_router_group[i], b_router_group[i], w_router_expert[i], b_router_expert[i],
                 w_exp_gate[i], w_exp_up[i], w_exp_down[i])
        x = _post_norm(alpha * x + g2 * _hier_moe(_modulate(x, sh2, sc2), *moe_w), ln2_g[i], ln2_b[i])
        if not is_last:
            ctx = _post_norm(alpha * ctx + g2c * _hier_moe(_modulate(ctx, sh2c, sc2c), *moe_w),
                             ln2_g[i], ln2_b[i])
    return x
```

```python
import functools
import math

import jax
import jax.numpy as jnp
from jax import lax
from jax.experimental import pallas as pl
from jax.experimental.pallas import tpu as pltpu

GRID_W = 64
S5_GROUP_CH = 16
S5_STATE = 64
HEAD_DIM = 128
N_KV_HEADS = 2
ROPE_THETA = 10000.0
N_EXPERT_GROUPS = 4
EXPERTS_PER_GROUP = 8
NORM_EPS = 1e-6
DEPTH = 1

S5_CHUNK = 16
S5_GROUP_BLOCK = 8
LANE = 128
VMEM_LIMIT = 56 * 1024 * 1024

F32 = jnp.float32
BF16 = jnp.bfloat16


def _layer_norm(x):
    mu = jnp.mean(x, axis=-1, keepdims=True)
    xc = x - mu
    var = jnp.mean(xc * xc, axis=-1, keepdims=True)
    return xc * lax.rsqrt(var + NORM_EPS)


def _split_bf16(a):
    hi = a.astype(BF16)
    lo = (a - hi.astype(F32)).astype(BF16)
    return hi, lo


def _dot(a, b):
    return jnp.dot(a, b, preferred_element_type=F32)


def _dot3(a_hi, a_lo, b_hi, b_lo):
    return _dot(a_hi, b_hi) + _dot(a_hi, b_lo) + _dot(a_lo, b_hi)


def _adaln_kernel(c_ref, w_ref, b_ref, o_ref):
    c = c_ref[...]
    s = c * jax.nn.sigmoid(c)
    s_hi, s_lo = _split_bf16(s)
    w_hi, w_lo = _split_bf16(w_ref[...])
    o_ref[...] = _dot3(s_hi, s_lo, w_hi, w_lo) + b_ref[...]


def _adaln(cond, w, b):
    rows, d = cond.shape
    n = w.shape[1]
    tn = min(n, 1024)
    return pl.pallas_call(
        _adaln_kernel,
        grid=(n // tn,),
        in_specs=[pl.BlockSpec((rows, d), lambda j: (0, 0)),
                  pl.BlockSpec((d, tn), lambda j: (0, j)),
                  pl.BlockSpec((1, tn), lambda j: (0, j))],
        out_specs=pl.BlockSpec((rows, tn), lambda j: (0, j)),
        out_shape=jax.ShapeDtypeStruct((rows, n), F32),
        compiler_params=pltpu.CompilerParams(vmem_limit_bytes=VMEM_LIMIT),
        name="adaln",
    )(cond, w, b.reshape(1, n))


def _rms_rope(t, gain, cos, sin_signed, first_half, scale):
    r = lax.rsqrt(jnp.mean(t * t, axis=-1, keepdims=True) + NORM_EPS)
    tn = t * r * gain
    if cos is not None:
        partner = jnp.where(first_half, pltpu.roll(tn, HEAD_DIM - 32, 1), pltpu.roll(tn, 32, 1))
        tn = tn * cos + partner * sin_signed
    if scale != 1.0:
        tn = tn * scale
    return tn


def _inproj_kernel(x_ref, mod_ref, w_ref, cos_ref, sin_ref, qg_ref, kg_ref, *out_refs,
                   d_s5, d_q, d_kv, latent):
    x = x_ref[0]
    shift = mod_ref[0, 0:1, :]
    scale = mod_ref[0, 1:2, :]
    h = (_layer_norm(x) * (1.0 + scale) + shift).astype(BF16)
    o_q = d_s5
    o_k = o_q + d_q
    o_v = o_k + d_kv
    o_g = o_v + d_kv
    if latent:
        u_ref, q_ref, k_ref, v_ref, g_ref = out_refs
        cos = cos_ref[...]
        sin = sin_ref[...]
        lane = lax.broadcasted_iota(jnp.int32, cos.shape, 1)
        first_half = (lane % 64) < 32
    else:
        u_ref, k_ref, v_ref = out_refs
        cos = sin = first_half = None
    u_ref[0] = _dot(h, w_ref[:, 0:d_s5])
    if latent:
        q = _dot(h, w_ref[:, o_q:o_k])
        q_scale = HEAD_DIM ** -0.5
        for hd in range(d_q // HEAD_DIM):
            sl = slice(hd * HEAD_DIM, (hd + 1) * HEAD_DIM)
            q_ref[0, :, sl] = _rms_rope(q[:, sl], qg_ref[...], cos, sin, first_half, q_scale).astype(BF16)
    k = _dot(h, w_ref[:, o_k:o_v])
    for hd in range(d_kv // HEAD_DIM):
        sl = slice(hd * HEAD_DIM, (hd + 1) * HEAD_DIM)
        k_ref[0, :, sl] = _rms_rope(k[:, sl], kg_ref[...], cos, sin, first_half, 1.0).astype(BF16)
    v_ref[0] = _dot(h, w_ref[:, o_v:o_g]).astype(BF16)
    if latent:
        g_ref[0] = jax.nn.sigmoid(_dot(h, w_ref[:, o_g:])).astype(BF16)


def _inproj(x, mod, w_in, cos, sin, q_gain, k_gain, *, d_s5, d_q, d_kv, latent):
    bsz, n, d = x.shape
    n_in = w_in.shape[1]
    d_gate = n_in - d_s5 - d_q - 2 * d_kv
    tm = min(n, 512)
    row = lambda w: pl.BlockSpec((1, tm, w), lambda b, i: (b, i, 0))
    out_shape = [jax.ShapeDtypeStruct((bsz, n, d_s5), F32)]
    out_specs = [row(d_s5)]
    if latent:
        out_shape.append(jax.ShapeDtypeStruct((bsz, n, d_q), BF16))
        out_specs.append(row(d_q))
    out_shape += [jax.ShapeDtypeStruct((bsz, n, d_kv), BF16)] * 2
    out_specs += [row(d_kv)] * 2
    if latent:
        out_shape.append(jax.ShapeDtypeStruct((bsz, n, d_gate), BF16))
        out_specs.append(row(d_gate))
    return pl.pallas_call(
        functools.partial(_inproj_kernel, d_s5=d_s5, d_q=d_q, d_kv=d_kv, latent=latent),
        grid=(bsz, n // tm),
        in_specs=[row(d),
                  pl.BlockSpec((1, 2, d), lambda b, i: (b, 0, 0)),
                  pl.BlockSpec((d, n_in), lambda b, i: (0, 0), pipeline_mode=pl.Buffered(1)),
                  pl.BlockSpec((tm, HEAD_DIM), lambda b, i: (i, 0)),
                  pl.BlockSpec((tm, HEAD_DIM), lambda b, i: (i, 0)),
                  pl.BlockSpec((1, HEAD_DIM), lambda b, i: (0, 0)),
                  pl.BlockSpec((1, HEAD_DIM), lambda b, i: (0, 0))],
        out_specs=out_specs,
        out_shape=out_shape,
        compiler_params=pltpu.CompilerParams(vmem_limit_bytes=VMEM_LIMIT),
        name="inproj_latent" if latent else "inproj_context",
    )(x, mod, w_in, cos, sin, q_gain, k_gain)


def _rope_tables(n_lat):
    rows = n_lat // GRID_W
    row = jnp.repeat(jnp.arange(rows, dtype=F32), GRID_W)
    col = jnp.tile(jnp.arange(GRID_W, dtype=F32), rows)
    axis_dim = HEAD_DIM // 2
    inv = ROPE_THETA ** (-jnp.arange(0, axis_dim, 2, dtype=F32) / axis_dim)
    ang_r = row[:, None] * inv
    ang_c = col[:, None] * inv
    cos = jnp.concatenate([jnp.cos(ang_r)] * 2 + [jnp.cos(ang_c)] * 2, axis=1)
    sin = jnp.concatenate([-jnp.sin(ang_r), jnp.sin(ang_r), -jnp.sin(ang_c), jnp.sin(ang_c)], axis=1)
    return cos, sin


def _s5_weights(a_re, a_im, log_dt, b_re, b_im, c_re, c_im):
    hp = lax.Precision.HIGHEST
    lc = S5_CHUNK
    dt = jnp.exp(log_dt)[..., None]
    lam_re, lam_im = a_re * dt, a_im * dt
    ea = jnp.exp(lam_re)
    ab_re, ab_im = ea * jnp.cos(lam_im), ea * jnp.sin(lam_im)
    den = a_re * a_re + a_im * a_im
    nr, ni = ab_re - 1.0, ab_im
    rr = (nr * a_re + ni * a_im) / den
    ri = (ni * a_re - nr * a_im) / den
    bb_re = rr[..., None] * b_re - ri[..., None] * b_im
    bb_im = rr[..., None] * b_im + ri[..., None] * b_re
    kk = jnp.arange(lc + 1, dtype=F32)[:, None, None, None]
    pk_mag = jnp.exp(kk * lam_re)
    pk_re, pk_im = pk_mag * jnp.cos(kk * lam_im), pk_mag * jnp.sin(kk * lam_im)
    akb_re = pk_re[:lc, ..., None] * bb_re - pk_im[:lc, ..., None] * bb_im
    akb_im = pk_re[:lc, ..., None] * bb_im + pk_im[:lc, ..., None] * bb_re
    kern = (jnp.einsum('dgop,kdgpi->kdgoi', c_re, akb_re, precision=hp)
            - jnp.einsum('dgop,kdgpi->kdgoi', c_im, akb_im, precision=hp))
    s_idx = jnp.arange(lc)[:, None]
    t_idx = jnp.arange(lc)[None, :]
    lag_f = jnp.clip(t_idx - s_idx, 0, lc - 1)
    lag_r = jnp.clip(s_idx - t_idx, 0, lc - 1)
    tf = jnp.where((t_idx >= s_idx)[..., None, None, None], kern[lag_f, 0], 0.0)
    tr = jnp.where((s_idx >= t_idx)[..., None, None, None], kern[lag_r, 1], 0.0)
    n_g = a_re.shape[1]
    w_t = (tf + tr).transpose(2, 0, 4, 1, 3).reshape(n_g, lc * S5_GROUP_CH, lc * S5_GROUP_CH)
    zf_re, zf_im = akb_re[::-1, 0], akb_im[::-1, 0]
    zr_re, zr_im = akb_re[:, 1], akb_im[:, 1]
    w_z = jnp.concatenate([zf_re, zr_re, zf_im, zr_im], axis=2)
    w_z = w_z.transpose(1, 0, 3, 2).reshape(n_g, lc * S5_GROUP_CH, 4 * S5_STATE)
    pf_re, pf_im = pk_re[1:, 0], pk_im[1:, 0]
    pr_re, pr_im = pk_re[lc:0:-1, 1], pk_im[lc:0:-1, 1]

    def cpow(cr, ci, pr, pi):
        return cr[None] * pr[:, :, None, :] - ci[None] * pi[:, :, None, :], \
               cr[None] * pi[:, :, None, :] + ci[None] * pr[:, :, None, :]

    cf_re, cf_im = cpow(c_re[0], c_im[0], pf_re, pf_im)
    cr_re, cr_im = cpow(c_re[1], c_im[1], pr_re, pr_im)
    w_c = jnp.concatenate([cf_re, cr_re, -cf_im, -cr_im], axis=3)
    w_c = w_c.transpose(1, 3, 0, 2).reshape(n_g, 4 * S5_STATE, lc * S5_GROUP_CH)
    a_chunk = jnp.concatenate([pk_re[lc, 0], pk_re[lc, 1], pk_im[lc, 0], pk_im[lc, 1]], axis=1)
    return w_t.astype(BF16), w_z.astype(BF16), w_c.astype(BF16), a_chunk


def _s5_kernel(ul_ref, uc_ref, wt_ref, wz_ref, wc_ref, a_ref, d_ref, y_ref,
               zl_re, zl_im, zc_re, zc_im, *, nb, ncl, ncc):
    gb = ul_ref.shape[0]
    nl = nb * ncl
    ncx = nb * ncc
    half = 2 * S5_STATE
    for g in range(gb):
        zl = _dot(ul_ref[g].astype(BF16), wz_ref[g])
        zl_re[g * nl:(g + 1) * nl, :] = zl[:, :half]
        zl_im[g * nl:(g + 1) * nl, :] = zl[:, half:]
        zc = _dot(uc_ref[g].astype(BF16), wz_ref[g])
        zc_re[g * ncx:(g + 1) * ncx, :] = zc[:, :half]
        zc_im[g * ncx:(g + 1) * ncx, :] = zc[:, half:]
    a_re = a_ref[:, :half]
    a_im = a_ref[:, half:]
    fwd = lax.broadcasted_iota(jnp.int32, (gb, half), 1) < S5_STATE

    def advance(h_re, h_im, z_re, z_im):
        return a_re * h_re - a_im * h_im + z_re, a_re * h_im + a_im * h_re + z_im

    def ctx_step(i, carry):
        out = []
        for b in range(nb):
            sl_f = pl.ds(b * ncc + i, gb, stride=ncx)
            sl_r = pl.ds(b * ncc + ncc - 1 - i, gb, stride=ncx)
            z_re = jnp.where(fwd, zc_re[sl_f, :], zc_re[sl_r, :])
            z_im = jnp.where(fwd, zc_im[sl_f, :], zc_im[sl_r, :])
            out.extend(advance(carry[2 * b], carry[2 * b + 1], z_re, z_im))
        return tuple(out)

    def lat_step(i, carry):
        out = []
        for b in range(nb):
            h_re, h_im = carry[2 * b], carry[2 * b + 1]
            sl_f = pl.ds(b * ncl + i, gb, stride=nl)
            sl_r = pl.ds(b * ncl + ncl - 1 - i, gb, stride=nl)
            f_re, f_im, r_re, r_im = zl_re[sl_f, :], zl_im[sl_f, :], zl_re[sl_r, :], zl_im[sl_r, :]
            zl_re[sl_f, :] = jnp.where(fwd, h_re, f_re)
            zl_im[sl_f, :] = jnp.where(fwd, h_im, f_im)
            zl_re[sl_r, :] = jnp.where(fwd, r_re, h_re)
            zl_im[sl_r, :] = jnp.where(fwd, r_im, h_im)
            out.extend(advance(h_re, h_im, jnp.where(fwd, f_re, r_re), jnp.where(fwd, f_im, r_im)))
        return tuple(out)

    zero = jnp.zeros((gb, half), F32)
    carry = lax.fori_loop(0, ncc, ctx_step, (zero,) * (2 * nb))
    lax.fori_loop(0, ncl, lat_step, carry)
    for g in range(gb):
        ug = ul_ref[g]
        rows = slice(g * nl, (g + 1) * nl)
        h_in = jnp.concatenate([zl_re[rows, :], zl_im[rows, :]], axis=1).astype(BF16)
        y_ref[g] = _dot(ug.astype(BF16), wt_ref[g]) + _dot(h_in, wc_ref[g]) + ug * d_ref[g:g + 1, :]


def _s5_branch(u, uc, w_t, w_z, w_c, a_chunk, s5_d):
    bsz, n, width = u.shape
    n_ctx = uc.shape[1]
    lc, ch = S5_CHUNK, S5_GROUP_CH
    n_g = width // ch
    ncl, ncc = n // lc, n_ctx // lc
    assert ncl % 2 == 0 and n_g % S5_GROUP_BLOCK == 0
    gb = S5_GROUP_BLOCK
    kw = lc * ch

    def to_chunks(t, nc):
        return t.reshape(bsz, nc, lc, n_g, ch).transpose(3, 0, 1, 2, 4).reshape(n_g, bsz * nc, kw)

    ul, ucx = to_chunks(u, ncl), to_chunks(uc, ncc)
    d_t = jnp.tile(s5_d.reshape(n_g, 1, ch), (1, lc, 1)).reshape(n_g, kw)
    nl, ncx = bsz * ncl, bsz * ncc
    blk3 = lambda r, c: pl.BlockSpec((gb, r, c), lambda i: (i, 0, 0))
    y = pl.pallas_call(
        functools.partial(_s5_kernel, nb=bsz, ncl=ncl, ncc=ncc),
        grid=(n_g // gb,),
        in_specs=[blk3(nl, kw), blk3(ncx, kw), blk3(kw, kw), blk3(kw, 4 * S5_STATE),
                  blk3(4 * S5_STATE, kw),
                  pl.BlockSpec((gb, 4 * S5_STATE), lambda i: (i, 0)),
                  pl.BlockSpec((gb, kw), lambda i: (i, 0))],
        out_specs=blk3(nl, kw),
        out_shape=jax.ShapeDtypeStruct((n_g, nl, kw), F32),
        scratch_shapes=[pltpu.VMEM((gb * nl, 2 * S5_STATE), F32), pltpu.VMEM((gb * nl, 2 * S5_STATE), F32),
                        pltpu.VMEM((gb * ncx, 2 * S5_STATE), F32), pltpu.VMEM((gb * ncx, 2 * S5_STATE), F32)],
        compiler_params=pltpu.CompilerParams(vmem_limit_bytes=VMEM_LIMIT),
        name="s5_chunked_scan",
    )(ul, ucx, w_t, w_z, w_c, a_chunk, d_t)
    return y.reshape(n_g, bsz, ncl, lc, ch).transpose(1, 2, 3, 0, 4).reshape(bsz, n, width)


def _attn_kernel(q_ref, k_ref, v_ref, o_ref, *, tk, group):
    tq = q_ref.shape[1]
    n_k = k_ref.shape[1]
    q = jnp.concatenate([q_ref[0, :, h * HEAD_DIM:(h + 1) * HEAD_DIM] for h in range(group)], axis=0)

    def body(c, carry):
        m, l, acc = carry
        start = pl.multiple_of(c * tk, tk)
        ks = k_ref[0, pl.ds(start, tk), :]
        vs = v_ref[0, pl.ds(start, tk), :]
        s = lax.dot_general(q, ks, (((1,), (1,)), ((), ())), preferred_element_type=F32)
        m_new = jnp.maximum(m, jnp.max(s, axis=-1, keepdims=True))
        p = jnp.exp(s - m_new)
        alpha = jnp.exp(m - m_new)
        l = alpha * l + jnp.sum(p, axis=-1, keepdims=True)
        acc = alpha * acc + _dot(p.astype(BF16), vs)
        return m_new, l, acc

    rows = group * tq
    init = (jnp.full((rows, 1), -jnp.inf, F32), jnp.zeros((rows, 1), F32), jnp.zeros((rows, HEAD_DIM), F32))
    _, l, acc = lax.fori_loop(0, n_k // tk, body, init)
    out = acc / l
    for h in range(group):
        o_ref[0, :, h * HEAD_DIM:(h + 1) * HEAD_DIM] = out[h * tq:(h + 1) * tq].astype(o_ref.dtype)


def _pick_divisor(n, pref):
    best = LANE
    for t in range(LANE, pref + 1, LANE):
        if n % t == 0:
            best = t
    return best


def _attention(q, k, v):
    bsz, n, dq = q.shape
    n_k, dkv = k.shape[1], k.shape[2]
    n_kv = dkv // HEAD_DIM
    group = dq // dkv
    tq = min(n, 256)
    tk = _pick_divisor(n_k, 1024)
    return pl.pallas_call(
        functools.partial(_attn_kernel, tk=tk, group=group),
        grid=(bsz, n_kv, n // tq),
        in_specs=[pl.BlockSpec((1, tq, group * HEAD_DIM), lambda b, h, i: (b, i, h)),
                  pl.BlockSpec((1, n_k, HEAD_DIM), lambda b, h, i: (b, 0, h)),
                  pl.BlockSpec((1, n_k, HEAD_DIM), lambda b, h, i: (b, 0, h))],
        out_specs=pl.BlockSpec((1, tq, group * HEAD_DIM), lambda b, h, i: (b, i, h)),
        out_shape=jax.ShapeDtypeStruct((bsz, n, dq), BF16),
        compiler_params=pltpu.CompilerParams(vmem_limit_bytes=VMEM_LIMIT),
        name="gqa_flash_attention",
    )(q, k, v)


def _merge_kernel(x_ref, y_ref, o_ref, g_ref, mod_ref, wa_ref, wb_ref, wo_ref, wout_ref,
                  ln_ref, wrh_ref, wrl_ref, br_ref, x1_ref, h2_ref, lg_ref, *, alpha):
    d = x_ref.shape[2]
    gact = jax.nn.gelu(y_ref[0]).astype(BF16)
    ssm = _dot(gact, wa_ref[...]) * jax.nn.sigmoid(_dot(gact, wb_ref[...]))
    att = _dot(o_ref[0], wo_ref[...])
    gate = g_ref[0].astype(F32)
    mixed = (gate[:, :d] * ssm + gate[:, d:] * att).astype(BF16)
    mix = _dot(mixed, wout_ref[...])
    g1 = mod_ref[0, 0:1, :]
    x1 = _layer_norm(alpha * x_ref[0] + g1 * mix) * ln_ref[0:1, :] + ln_ref[1:2, :]
    x1_ref[0] = x1
    h2 = _layer_norm(x1) * (1.0 + mod_ref[0, 2:3, :]) + mod_ref[0, 1:2, :]
    h2_ref[0] = h2
    h_hi, h_lo = _split_bf16(h2)
    lg_ref[0] = _dot3(h_hi, h_lo, wrh_ref[...], wrl_ref[...]) + br_ref[...]


def _merge(x, y_ssm, o, gates, mod, w_glu_a, w_glu_b, w_attn_o, w_out, ln, wr_hi, wr_lo, br, alpha):
    bsz, n, d = x.shape
    tm = min(n, 512)
    row = lambda w: pl.BlockSpec((1, tm, w), lambda b, i: (b, i, 0))
    const = lambda a: pl.BlockSpec(a.shape, lambda b, i: (0,) * a.ndim)
    nr = wr_hi.shape[1]
    return pl.pallas_call(
        functools.partial(_merge_kernel, alpha=alpha),
        grid=(bsz, n // tm),
        in_specs=[row(d), row(y_ssm.shape[2]), row(o.shape[2]), row(gates.shape[2]),
                  pl.BlockSpec((1, 3, d), lambda b, i: (b, 0, 0)),
                  const(w_glu_a), const(w_glu_b), const(w_attn_o), const(w_out), const(ln),
                  const(wr_hi), const(wr_lo), const(br)],
        out_specs=[row(d), row(d), row(nr)],
        out_shape=[jax.ShapeDtypeStruct((bsz, n, d), F32), jax.ShapeDtypeStruct((bsz, n, d), F32),
                   jax.ShapeDtypeStruct((bsz, n, nr), F32)],
        compiler_params=pltpu.CompilerParams(vmem_limit_bytes=VMEM_LIMIT),
        name="merge_postnorm_router",
    )(x, y_ssm, o, gates, mod, w_glu_a, w_glu_b, w_attn_o, w_out, ln, wr_hi, wr_lo, br)


def _route_kernel(lt_ref, e_ref, w_ref):
    gl = lt_ref[0:N_EXPERT_GROUPS, :]
    gmax = jnp.max(gl, axis=0, keepdims=True)
    gi = lax.broadcasted_iota(jnp.int32, gl.shape, 0)
    gidx = jnp.min(jnp.where(gl == gmax, gi, N_EXPERT_GROUPS), axis=0, keepdims=True)
    gw = 1.0 / jnp.sum(jnp.exp(gl - gmax), axis=0, keepdims=True)
    epg = EXPERTS_PER_GROUP
    e_in = lt_ref[8:8 + epg, :]
    for g in range(1, N_EXPERT_GROUPS):
        e_in = jnp.where(gidx == g, lt_ref[8 + g * epg:8 + (g + 1) * epg, :], e_in)
    ei = lax.broadcasted_iota(jnp.int32, e_in.shape, 0)
    v0 = jnp.max(e_in, axis=0, keepdims=True)
    i0 = jnp.min(jnp.where(e_in == v0, ei, epg), axis=0, keepdims=True)
    rest = jnp.where(ei == i0, -jnp.inf, e_in)
    v1 = jnp.max(rest, axis=0, keepdims=True)
    i1 = jnp.min(jnp.where(rest == v1, ei, epg), axis=0, keepdims=True)
    t = jnp.exp(v1 - v0)
    w0 = gw / (1.0 + t)
    w1 = gw * t / (1.0 + t)
    zi = jnp.zeros_like(i0)
    e_ref[...] = jnp.concatenate([gidx * epg + i0, gidx * epg + i1] + [zi] * 6, axis=0)
    w_ref[...] = jnp.concatenate([w0, w1] + [jnp.zeros_like(w0)] * 6, axis=0)


def _route(logits_t):
    rows, n = logits_t.shape
    tn = min(n, 2048)
    return pl.pallas_call(
        _route_kernel,
        grid=(n // tn,),
        in_specs=[pl.BlockSpec((rows, tn), lambda i: (0, i))],
        out_specs=[pl.BlockSpec((8, tn), lambda i: (0, i))] * 2,
        out_shape=[jax.ShapeDtypeStruct((8, n), jnp.int32), jax.ShapeDtypeStruct((8, n), F32)],
        name="route_top2",
    )(logits_t)


def _expert_kernel(tile_e_ref, src_ref, h_hbm, wg_ref, wu_ref, wd_ref, rw_ref, y_ref, xbuf, sem):
    i = pl.program_id(0)
    tm = xbuf.shape[0]
    base = i * tm

    def row_copy(r):
        tok = src_ref[base + r]
        return pltpu.make_async_copy(h_hbm.at[pl.ds(tok, 1), :], xbuf.at[pl.ds(r, 1), :], sem)

    def issue(r, c):
        row_copy(r).start()
        return c

    def drain(r, c):
        row_copy(r).wait()
        return c

    lax.fori_loop(0, tm, issue, 0)
    lax.fori_loop(0, tm, drain, 0)
    xb = xbuf[...].astype(BF16)
    a = _dot(xb, wg_ref[0].astype(BF16))
    b = _dot(xb, wu_ref[0].astype(BF16))
    mid = (a * jax.nn.sigmoid(a) * b * rw_ref[...]).astype(BF16)
    y_ref[...] = _dot(mid, wd_ref[0].astype(BF16))


def _experts(h2, tile_expert, src_tok, row_w, w_gate, w_up, w_down, tm):
    n_rows = src_tok.shape[0]
    d = h2.shape[1]
    d_e = w_gate.shape[2]
    grid_spec = pltpu.PrefetchScalarGridSpec(
        num_scalar_prefetch=2,
        grid=(n_rows // tm,),
        in_specs=[pl.BlockSpec(memory_space=pl.ANY),
                  pl.BlockSpec((1, d, d_e), lambda i, te, st: (te[i], 0, 0)),
                  pl.BlockSpec((1, d, d_e), lambda i, te, st: (te[i], 0, 0)),
                  pl.BlockSpec((1, d_e, d), lambda i, te, st: (te[i], 0, 0)),
                  pl.BlockSpec((tm, 1), lambda i, te, st: (i, 0))],
        out_specs=pl.BlockSpec((tm, d), lambda i, te, st: (i, 0)),
        scratch_shapes=[pltpu.VMEM((tm, d), F32), pltpu.SemaphoreType.DMA(())],
    )
    return pl.pallas_call(
        _expert_kernel,
        grid_spec=grid_spec,
        out_shape=jax.ShapeDtypeStruct((n_rows, d), F32),
        compiler_params=pltpu.CompilerParams(vmem_limit_bytes=VMEM_LIMIT),
        name="routed_experts",
    )(tile_expert, src_tok, h2, w_gate, w_up, w_down, row_w)


def _final_kernel(pos_ref, x1_ref, mod_ref, ln_ref, y_hbm, o_ref, ybuf, sem, *, alpha, n_tok):
    i = pl.program_id(0)
    tm = x1_ref.shape[0]
    base = i * tm

    def row_copy(r, slot):
        p = pos_ref[slot * n_tok + base + r]
        return pltpu.make_async_copy(y_hbm.at[pl.ds(p, 1), :], ybuf.at[slot, pl.ds(r, 1), :], sem)

    def issue(r, c):
        row_copy(r, 0).start()
        row_copy(r, 1).start()
        return c

    def drain(r, c):
        row_copy(r, 0).wait()
        row_copy(r, 1).wait()
        return c

    lax.fori_loop(0, tm, issue, 0)
    lax.fori_loop(0, tm, drain, 0)
    moe = ybuf[0] + ybuf[1]
    z = alpha * x1_ref[...] + mod_ref[0] * moe
    o_ref[...] = _layer_norm(z) * ln_ref[0:1, :] + ln_ref[1:2, :]


def _final(x1, pos, y_sorted, g2, ln, alpha, seq):
    n_tok, d = x1.shape
    tm = min(seq, 256)
    per_b = seq // tm
    grid_spec = pltpu.PrefetchScalarGridSpec(
        num_scalar_prefetch=1,
        grid=(n_tok // tm,),
        in_specs=[pl.BlockSpec((tm, d), lambda i, p: (i, 0)),
                  pl.BlockSpec((1, 1, d), lambda i, p: (i // per_b, 0, 0)),
                  pl.BlockSpec((2, d), lambda i, p: (0, 0)),
                  pl.BlockSpec(memory_space=pl.ANY)],
        out_specs=pl.BlockSpec((tm, d), lambda i, p: (i, 0)),
        scratch_shapes=[pltpu.VMEM((2, tm, d), F32), pltpu.SemaphoreType.DMA(())],
    )
    return pl.pallas_call(
        functools.partial(_final_kernel, alpha=alpha, n_tok=n_tok),
        grid_spec=grid_spec,
        out_shape=jax.ShapeDtypeStruct((n_tok, d), F32),
        compiler_params=pltpu.CompilerParams(vmem_limit_bytes=VMEM_LIMIT),
        name="combine_postnorm",
    )(pos, x1, g2, ln, y_sorted)


def _dispatch_plan(eid, wts, n_experts, tm):
    n_tok = eid.shape[1]
    flat = eid.reshape(-1)
    onehot = (flat[:, None] == jnp.arange(n_experts, dtype=jnp.int32)[None, :]).astype(jnp.int32)
    csum = jnp.cumsum(onehot, axis=0)
    counts = csum[-1]
    rank = jnp.take_along_axis(csum, flat[:, None], axis=1)[:, 0] - 1
    padded = ((counts + tm - 1) // tm) * tm
    ends = jnp.cumsum(padded)
    starts = ends - padded
    pos = starts[flat] + rank
    n_rows = 2 * n_tok + n_experts * tm
    tok = jnp.tile(jnp.arange(n_tok, dtype=jnp.int32), 2)
    src_tok = jnp.zeros((n_rows,), jnp.int32).at[pos].set(tok)
    row_w = jnp.zeros((n_rows,), F32).at[pos].set(wts.reshape(-1))
    tile_start = jnp.arange(n_rows // tm, dtype=jnp.int32) * tm
    tile_e = jnp.minimum(jnp.searchsorted(ends, tile_start, side='right'), n_experts - 1).astype(jnp.int32)
    return src_tok, row_w.reshape(n_rows, 1), tile_e, pos.astype(jnp.int32)


def kernel(x, c, ctx, c_ctx, w_mod, b_mod, w_in, s5_a_re, s5_a_im, s5_log_dt, s5_b_re, s5_b_im, s5_c_re, s5_c_im, s5_d, w_glu_a, w_glu_b, q_gain, k_gain, w_attn_o, w_out, ln1_g, ln1_b, w_router_group, b_router_group, w_router_expert, b_router_expert, w_exp_gate, w_exp_up, w_exp_down, ln2_g, ln2_b):
    bsz, n_lat, d = x.shape
    n_ctx = ctx.shape[1]
    assert w_mod.shape[0] == DEPTH == 1 and bsz + 1 <= 8
    alpha = (2.0 * DEPTH) ** 0.25
    d_s5 = s5_d.shape[1]
    d_q = w_attn_o.shape[1]
    d_kv = N_KV_HEADS * HEAD_DIM
    n_experts = w_exp_gate.shape[1]

    cond = jnp.zeros((8, d), F32).at[:bsz].set(c).at[bsz].set(c_ctx)
    mod = _adaln(cond, w_mod[0], b_mod[0]).reshape(8, 6, d)
    mod_lat = mod[:bsz]
    mod_ctx = jnp.broadcast_to(mod[bsz:bsz + 1], (bsz, 6, d))

    w_in_b = w_in[0].astype(BF16)
    cos, sin = _rope_tables(n_lat)
    qg, kg = q_gain[0].reshape(1, HEAD_DIM), k_gain[0].reshape(1, HEAD_DIM)
    dims = dict(d_s5=d_s5, d_q=d_q, d_kv=d_kv)
    u, q, k, v, gates = _inproj(x, mod_lat[:, 0:2], w_in_b, cos, sin, qg, kg, latent=True, **dims)
    uc, kc, vc = _inproj(ctx, mod_ctx[:, 0:2], w_in_b, cos[:n_ctx], sin[:n_ctx], qg, kg, latent=False, **dims)

    w_t, w_z, w_c, a_chunk = _s5_weights(s5_a_re[0], s5_a_im[0], s5_log_dt[0], s5_b_re[0], s5_b_im[0],
                                         s5_c_re[0], s5_c_im[0])
    y_ssm = _s5_branch(u, uc, w_t, w_z, w_c, a_chunk, s5_d[0])

    o = _attention(q, jnp.concatenate([k, kc], axis=1), jnp.concatenate([v, vc], axis=1))

    n_r = 8 + n_experts
    n_r_pad = ((n_r + LANE - 1) // LANE) * LANE
    w_r = jnp.zeros((d, n_r_pad), F32).at[:, :N_EXPERT_GROUPS].set(w_router_group[0])
    w_r = w_r.at[:, 8:n_r].set(w_router_expert[0])
    b_r = jnp.zeros((1, n_r_pad), F32).at[0, :N_EXPERT_GROUPS].set(b_router_group[0])
    b_r = b_r.at[0, 8:n_r].set(b_router_expert[0])
    wr_hi, wr_lo = _split_bf16(w_r)
    ln1 = jnp.stack([ln1_g[0], ln1_b[0]])
    x1, h2, logits = _merge(x, y_ssm, o, gates, mod_lat[:, 2:5], w_glu_a[0].astype(BF16),
                            w_glu_b[0].astype(BF16), w_attn_o[0].astype(BF16), w_out[0].astype(BF16),
                            ln1, wr_hi, wr_lo, b_r, alpha)

    n_tok = bsz * n_lat
    logits_t = logits.reshape(n_tok, n_r_pad)[:, :n_r].T
    eid, wts = _route(logits_t)
    tm_e = 256
    src_tok, row_w, tile_e, pos = _dispatch_plan(eid[:2], wts[:2], n_experts, tm_e)
    y_sorted = _experts(h2.reshape(n_tok, d), tile_e, src_tok, row_w, w_exp_gate[0], w_exp_up[0],
                        w_exp_down[0], tm_e)
    ln2 = jnp.stack([ln2_g[0], ln2_b[0]])
    out = _final(x1.reshape(n_tok, d), pos, y_sorted, mod_lat[:, 5:6], ln2, alpha, n_lat)
    return out.reshape(bsz, n_lat, d)
```

```python
import functools
import math

import jax
import jax.numpy as jnp
from jax import lax
from jax.experimental import pallas as pl
from jax.experimental.pallas import tpu as pltpu

GRID_W = 64
S5_GROUP_CH = 16
S5_STATE = 64
HEAD_DIM = 128
N_KV_HEADS = 2
ROPE_THETA = 10000.0
N_EXPERT_GROUPS = 4
EXPERTS_PER_GROUP = 8
NORM_EPS = 1e-6
DEPTH = 1

S5_CHUNK = 16
S5_GROUP_BLOCK = 8
LANE = 128
VMEM_LIMIT = 56 * 1024 * 1024

F32 = jnp.float32
BF16 = jnp.bfloat16


def _layer_norm(x):
    mu = jnp.mean(x, axis=-1, keepdims=True)
    xc = x - mu
    var = jnp.mean(xc * xc, axis=-1, keepdims=True)
    return xc * lax.rsqrt(var + NORM_EPS)


def _split_bf16(a):
    hi = a.astype(BF16)
    lo = (a - hi.astype(F32)).astype(BF16)
    return hi, lo


def _dot(a, b):
    return jnp.dot(a, b, preferred_element_type=F32)


def _dot3(a_hi, a_lo, b_hi, b_lo):
    return _dot(a_hi, b_hi) + _dot(a_hi, b_lo) + _dot(a_lo, b_hi)


def _adaln_kernel(c_ref, w_ref, b_ref, o_ref):
    c = c_ref[...]
    s = c * jax.nn.sigmoid(c)
    s_hi, s_lo = _split_bf16(s)
    w_hi, w_lo = _split_bf16(w_ref[...])
    o_ref[...] = _dot3(s_hi, s_lo, w_hi, w_lo) + b_ref[...]


def _adaln(cond, w, b):
    rows, d = cond.shape
    n = w.shape[1]
    tn = min(n, 1024)
    return pl.pallas_call(
        _adaln_kernel,
        grid=(n // tn,),
        in_specs=[pl.BlockSpec((rows, d), lambda j: (0, 0)),
                  pl.BlockSpec((d, tn), lambda j: (0, j)),
                  pl.BlockSpec((1, tn), lambda j: (0, j))],
        out_specs=pl.BlockSpec((rows, tn), lambda j: (0, j)),
        out_shape=jax.ShapeDtypeStruct((rows, n), F32),
        compiler_params=pltpu.CompilerParams(vmem_limit_bytes=VMEM_LIMIT),
        name="adaln",
    )(cond, w, b.reshape(1, n))


def _rms_rope(t, gain, cos, sin_signed, first_half, scale):
    r = lax.rsqrt(jnp.mean(t * t, axis=-1, keepdims=True) + NORM_EPS)
    tn = t * r * gain
    if cos is not None:
        partner = jnp.where(first_half, pltpu.roll(tn, HEAD_DIM - 32, 1), pltpu.roll(tn, 32, 1))
        tn = tn * cos + partner * sin_signed
    if scale != 1.0:
        tn = tn * scale
    return tn


def _inproj_kernel(x_ref, mod_ref, w_ref, cos_ref, sin_ref, qg_ref, kg_ref, *out_refs,
                   d_s5, d_q, d_kv, latent):
    x = x_ref[0]
    shift = mod_ref[0, 0:1, :]
    scale = mod_ref[0, 1:2, :]
    h = (_layer_norm(x) * (1.0 + scale) + shift).astype(BF16)
    o_q = d_s5
    o_k = o_q + d_q
    o_v = o_k + d_kv
    o_g = o_v + d_kv
    if latent:
        u_ref, q_ref, k_ref, v_ref, g_ref = out_refs
        cos = cos_ref[...]
        sin = sin_ref[...]
        lane = lax.broadcasted_iota(jnp.int32, cos.shape, 1)
        first_half = (lane % 64) < 32
    else:
        u_ref, k_ref, v_ref = out_refs
        cos = sin = first_half = None
    u_ref[0] = _dot(h, w_ref[:, 0:d_s5])
    if latent:
        q = _dot(h, w_ref[:, o_q:o_k])
        q_scale = HEAD_DIM ** -0.5 * math.log2(math.e)
        for hd in range(d_q // HEAD_DIM):
            sl = slice(hd * HEAD_DIM, (hd + 1) * HEAD_DIM)
            q_ref[0, :, sl] = _rms_rope(q[:, sl], qg_ref[...], cos, sin, first_half, q_scale).astype(BF16)
    k = _dot(h, w_ref[:, o_k:o_v])
    for hd in range(d_kv // HEAD_DIM):
        sl = slice(hd * HEAD_DIM, (hd + 1) * HEAD_DIM)
        k_ref[0, :, sl] = _rms_rope(k[:, sl], kg_ref[...], cos, sin, first_half, 1.0).astype(BF16)
    v_ref[0] = _dot(h, w_ref[:, o_v:o_g]).astype(BF16)
    if latent:
        g_ref[0] = jax.nn.sigmoid(_dot(h, w_ref[:, o_g:])).astype(BF16)


def _inproj(x, mod, w_in, cos, sin, q_gain, k_gain, *, d_s5, d_q, d_kv, latent):
    bsz, n, d = x.shape
    n_in = w_in.shape[1]
    d_gate = n_in - d_s5 - d_q - 2 * d_kv
    tm = min(n, 512)
    row = lambda w: pl.BlockSpec((1, tm, w), lambda b, i: (b, i, 0))
    out_shape = [jax.ShapeDtypeStruct((bsz, n, d_s5), F32)]
    out_specs = [row(d_s5)]
    if latent:
        out_shape.append(jax.ShapeDtypeStruct((bsz, n, d_q), BF16))
        out_specs.append(row(d_q))
    out_shape += [jax.ShapeDtypeStruct((bsz, n, d_kv), BF16)] * 2
    out_specs += [row(d_kv)] * 2
    if latent:
        out_shape.append(jax.ShapeDtypeStruct((bsz, n, d_gate), BF16))
        out_specs.append(row(d_gate))
    return pl.pallas_call(
        functools.partial(_inproj_kernel, d_s5=d_s5, d_q=d_q, d_kv=d_kv, latent=latent),
        grid=(bsz, n // tm),
        in_specs=[row(d),
                  pl.BlockSpec((1, 2, d), lambda b, i: (b, 0, 0)),
                  pl.BlockSpec((d, n_in), lambda b, i: (0, 0), pipeline_mode=pl.Buffered(1)),
                  pl.BlockSpec((tm, HEAD_DIM), lambda b, i: (i, 0)),
                  pl.BlockSpec((tm, HEAD_DIM), lambda b, i: (i, 0)),
                  pl.BlockSpec((1, HEAD_DIM), lambda b, i: (0, 0)),
                  pl.BlockSpec((1, HEAD_DIM), lambda b, i: (0, 0))],
        out_specs=out_specs,
        out_shape=out_shape,
        compiler_params=pltpu.CompilerParams(vmem_limit_bytes=VMEM_LIMIT),
        name="inproj_latent" if latent else "inproj_context",
    )(x, mod, w_in, cos, sin, q_gain, k_gain)


def _rope_tables(n_lat):
    rows = n_lat // GRID_W
    row = jnp.repeat(jnp.arange(rows, dtype=F32), GRID_W)
    col = jnp.tile(jnp.arange(GRID_W, dtype=F32), rows)
    axis_dim = HEAD_DIM // 2
    inv = ROPE_THETA ** (-jnp.arange(0, axis_dim, 2, dtype=F32) / axis_dim)
    ang_r = row[:, None] * inv
    ang_c = col[:, None] * inv
    cos = jnp.concatenate([jnp.cos(ang_r)] * 2 + [jnp.cos(ang_c)] * 2, axis=1)
    sin = jnp.concatenate([-jnp.sin(ang_r), jnp.sin(ang_r), -jnp.sin(ang_c), jnp.sin(ang_c)], axis=1)
    return cos, sin


def _s5_weights(a_re, a_im, log_dt, b_re, b_im, c_re, c_im):
    hp = lax.Precision.HIGHEST
    lc = S5_CHUNK
    dt = jnp.exp(log_dt)[..., None]
    lam_re, lam_im = a_re * dt, a_im * dt
    ea = jnp.exp(lam_re)
    ab_re, ab_im = ea * jnp.cos(lam_im), ea * jnp.sin(lam_im)
    den = a_re * a_re + a_im * a_im
    nr, ni = ab_re - 1.0, ab_im
    rr = (nr * a_re + ni * a_im) / den
    ri = (ni * a_re - nr * a_im) / den
    bb_re = rr[..., None] * b_re - ri[..., None] * b_im
    bb_im = rr[..., None] * b_im + ri[..., None] * b_re
    kk = jnp.arange(lc + 1, dtype=F32)[:, None, None, None]
    pk_mag = jnp.exp(kk * lam_re)
    pk_re, pk_im = pk_mag * jnp.cos(kk * lam_im), pk_mag * jnp.sin(kk * lam_im)
    akb_re = pk_re[:lc, ..., None] * bb_re - pk_im[:lc, ..., None] * bb_im
    akb_im = pk_re[:lc, ..., None] * bb_im + pk_im[:lc, ..., None] * bb_re
    kern = (jnp.einsum('dgop,kdgpi->kdgoi', c_re, akb_re, precision=hp)
            - jnp.einsum('dgop,kdgpi->kdgoi', c_im, akb_im, precision=hp))
    s_idx = jnp.arange(lc)[:, None]
    t_idx = jnp.arange(lc)[None, :]
    lag_f = jnp.clip(t_idx - s_idx, 0, lc - 1)
    lag_r = jnp.clip(s_idx - t_idx, 0, lc - 1)
    tf = jnp.where((t_idx >= s_idx)[..., None, None, None], kern[lag_f, 0], 0.0)
    tr = jnp.where((s_idx >= t_idx)[..., None, None, None], kern[lag_r, 1], 0.0)
    n_g = a_re.shape[1]
    w_t = (tf + tr).transpose(2, 0, 4, 1, 3).reshape(n_g, lc * S5_GROUP_CH, lc * S5_GROUP_CH)
    zf_re, zf_im = akb_re[::-1, 0], akb_im[::-1, 0]
    zr_re, zr_im = akb_re[:, 1], akb_im[:, 1]
    w_z = jnp.concatenate([zf_re, zr_re, zf_im, zr_im], axis=2)
    w_z = w_z.transpose(1, 0, 3, 2).reshape(n_g, lc * S5_GROUP_CH, 4 * S5_STATE)
    pf_re, pf_im = pk_re[1:, 0], pk_im[1:, 0]
    pr_re, pr_im = pk_re[lc:0:-1, 1], pk_im[lc:0:-1, 1]

    def cpow(cr, ci, pr, pi):
        return cr[None] * pr[:, :, None, :] - ci[None] * pi[:, :, None, :], \
               cr[None] * pi[:, :, None, :] + ci[None] * pr[:, :, None, :]

    cf_re, cf_im = cpow(c_re[0], c_im[0], pf_re, pf_im)
    cr_re, cr_im = cpow(c_re[1], c_im[1], pr_re, pr_im)
    w_c = jnp.concatenate([cf_re, cr_re, -cf_im, -cr_im], axis=3)
    w_c = w_c.transpose(1, 3, 0, 2).reshape(n_g, 4 * S5_STATE, lc * S5_GROUP_CH)
    a_chunk = jnp.concatenate([pk_re[lc, 0], pk_re[lc, 1], pk_im[lc, 0], pk_im[lc, 1]], axis=1)
    return w_t.astype(BF16), w_z.astype(BF16), w_c.astype(BF16), a_chunk


def _s5_kernel(ul_ref, uc_ref, wt_ref, wz_ref, wc_ref, a_ref, d_ref, y_ref,
               zl_re, zl_im, zc_re, zc_im, *, nb, ncl, ncc):
    gb = ul_ref.shape[0]
    nl = nb * ncl
    ncx = nb * ncc
    half = 2 * S5_STATE
    for g in range(gb):
        zl = _dot(ul_ref[g].astype(BF16), wz_ref[g])
        zl_re[g * nl:(g + 1) * nl, :] = zl[:, :half]
        zl_im[g * nl:(g + 1) * nl, :] = zl[:, half:]
        zc = _dot(uc_ref[g].astype(BF16), wz_ref[g])
        zc_re[g * ncx:(g + 1) * ncx, :] = zc[:, :half]
        zc_im[g * ncx:(g + 1) * ncx, :] = zc[:, half:]
    a_re = a_ref[:, :half]
    a_im = a_ref[:, half:]
    fwd = lax.broadcasted_iota(jnp.int32, (gb, half), 1) < S5_STATE

    def advance(h_re, h_im, z_re, z_im):
        return a_re * h_re - a_im * h_im + z_re, a_re * h_im + a_im * h_re + z_im

    def ctx_step(i, carry):
        out = []
        for b in range(nb):
            sl_f = pl.ds(b * ncc + i, gb, stride=ncx)
            sl_r = pl.ds(b * ncc + ncc - 1 - i, gb, stride=ncx)
            z_re = jnp.where(fwd, zc_re[sl_f, :], zc_re[sl_r, :])
            z_im = jnp.where(fwd, zc_im[sl_f, :], zc_im[sl_r, :])
            out.extend(advance(carry[2 * b], carry[2 * b + 1], z_re, z_im))
        return tuple(out)

    def lat_step(i, carry):
        out = []
        for b in range(nb):
            h_re, h_im = carry[2 * b], carry[2 * b + 1]
            sl_f = pl.ds(b * ncl + i, gb, stride=nl)
            sl_r = pl.ds(b * ncl + ncl - 1 - i, gb, stride=nl)
            f_re, f_im, r_re, r_im = zl_re[sl_f, :], zl_im[sl_f, :], zl_re[sl_r, :], zl_im[sl_r, :]
            zl_re[sl_f, :] = jnp.where(fwd, h_re, f_re)
            zl_im[sl_f, :] = jnp.where(fwd, h_im, f_im)
            zl_re[sl_r, :] = jnp.where(fwd, r_re, h_re)
            zl_im[sl_r, :] = jnp.where(fwd, r_im, h_im)
            out.extend(advance(h_re, h_im, jnp.where(fwd, f_re, r_re), jnp.where(fwd, f_im, r_im)))
        return tuple(out)

    zero = jnp.zeros((gb, half), F32)
    carry = lax.fori_loop(0, ncc, ctx_step, (zero,) * (2 * nb))
    lax.fori_loop(0, ncl, lat_step, carry)
    for g in range(gb):
        ug = ul_ref[g]
        rows = slice(g * nl, (g + 1) * nl)
        h_in = jnp.concatenate([zl_re[rows, :], zl_im[rows, :]], axis=1).astype(BF16)
        y_ref[g] = _dot(ug.astype(BF16), wt_ref[g]) + _dot(h_in, wc_ref[g]) + ug * d_ref[g:g + 1, :]


def _s5_branch(u, uc, w_t, w_z, w_c, a_chunk, s5_d):
    bsz, n, width = u.shape
    n_ctx = uc.shape[1]
    lc, ch = S5_CHUNK, S5_GROUP_CH
    n_g = width // ch
    ncl, ncc = n // lc, n_ctx // lc
    assert ncl % 2 == 0 and n_g % S5_GROUP_BLOCK == 0
    gb = S5_GROUP_BLOCK
    kw = lc * ch

    def to_chunks(t, nc):
        return t.reshape(bsz, nc, lc, n_g, ch).transpose(3, 0, 1, 2, 4).reshape(n_g, bsz * nc, kw)

    ul, ucx = to_chunks(u, ncl), to_chunks(uc, ncc)
    d_t = jnp.tile(s5_d.reshape(n_g, 1, ch), (1, lc, 1)).reshape(n_g, kw)
    nl, ncx = bsz * ncl, bsz * ncc
    blk3 = lambda r, c: pl.BlockSpec((gb, r, c), lambda i: (i, 0, 0))
    y = pl.pallas_call(
        functools.partial(_s5_kernel, nb=bsz, ncl=ncl, ncc=ncc),
        grid=(n_g // gb,),
        in_specs=[blk3(nl, kw), blk3(ncx, kw), blk3(kw, kw), blk3(kw, 4 * S5_STATE),
                  blk3(4 * S5_STATE, kw),
                  pl.BlockSpec((gb, 4 * S5_STATE), lambda i: (i, 0)),
                  pl.BlockSpec((gb, kw), lambda i: (i, 0))],
        out_specs=blk3(nl, kw),
        out_shape=jax.ShapeDtypeStruct((n_g, nl, kw), F32),
        scratch_shapes=[pltpu.VMEM((gb * nl, 2 * S5_STATE), F32), pltpu.VMEM((gb * nl, 2 * S5_STATE), F32),
                        pltpu.VMEM((gb * ncx, 2 * S5_STATE), F32), pltpu.VMEM((gb * ncx, 2 * S5_STATE), F32)],
        compiler_params=pltpu.CompilerParams(vmem_limit_bytes=VMEM_LIMIT),
        name="s5_chunked_scan",
    )(ul, ucx, w_t, w_z, w_c, a_chunk, d_t)
    return y.reshape(n_g, bsz, ncl, lc, ch).transpose(1, 2, 3, 0, 4).reshape(bsz, n, width)


def _attn_kernel(q_ref, k_ref, v_ref, o_ref, *, tk, group):
    tq = q_ref.shape[1]
    n_k = k_ref.shape[1]
    qs = [q_ref[0, :, h * HEAD_DIM:(h + 1) * HEAD_DIM] for h in range(group)]

    ones = jnp.ones((tk, HEAD_DIM), BF16)

    def body(c, carry):
        start = pl.multiple_of(c * tk, tk)
        ks = k_ref[0, pl.ds(start, tk), :]
        vs = jnp.concatenate([v_ref[0, pl.ds(start, tk), :], ones], axis=1)
        out = []
        score = lambda h: lax.dot_general(qs[h], ks, (((1,), (1,)), ((), ())), preferred_element_type=F32)
        s_next = score(0)
        for h in range(group):
            m, acc = carry[2 * h:2 * h + 2]
            s = s_next
            if h + 1 < group:
                s_next = score(h + 1)
            m_new = jnp.maximum(m, jnp.max(s, axis=-1, keepdims=True))
            p = jnp.exp2(s - m_new)
            alpha = jnp.exp2(m - m_new)
            acc = alpha * acc + _dot(p.astype(BF16), vs)
            out.extend((m_new, acc))
        return tuple(out)

    init = (jnp.full((tq, 1), -jnp.inf, F32), jnp.zeros((tq, 2 * HEAD_DIM), F32)) * group
    fin = lax.fori_loop(0, n_k // tk, body, init)
    for h in range(group):
        acc = fin[2 * h + 1]
        o_ref[0, :, h * HEAD_DIM:(h + 1) * HEAD_DIM] = (acc[:, :HEAD_DIM] / acc[:, HEAD_DIM:]).astype(o_ref.dtype)


def _pick_divisor(n, pref):
    best = LANE
    for t in range(LANE, pref + 1, LANE):
        if n % t == 0:
            best = t
    return best


def _attention(q, k, v):
    bsz, n, dq = q.shape
    n_k, dkv = k.shape[1], k.shape[2]
    n_kv = dkv // HEAD_DIM
    group = dq // dkv
    tq = min(n, 256)
    tk = _pick_divisor(n_k, 3072)
    return pl.pallas_call(
        functools.partial(_attn_kernel, tk=tk, group=group),
        grid=(bsz, n_kv, n // tq),
        in_specs=[pl.BlockSpec((1, tq, group * HEAD_DIM), lambda b, h, i: (b, i, h)),
                  pl.BlockSpec((1, n_k, HEAD_DIM), lambda b, h, i: (b, 0, h)),
                  pl.BlockSpec((1, n_k, HEAD_DIM), lambda b, h, i: (b, 0, h))],
        out_specs=pl.BlockSpec((1, tq, group * HEAD_DIM), lambda b, h, i: (b, i, h)),
        out_shape=jax.ShapeDtypeStruct((bsz, n, dq), BF16),
        compiler_params=pltpu.CompilerParams(vmem_limit_bytes=VMEM_LIMIT),
        name="gqa_flash_attention",
    )(q, k, v)


def _merge_kernel(x_ref, y_ref, o_ref, g_ref, mod_ref, wa_ref, wb_ref, wo_ref, wout_ref,
                  ln_ref, wrh_ref, wrl_ref, br_ref, x1_ref, h2_ref, lg_ref, *, alpha):
    d = x_ref.shape[2]
    gact = jax.nn.gelu(y_ref[0]).astype(BF16)
    ssm = _dot(gact, wa_ref[...]) * jax.nn.sigmoid(_dot(gact, wb_ref[...]))
    att = _dot(o_ref[0], wo_ref[...])
    gate = g_ref[0].astype(F32)
    mixed = (gate[:, :d] * ssm + gate[:, d:] * att).astype(BF16)
    mix = _dot(mixed, wout_ref[...])
    g1 = mod_ref[0, 0:1, :]
    x1 = _layer_norm(alpha * x_ref[0] + g1 * mix) * ln_ref[0:1, :] + ln_ref[1:2, :]
    x1_ref[0] = x1
    h2 = _layer_norm(x1) * (1.0 + mod_ref[0, 2:3, :]) + mod_ref[0, 1:2, :]
    h2_ref[0] = h2
    h_hi, h_lo = _split_bf16(h2)
    lg_ref[0] = _dot3(h_hi, h_lo, wrh_ref[...], wrl_ref[...]) + br_ref[...]


def _merge(x, y_ssm, o, gates, mod, w_glu_a, w_glu_b, w_attn_o, w_out, ln, wr_hi, wr_lo, br, alpha):
    bsz, n, d = x.shape
    tm = min(n, 512)
    row = lambda w: pl.BlockSpec((1, tm, w), lambda b, i: (b, i, 0))
    const = lambda a: pl.BlockSpec(a.shape, lambda b, i: (0,) * a.ndim)
    nr = wr_hi.shape[1]
    return pl.pallas_call(
        functools.partial(_merge_kernel, alpha=alpha),
        grid=(bsz, n // tm),
        in_specs=[row(d), row(y_ssm.shape[2]), row(o.shape[2]), row(gates.shape[2]),
                  pl.BlockSpec((1, 3, d), lambda b, i: (b, 0, 0)),
                  const(w_glu_a), const(w_glu_b), const(w_attn_o), const(w_out), const(ln),
                  const(wr_hi), const(wr_lo), const(br)],
        out_specs=[row(d), row(d), row(nr)],
        out_shape=[jax.ShapeDtypeStruct((bsz, n, d), F32), jax.ShapeDtypeStruct((bsz, n, d), F32),
                   jax.ShapeDtypeStruct((bsz, n, nr), F32)],
        compiler_params=pltpu.CompilerParams(vmem_limit_bytes=VMEM_LIMIT),
        name="merge_postnorm_router",
    )(x, y_ssm, o, gates, mod, w_glu_a, w_glu_b, w_attn_o, w_out, ln, wr_hi, wr_lo, br)


def _route_kernel(lt_ref, e_ref, w_ref):
    gl = lt_ref[0:N_EXPERT_GROUPS, :]
    gmax = jnp.max(gl, axis=0, keepdims=True)
    gi = lax.broadcasted_iota(jnp.int32, gl.shape, 0)
    gidx = jnp.min(jnp.where(gl == gmax, gi, N_EXPERT_GROUPS), axis=0, keepdims=True)
    gw = 1.0 / jnp.sum(jnp.exp(gl - gmax), axis=0, keepdims=True)
    epg = EXPERTS_PER_GROUP
    e_in = lt_ref[8:8 + epg, :]
    for g in range(1, N_EXPERT_GROUPS):
        e_in = jnp.where(gidx == g, lt_ref[8 + g * epg:8 + (g + 1) * epg, :], e_in)
    ei = lax.broadcasted_iota(jnp.int32, e_in.shape, 0)
    v0 = jnp.max(e_in, axis=0, keepdims=True)
    i0 = jnp.min(jnp.where(e_in == v0, ei, epg), axis=0, keepdims=True)
    rest = jnp.where(ei == i0, -jnp.inf, e_in)
    v1 = jnp.max(rest, axis=0, keepdims=True)
    i1 = jnp.min(jnp.where(rest == v1, ei, epg), axis=0, keepdims=True)
    t = jnp.exp(v1 - v0)
    w0 = gw / (1.0 + t)
    w1 = gw * t / (1.0 + t)
    zi = jnp.zeros_like(i0)
    e_ref[...] = jnp.concatenate([gidx * epg + i0, gidx * epg + i1] + [zi] * 6, axis=0)
    w_ref[...] = jnp.concatenate([w0, w1] + [jnp.zeros_like(w0)] * 6, axis=0)


def _route(logits_t):
    rows, n = logits_t.shape
    tn = min(n, 2048)
    return pl.pallas_call(
        _route_kernel,
        grid=(n // tn,),
        in_specs=[pl.BlockSpec((rows, tn), lambda i: (0, i))],
        out_specs=[pl.BlockSpec((8, tn), lambda i: (0, i))] * 2,
        out_shape=[jax.ShapeDtypeStruct((8, n), jnp.int32), jax.ShapeDtypeStruct((8, n), F32)],
        name="route_top2",
    )(logits_t)


def _expert_kernel(tile_e_ref, src_ref, h_hbm, wg_ref, wu_ref, wd_ref, rw_ref, y_ref, xbuf, sem):
    i = pl.program_id(0)
    tm = xbuf.shape[0]
    base = i * tm

    def row_copy(r):
        tok = src_ref[base + r]
        return pltpu.make_async_copy(h_hbm.at[pl.ds(tok, 1), :], xbuf.at[pl.ds(r, 1), :], sem)

    def issue(r, c):
        row_copy(r).start()
        return c

    def drain(r, c):
        row_copy(r).wait()
        return c

    lax.fori_loop(0, tm, issue, 0)
    lax.fori_loop(0, tm, drain, 0)
    xb = xbuf[...].astype(BF16)
    a = _dot(xb, wg_ref[0].astype(BF16))
    b = _dot(xb, wu_ref[0].astype(BF16))
    mid = (a * jax.nn.sigmoid(a) * b * rw_ref[...]).astype(BF16)
    y_ref[...] = _dot(mid, wd_ref[0].astype(BF16))


def _experts(h2, tile_expert, src_tok, row_w, w_gate, w_up, w_down, tm):
    n_rows = src_tok.shape[0]
    d = h2.shape[1]
    d_e = w_gate.shape[2]
    grid_spec = pltpu.PrefetchScalarGridSpec(
        num_scalar_prefetch=2,
        grid=(n_rows // tm,),
        in_specs=[pl.BlockSpec(memory_space=pl.ANY),
                  pl.BlockSpec((1, d, d_e), lambda i, te, st: (te[i], 0, 0)),
                  pl.BlockSpec((1, d, d_e), lambda i, te, st: (te[i], 0, 0)),
                  pl.BlockSpec((1, d_e, d), lambda i, te, st: (te[i], 0, 0)),
                  pl.BlockSpec((tm, 1), lambda i, te, st: (i, 0))],
        out_specs=pl.BlockSpec((tm, d), lambda i, te, st: (i, 0)),
        scratch_shapes=[pltpu.VMEM((tm, d), F32), pltpu.SemaphoreType.DMA(())],
    )
    return pl.pallas_call(
        _expert_kernel,
        grid_spec=grid_spec,
        out_shape=jax.ShapeDtypeStruct((n_rows, d), F32),
        compiler_params=pltpu.CompilerParams(vmem_limit_bytes=VMEM_LIMIT),
        name="routed_experts",
    )(tile_expert, src_tok, h2, w_gate, w_up, w_down, row_w)


def _final_kernel(pos_ref, x1_ref, mod_ref, ln_ref, y_hbm, o_ref, ybuf, sem, *, alpha, n_tok):
    i = pl.program_id(0)
    tm = x1_ref.shape[0]
    base = i * tm

    def row_copy(r, slot):
        p = pos_ref[slot * n_tok + base + r]
        return pltpu.make_async_copy(y_hbm.at[pl.ds(p, 1), :], ybuf.at[slot, pl.ds(r, 1), :], sem)

    def issue(r, c):
        row_copy(r, 0).start()
        row_copy(r, 1).start()
        return c

    def drain(r, c):
        row_copy(r, 0).wait()
        row_copy(r, 1).wait()
        return c

    lax.fori_loop(0, tm, issue, 0)
    lax.fori_loop(0, tm, drain, 0)
    moe = ybuf[0] + ybuf[1]
    z = alpha * x1_ref[...] + mod_ref[0] * moe
    o_ref[...] = _layer_norm(z) * ln_ref[0:1, :] + ln_ref[1:2, :]


def _final(x1, pos, y_sorted, g2, ln, alpha, seq):
    n_tok, d = x1.shape
    tm = min(seq, 256)
    per_b = seq // tm
    grid_spec = pltpu.PrefetchScalarGridSpec(
        num_scalar_prefetch=1,
        grid=(n_tok // tm,),
        in_specs=[pl.BlockSpec((tm, d), lambda i, p: (i, 0)),
                  pl.BlockSpec((1, 1, d), lambda i, p: (i // per_b, 0, 0)),
                  pl.BlockSpec((2, d), lambda i, p: (0, 0)),
                  pl.BlockSpec(memory_space=pl.ANY)],
        out_specs=pl.BlockSpec((tm, d), lambda i, p: (i, 0)),
        scratch_shapes=[pltpu.VMEM((2, tm, d), F32), pltpu.SemaphoreType.DMA(())],
    )
    return pl.pallas_call(
        functools.partial(_final_kernel, alpha=alpha, n_tok=n_tok),
        grid_spec=grid_spec,
        out_shape=jax.ShapeDtypeStruct((n_tok, d), F32),
        compiler_params=pltpu.CompilerParams(vmem_limit_bytes=VMEM_LIMIT),
        name="combine_postnorm",
    )(pos, x1, g2, ln, y_sorted)


def _dispatch_plan(eid, wts, n_experts, tm):
    n_tok = eid.shape[1]
    flat = eid.reshape(-1)
    onehot = (flat[:, None] == jnp.arange(n_experts, dtype=jnp.int32)[None, :]).astype(jnp.int32)
    csum = jnp.cumsum(onehot, axis=0)
    counts = csum[-1]
    rank = jnp.take_along_axis(csum, flat[:, None], axis=1)[:, 0] - 1
    padded = ((counts + tm - 1) // tm) * tm
    ends = jnp.cumsum(padded)
    starts = ends - padded
    pos = starts[flat] + rank
    n_rows = 2 * n_tok + n_experts * tm
    tok = jnp.tile(jnp.arange(n_tok, dtype=jnp.int32), 2)
    src_tok = jnp.zeros((n_rows,), jnp.int32).at[pos].set(tok)
    row_w = jnp.zeros((n_rows,), F32).at[pos].set(wts.reshape(-1))
    tile_start = jnp.arange(n_rows // tm, dtype=jnp.int32) * tm
    tile_e = jnp.sum((tile_start[:, None] >= ends[None, :]).astype(jnp.int32), axis=1)
    tile_e = jnp.minimum(tile_e, n_experts - 1)
    return src_tok, row_w.reshape(n_rows, 1), tile_e, pos.astype(jnp.int32)


def kernel(x, c, ctx, c_ctx, w_mod, b_mod, w_in, s5_a_re, s5_a_im, s5_log_dt, s5_b_re, s5_b_im, s5_c_re, s5_c_im, s5_d, w_glu_a, w_glu_b, q_gain, k_gain, w_attn_o, w_out, ln1_g, ln1_b, w_router_group, b_router_group, w_router_expert, b_router_expert, w_exp_gate, w_exp_up, w_exp_down, ln2_g, ln2_b):
    bsz, n_lat, d = x.shape
    n_ctx = ctx.shape[1]
    assert w_mod.shape[0] == DEPTH == 1 and bsz + 1 <= 8
    alpha = (2.0 * DEPTH) ** 0.25
    d_s5 = s5_d.shape[1]
    d_q = w_attn_o.shape[1]
    d_kv = N_KV_HEADS * HEAD_DIM
    n_experts = w_exp_gate.shape[1]

    cond = jnp.zeros((8, d), F32).at[:bsz].set(c).at[bsz].set(c_ctx)
    mod = _adaln(cond, w_mod[0], b_mod[0]).reshape(8, 6, d)
    mod_lat = mod[:bsz]
    mod_ctx = jnp.broadcast_to(mod[bsz:bsz + 1], (bsz, 6, d))

    w_in_b = w_in[0].astype(BF16)
    cos, sin = _rope_tables(n_lat)
    qg, kg = q_gain[0].reshape(1, HEAD_DIM), k_gain[0].reshape(1, HEAD_DIM)
    dims = dict(d_s5=d_s5, d_q=d_q, d_kv=d_kv)
    u, q, k, v, gates = _inproj(x, mod_lat[:, 0:2], w_in_b, cos, sin, qg, kg, latent=True, **dims)
    uc, kc, vc = _inproj(ctx, mod_ctx[:, 0:2], w_in_b, cos[:n_ctx], sin[:n_ctx], qg, kg, latent=False, **dims)

    w_t, w_z, w_c, a_chunk = _s5_weights(s5_a_re[0], s5_a_im[0], s5_log_dt[0], s5_b_re[0], s5_b_im[0],
                                         s5_c_re[0], s5_c_im[0])
    y_ssm = _s5_branch(u, uc, w_t, w_z, w_c, a_chunk, s5_d[0])

    o = _attention(q, jnp.concatenate([k, kc], axis=1), jnp.concatenate([v, vc], axis=1))

    n_r = 8 + n_experts
    n_r_pad = ((n_r + LANE - 1) // LANE) * LANE
    w_r = jnp.zeros((d, n_r_pad), F32).at[:, :N_EXPERT_GROUPS].set(w_router_group[0])
    w_r = w_r.at[:, 8:n_r].set(w_router_expert[0])
    b_r = jnp.zeros((1, n_r_pad), F32).at[0, :N_EXPERT_GROUPS].set(b_router_group[0])
    b_r = b_r.at[0, 8:n_r].set(b_router_expert[0])
    wr_hi, wr_lo = _split_bf16(w_r)
    ln1 = jnp.stack([ln1_g[0], ln1_b[0]])
    x1, h2, logits = _merge(x, y_ssm, o, gates, mod_lat[:, 2:5], w_glu_a[0].astype(BF16),
                            w_glu_b[0].astype(BF16), w_attn_o[0].astype(BF16), w_out[0].astype(BF16),
                            ln1, wr_hi, wr_lo, b_r, alpha)

    n_tok = bsz * n_lat
    logits_t = logits.reshape(n_tok, n_r_pad)[:, :n_r].T
    eid, wts = _route(logits_t)
    tm_e = 256
    src_tok, row_w, tile_e, pos = _dispatch_plan(eid[:2], wts[:2], n_experts, tm_e)
    y_sorted = _experts(h2.reshape(n_tok, d), tile_e, src_tok, row_w, w_exp_gate[0], w_exp_up[0],
                        w_exp_down[0], tm_e)
    ln2 = jnp.stack([ln2_g[0], ln2_b[0]])
    out = _final(x1.reshape(n_tok, d), pos, y_sorted, mod_lat[:, 5:6], ln2, alpha, n_lat)
    return out.reshape(bsz, n_lat, d)
```

```python
import functools
import math

import jax
import jax.numpy as jnp
from jax import lax
from jax.experimental import pallas as pl
from jax.experimental.pallas import tpu as pltpu

GRID_W = 64
S5_GROUP_CH = 16
S5_STATE = 64
HEAD_DIM = 128
N_KV_HEADS = 2
ROPE_THETA = 10000.0
N_EXPERT_GROUPS = 4
EXPERTS_PER_GROUP = 8
NORM_EPS = 1e-6
DEPTH = 1

S5_CHUNK = 16
S5_GROUP_BLOCK = 8
EXPERT_TILE = 256
DMA_UNROLL = 8
LANE = 128
VMEM_LIMIT = 56 * 1024 * 1024

F32 = jnp.float32
BF16 = jnp.bfloat16


def _layer_norm(x):
    mu = jnp.mean(x, axis=-1, keepdims=True)
    xc = x - mu
    var = jnp.mean(xc * xc, axis=-1, keepdims=True)
    return xc * lax.rsqrt(var + NORM_EPS)


def _split_bf16(a):
    hi = a.astype(BF16)
    lo = (a - hi.astype(F32)).astype(BF16)
    return hi, lo


def _dot(a, b):
    return jnp.dot(a, b, preferred_element_type=F32)


def _dot3(a_hi, a_lo, b_hi, b_lo):
    return _dot(a_hi, b_hi) + _dot(a_hi, b_lo) + _dot(a_lo, b_hi)


def _adaln_kernel(c_ref, w_ref, b_ref, o_ref):
    c = c_ref[...]
    s = c * jax.nn.sigmoid(c)
    s_hi, s_lo = _split_bf16(s)
    w_hi, w_lo = _split_bf16(w_ref[...])
    o_ref[...] = _dot3(s_hi, s_lo, w_hi, w_lo) + b_ref[...]


def _adaln(cond, w, b):
    rows, d = cond.shape
    n = w.shape[1]
    tn = min(n, 1024)
    return pl.pallas_call(
        _adaln_kernel,
        grid=(n // tn,),
        in_specs=[pl.BlockSpec((rows, d), lambda j: (0, 0)),
                  pl.BlockSpec((d, tn), lambda j: (0, j)),
                  pl.BlockSpec((1, tn), lambda j: (0, j))],
        out_specs=pl.BlockSpec((rows, tn), lambda j: (0, j)),
        out_shape=jax.ShapeDtypeStruct((rows, n), F32),
        compiler_params=pltpu.CompilerParams(vmem_limit_bytes=VMEM_LIMIT),
        name="adaln",
    )(cond, w, b.reshape(1, n))


def _rms_rope(t, gain, cos, sin_signed, first_half, scale):
    r = lax.rsqrt(jnp.mean(t * t, axis=-1, keepdims=True) + NORM_EPS)
    tn = t * r * gain
    if cos is not None:
        partner = jnp.where(first_half, pltpu.roll(tn, HEAD_DIM - 32, 1), pltpu.roll(tn, 32, 1))
        tn = tn * cos + partner * sin_signed
    if scale != 1.0:
        tn = tn * scale
    return tn


def _inproj_kernel(x_ref, mod_ref, w_ref, cos_ref, sin_ref, qg_ref, kg_ref, *out_refs,
                   d_s5, d_q, d_kv, latent):
    x = x_ref[0]
    shift = mod_ref[0, 0:1, :]
    scale = mod_ref[0, 1:2, :]
    h = (_layer_norm(x) * (1.0 + scale) + shift).astype(BF16)
    o_q = d_s5
    o_k = o_q + d_q
    o_v = o_k + d_kv
    o_g = o_v + d_kv
    if latent:
        u_ref, q_ref, k_ref, v_ref, g_ref = out_refs
        cos = cos_ref[...]
        sin = sin_ref[...]
        lane = lax.broadcasted_iota(jnp.int32, cos.shape, 1)
        first_half = (lane % 64) < 32
    else:
        u_ref, k_ref, v_ref = out_refs
        cos = sin = first_half = None
    u_ref[0] = _dot(h, w_ref[:, 0:d_s5])
    if latent:
        q = _dot(h, w_ref[:, o_q:o_k])
        q_scale = HEAD_DIM ** -0.5 * math.log2(math.e)
        for hd in range(d_q // HEAD_DIM):
            sl = slice(hd * HEAD_DIM, (hd + 1) * HEAD_DIM)
            q_ref[0, :, sl] = _rms_rope(q[:, sl], qg_ref[...], cos, sin, first_half, q_scale).astype(BF16)
    k = _dot(h, w_ref[:, o_k:o_v])
    for hd in range(d_kv // HEAD_DIM):
        sl = slice(hd * HEAD_DIM, (hd + 1) * HEAD_DIM)
        k_ref[0, :, sl] = _rms_rope(k[:, sl], kg_ref[...], cos, sin, first_half, 1.0).astype(BF16)
    v_ref[0] = _dot(h, w_ref[:, o_v:o_g]).astype(BF16)
    if latent:
        g_ref[0] = jax.nn.sigmoid(_dot(h, w_ref[:, o_g:])).astype(BF16)


def _inproj(x, mod, w_in, cos, sin, q_gain, k_gain, *, d_s5, d_q, d_kv, latent):
    bsz, n, d = x.shape
    n_in = w_in.shape[1]
    d_gate = n_in - d_s5 - d_q - 2 * d_kv
    tm = min(n, 512)
    row = lambda w: pl.BlockSpec((1, tm, w), lambda b, i: (b, i, 0))
    out_shape = [jax.ShapeDtypeStruct((bsz, n, d_s5), F32)]
    out_specs = [row(d_s5)]
    if latent:
        out_shape.append(jax.ShapeDtypeStruct((bsz, n, d_q), BF16))
        out_specs.append(row(d_q))
    out_shape += [jax.ShapeDtypeStruct((bsz, n, d_kv), BF16)] * 2
    out_specs += [row(d_kv)] * 2
    if latent:
        out_shape.append(jax.ShapeDtypeStruct((bsz, n, d_gate), BF16))
        out_specs.append(row(d_gate))
    return pl.pallas_call(
        functools.partial(_inproj_kernel, d_s5=d_s5, d_q=d_q, d_kv=d_kv, latent=latent),
        grid=(bsz, n // tm),
        in_specs=[row(d),
                  pl.BlockSpec((1, 2, d), lambda b, i: (b, 0, 0)),
                  pl.BlockSpec((d, n_in), lambda b, i: (0, 0), pipeline_mode=pl.Buffered(1)),
                  pl.BlockSpec((tm, HEAD_DIM), lambda b, i: (i, 0)),
                  pl.BlockSpec((tm, HEAD_DIM), lambda b, i: (i, 0)),
                  pl.BlockSpec((1, HEAD_DIM), lambda b, i: (0, 0)),
                  pl.BlockSpec((1, HEAD_DIM), lambda b, i: (0, 0))],
        out_specs=out_specs,
        out_shape=out_shape,
        compiler_params=pltpu.CompilerParams(vmem_limit_bytes=VMEM_LIMIT),
        name="inproj_latent" if latent else "inproj_context",
    )(x, mod, w_in, cos, sin, q_gain, k_gain)


def _rope_tables(n_lat):
    rows = n_lat // GRID_W
    row = jnp.repeat(jnp.arange(rows, dtype=F32), GRID_W)
    col = jnp.tile(jnp.arange(GRID_W, dtype=F32), rows)
    axis_dim = HEAD_DIM // 2
    inv = ROPE_THETA ** (-jnp.arange(0, axis_dim, 2, dtype=F32) / axis_dim)
    ang_r = row[:, None] * inv
    ang_c = col[:, None] * inv
    cos = jnp.concatenate([jnp.cos(ang_r)] * 2 + [jnp.cos(ang_c)] * 2, axis=1)
    sin = jnp.concatenate([-jnp.sin(ang_r), jnp.sin(ang_r), -jnp.sin(ang_c), jnp.sin(ang_c)], axis=1)
    return cos, sin


def _s5_weights(a_re, a_im, log_dt, b_re, b_im, c_re, c_im):
    hp = lax.Precision.HIGHEST
    lc = S5_CHUNK
    dt = jnp.exp(log_dt)[..., None]
    lam_re, lam_im = a_re * dt, a_im * dt
    ea = jnp.exp(lam_re)
    ab_re, ab_im = ea * jnp.cos(lam_im), ea * jnp.sin(lam_im)
    den = a_re * a_re + a_im * a_im
    nr, ni = ab_re - 1.0, ab_im
    rr = (nr * a_re + ni * a_im) / den
    ri = (ni * a_re - nr * a_im) / den
    bb_re = rr[..., None] * b_re - ri[..., None] * b_im
    bb_im = rr[..., None] * b_im + ri[..., None] * b_re
    kk = jnp.arange(lc + 1, dtype=F32)[:, None, None, None]
    pk_mag = jnp.exp(kk * lam_re)
    pk_re, pk_im = pk_mag * jnp.cos(kk * lam_im), pk_mag * jnp.sin(kk * lam_im)
    akb_re = pk_re[:lc, ..., None] * bb_re - pk_im[:lc, ..., None] * bb_im
    akb_im = pk_re[:lc, ..., None] * bb_im + pk_im[:lc, ..., None] * bb_re
    kern = (jnp.einsum('dgop,kdgpi->kdgoi', c_re, akb_re, precision=hp)
            - jnp.einsum('dgop,kdgpi->kdgoi', c_im, akb_im, precision=hp))
    s_idx = jnp.arange(lc)[:, None]
    t_idx = jnp.arange(lc)[None, :]
    lag_f = jnp.clip(t_idx - s_idx, 0, lc - 1)
    lag_r = jnp.clip(s_idx - t_idx, 0, lc - 1)
    tf = jnp.where((t_idx >= s_idx)[..., None, None, None], kern[lag_f, 0], 0.0)
    tr = jnp.where((s_idx >= t_idx)[..., None, None, None], kern[lag_r, 1], 0.0)
    n_g = a_re.shape[1]
    w_t = (tf + tr).transpose(2, 0, 4, 1, 3).reshape(n_g, lc * S5_GROUP_CH, lc * S5_GROUP_CH)
    zf_re, zf_im = akb_re[::-1, 0], akb_im[::-1, 0]
    zr_re, zr_im = akb_re[:, 1], akb_im[:, 1]
    w_z = jnp.concatenate([zf_re, zr_re, zf_im, zr_im], axis=2)
    w_z = w_z.transpose(1, 0, 3, 2).reshape(n_g, lc * S5_GROUP_CH, 4 * S5_STATE)
    pf_re, pf_im = pk_re[1:, 0], pk_im[1:, 0]
    pr_re, pr_im = pk_re[lc:0:-1, 1], pk_im[lc:0:-1, 1]

    def cpow(cr, ci, pr, pi):
        return cr[None] * pr[:, :, None, :] - ci[None] * pi[:, :, None, :], \
               cr[None] * pi[:, :, None, :] + ci[None] * pr[:, :, None, :]

    cf_re, cf_im = cpow(c_re[0], c_im[0], pf_re, pf_im)
    cr_re, cr_im = cpow(c_re[1], c_im[1], pr_re, pr_im)
    w_c = jnp.concatenate([cf_re, cr_re, -cf_im, -cr_im], axis=3)
    w_c = w_c.transpose(1, 3, 0, 2).reshape(n_g, 4 * S5_STATE, lc * S5_GROUP_CH)
    a_chunk = jnp.concatenate([pk_re[lc, 0], pk_re[lc, 1], pk_im[lc, 0], pk_im[lc, 1]], axis=1)
    return w_t.astype(BF16), w_z.astype(BF16), w_c.astype(BF16), a_chunk


def _regroup_rows(n_rows):
    return min(n_rows, 64)


def _s5_kernel(ul_ref, uc_ref, wt_ref, wz_ref, wc_ref, a_ref, d_ref, y_ref,
               lhs_l, lhs_c, zl_re, zl_im, zc_re, zc_im, yg_ref, *, nb, ncl, ncc):
    gb = S5_GROUP_BLOCK
    lc, ch = S5_CHUNK, S5_GROUP_CH
    per_tile = LANE // ch
    nl = nb * ncl
    ncx = nb * ncc
    half = 2 * S5_STATE
    lane_blk = lambda rows: lax.broadcasted_iota(jnp.int32, (rows, LANE), 1) // ch

    def block_transpose(arrs, blk):
        n = len(arrs)
        rolled = []
        for k in range(n):
            w = arrs[k]
            for g in range(1, n):
                w = jnp.where(blk == g, arrs[(g + k) % n], w)
            rolled.append(w if k == 0 else pltpu.roll(w, ch * k, 1))
        outs = []
        for b in range(n):
            o = rolled[0]
            for k in range(1, n):
                o = jnp.where(blk == (b + k) % n, rolled[k], o)
            outs.append(o)
        return outs

    def gather_chunks(src_ref, dst_ref, n_chunks):
        rb = _regroup_rows(n_chunks)
        blk = lane_blk(rb)

        def step(i, carry):
            r0 = pl.multiple_of(i * rb, rb)
            for hh in range(lc // per_tile):
                ut = [src_ref[pl.ds(r0 * lc + hh * per_tile + j, rb, stride=lc), :] for j in range(per_tile)]
                for g, out in enumerate(block_transpose(ut, blk)):
                    dst_ref[g, pl.ds(r0, rb), hh * LANE:(hh + 1) * LANE] = out.astype(BF16)
            return carry

        lax.fori_loop(0, n_chunks // rb, step, 0)

    gather_chunks(ul_ref, lhs_l, nl)
    gather_chunks(uc_ref, lhs_c, ncx)

    for g in range(gb):
        zl = _dot(lhs_l[g], wz_ref[g])
        zl_re[pl.ds(g, nl, stride=gb), :] = zl[:, :half]
        zl_im[pl.ds(g, nl, stride=gb), :] = zl[:, half:]
        zc = _dot(lhs_c[g], wz_ref[g])
        zc_re[pl.ds(g, ncx, stride=gb), :] = zc[:, :half]
        zc_im[pl.ds(g, ncx, stride=gb), :] = zc[:, half:]
    a_re = a_ref[:, :half]
    a_im = a_ref[:, half:]
    fwd = lax.broadcasted_iota(jnp.int32, (gb, half), 1) < S5_STATE

    def advance(h_re, h_im, z_re, z_im):
        return a_re * h_re - a_im * h_im + z_re, a_re * h_im + a_im * h_re + z_im

    def ctx_step(i, carry):
        out = []
        for b in range(nb):
            sl_f = pl.ds(pl.multiple_of((b * ncc + i) * gb, gb), gb)
            sl_r = pl.ds(pl.multiple_of((b * ncc + ncc - 1 - i) * gb, gb), gb)
            z_re = jnp.where(fwd, zc_re[sl_f, :], zc_re[sl_r, :])
            z_im = jnp.where(fwd, zc_im[sl_f, :], zc_im[sl_r, :])
            out.extend(advance(carry[2 * b], carry[2 * b + 1], z_re, z_im))
        return tuple(out)

    def lat_step(i, carry):
        out = []
        for b in range(nb):
            h_re, h_im = carry[2 * b], carry[2 * b + 1]
            sl_f = pl.ds(pl.multiple_of((b * ncl + i) * gb, gb), gb)
            sl_r = pl.ds(pl.multiple_of((b * ncl + ncl - 1 - i) * gb, gb), gb)
            f_re, f_im, r_re, r_im = zl_re[sl_f, :], zl_im[sl_f, :], zl_re[sl_r, :], zl_im[sl_r, :]
            zl_re[sl_f, :] = jnp.where(fwd, h_re, f_re)
            zl_im[sl_f, :] = jnp.where(fwd, h_im, f_im)
            zl_re[sl_r, :] = jnp.where(fwd, r_re, h_re)
            zl_im[sl_r, :] = jnp.where(fwd, r_im, h_im)
            out.extend(advance(h_re, h_im, jnp.where(fwd, f_re, r_re), jnp.where(fwd, f_im, r_im)))
        return tuple(out)

    zero = jnp.zeros((gb, half), F32)
    carry = lax.fori_loop(0, ncc, ctx_step, (zero,) * (2 * nb))
    lax.fori_loop(0, ncl, lat_step, carry)
    for g in range(gb):
        rows = pl.ds(g, nl, stride=gb)
        h_in = jnp.concatenate([zl_re[rows, :], zl_im[rows, :]], axis=1).astype(BF16)
        yg_ref[g] = _dot(lhs_l[g], wt_ref[g]) + _dot(h_in, wc_ref[g])

    rb = _regroup_rows(nl)
    blk = lane_blk(rb)
    d_row = d_ref[0]

    def scatter_step(i, carry):
        r0 = pl.multiple_of(i * rb, rb)
        for hh in range(lc // per_tile):
            yt = [yg_ref[g, pl.ds(r0, rb), hh * LANE:(hh + 1) * LANE] for g in range(gb)]
            for j, out in enumerate(block_transpose(yt, blk)):
                rows = pl.ds(r0 * lc + hh * per_tile + j, rb, stride=lc)
                y_ref[rows, :] = out + ul_ref[rows, :] * d_row
        return carry

    lax.fori_loop(0, nl // rb, scatter_step, 0)


def _s5_branch(u, uc, w_t, w_z, w_c, a_chunk, s5_d):
    bsz, n, width = u.shape
    n_ctx = uc.shape[1]
    lc, ch = S5_CHUNK, S5_GROUP_CH
    n_g = width // ch
    ncl, ncc = n // lc, n_ctx // lc
    gb = S5_GROUP_BLOCK
    assert ncl % 2 == 0 and n_g % gb == 0 and gb * ch == LANE
    kw = lc * ch
    nl, ncx = bsz * ncl, bsz * ncc
    assert nl % _regroup_rows(nl) == 0 and ncx % _regroup_rows(ncx) == 0
    once = pl.Buffered(1)
    slab = lambda rows: pl.BlockSpec((rows, LANE), lambda i: (0, i), pipeline_mode=once)
    blk3 = lambda r, c: pl.BlockSpec((gb, r, c), lambda i: (i, 0, 0))
    half = 2 * S5_STATE
    y = pl.pallas_call(
        functools.partial(_s5_kernel, nb=bsz, ncl=ncl, ncc=ncc),
        grid=(n_g // gb,),
        in_specs=[slab(bsz * n), slab(bsz * n_ctx), blk3(kw, kw), blk3(kw, 2 * half), blk3(2 * half, kw),
                  pl.BlockSpec((gb, 2 * half), lambda i: (i, 0)),
                  pl.BlockSpec((1, 1, LANE), lambda i: (i, 0, 0))],
        out_specs=slab(bsz * n),
        out_shape=jax.ShapeDtypeStruct((bsz * n, width), F32),
        scratch_shapes=[pltpu.VMEM((gb, nl, kw), BF16), pltpu.VMEM((gb, ncx, kw), BF16),
                        pltpu.VMEM((gb * nl, half), F32), pltpu.VMEM((gb * nl, half), F32),
                        pltpu.VMEM((gb * ncx, half), F32), pltpu.VMEM((gb * ncx, half), F32),
                        pltpu.VMEM((gb, nl, kw), F32)],
        compiler_params=pltpu.CompilerParams(vmem_limit_bytes=VMEM_LIMIT),
        name="s5_chunked_scan",
    )(u.reshape(bsz * n, width), uc.reshape(bsz * n_ctx, width), w_t, w_z, w_c, a_chunk,
      s5_d.reshape(n_g // gb, 1, LANE))
    return y.reshape(bsz, n, width)


def _attn_kernel(q_ref, k_ref, v_ref, o_ref, *, tk, group):
    tq = q_ref.shape[1]
    n_k = k_ref.shape[1]
    qs = [q_ref[0, :, h * HEAD_DIM:(h + 1) * HEAD_DIM] for h in range(group)]

    ones = jnp.ones((tk, HEAD_DIM), BF16)

    def body(c, carry):
        start = pl.multiple_of(c * tk, tk)
        ks = k_ref[0, pl.ds(start, tk), :]
        vs = jnp.concatenate([v_ref[0, pl.ds(start, tk), :], ones], axis=1)
        out = []
        score = lambda h: lax.dot_general(qs[h], ks, (((1,), (1,)), ((), ())), preferred_element_type=F32)
        s_next = score(0)
        for h in range(group):
            m, acc = carry[2 * h:2 * h + 2]
            s = s_next
            if h + 1 < group:
                s_next = score(h + 1)
            m_new = jnp.maximum(m, jnp.max(s, axis=-1, keepdims=True))
            p = jnp.exp2(s - m_new)
            alpha = jnp.exp2(m - m_new)
            acc = alpha * acc + _dot(p.astype(BF16), vs)
            out.extend((m_new, acc))
        return tuple(out)

    init = (jnp.full((tq, 1), -jnp.inf, F32), jnp.zeros((tq, 2 * HEAD_DIM), F32)) * group
    fin = lax.fori_loop(0, n_k // tk, body, init)
    for h in range(group):
        acc = fin[2 * h + 1]
        o_ref[0, :, h * HEAD_DIM:(h + 1) * HEAD_DIM] = (acc[:, :HEAD_DIM] / acc[:, HEAD_DIM:]).astype(o_ref.dtype)


def _pick_divisor(n, pref):
    best = LANE
    for t in range(LANE, pref + 1, LANE):
        if n % t == 0:
            best = t
    return best


def _attention(q, k, v):
    bsz, n, dq = q.shape
    n_k, dkv = k.shape[1], k.shape[2]
    n_kv = dkv // HEAD_DIM
    group = dq // dkv
    tq = min(n, 256)
    tk = _pick_divisor(n_k, 3072)
    return pl.pallas_call(
        functools.partial(_attn_kernel, tk=tk, group=group),
        grid=(bsz, n_kv, n // tq),
        in_specs=[pl.BlockSpec((1, tq, group * HEAD_DIM), lambda b, h, i: (b, i, h)),
                  pl.BlockSpec((1, n_k, HEAD_DIM), lambda b, h, i: (b, 0, h)),
                  pl.BlockSpec((1, n_k, HEAD_DIM), lambda b, h, i: (b, 0, h))],
        out_specs=pl.BlockSpec((1, tq, group * HEAD_DIM), lambda b, h, i: (b, i, h)),
        out_shape=jax.ShapeDtypeStruct((bsz, n, dq), BF16),
        compiler_params=pltpu.CompilerParams(vmem_limit_bytes=VMEM_LIMIT),
        name="gqa_flash_attention",
    )(q, k, v)


def _merge_kernel(x_ref, y_ref, o_ref, g_ref, mod_ref, wa_ref, wb_ref, wo_ref, wout_ref,
                  ln_ref, wrh_ref, wrl_ref, br_ref, x1_ref, h2_ref, lg_ref, *, alpha):
    d = x_ref.shape[2]
    gact = jax.nn.gelu(y_ref[0]).astype(BF16)
    ssm = _dot(gact, wa_ref[...]) * jax.nn.sigmoid(_dot(gact, wb_ref[...]))
    att = _dot(o_ref[0], wo_ref[...])
    gate = g_ref[0].astype(F32)
    mixed = (gate[:, :d] * ssm + gate[:, d:] * att).astype(BF16)
    mix = _dot(mixed, wout_ref[...])
    g1 = mod_ref[0, 0:1, :]
    x1 = _layer_norm(alpha * x_ref[0] + g1 * mix) * ln_ref[0:1, :] + ln_ref[1:2, :]
    x1_ref[0] = x1
    h2 = _layer_norm(x1) * (1.0 + mod_ref[0, 2:3, :]) + mod_ref[0, 1:2, :]
    h2_ref[0] = h2
    h_hi, h_lo = _split_bf16(h2)
    lg_ref[0] = _dot3(h_hi, h_lo, wrh_ref[...], wrl_ref[...]) + br_ref[...]


def _merge(x, y_ssm, o, gates, mod, w_glu_a, w_glu_b, w_attn_o, w_out, ln, wr_hi, wr_lo, br, alpha):
    bsz, n, d = x.shape
    tm = min(n, 512)
    row = lambda w: pl.BlockSpec((1, tm, w), lambda b, i: (b, i, 0))
    const = lambda a: pl.BlockSpec(a.shape, lambda b, i: (0,) * a.ndim)
    nr = wr_hi.shape[1]
    return pl.pallas_call(
        functools.partial(_merge_kernel, alpha=alpha),
        grid=(bsz, n // tm),
        in_specs=[row(d), row(y_ssm.shape[2]), row(o.shape[2]), row(gates.shape[2]),
                  pl.BlockSpec((1, 3, d), lambda b, i: (b, 0, 0)),
                  const(w_glu_a), const(w_glu_b), const(w_attn_o), const(w_out), const(ln),
                  const(wr_hi), const(wr_lo), const(br)],
        out_specs=[row(d), row(d), row(nr)],
        out_shape=[jax.ShapeDtypeStruct((bsz, n, d), F32), jax.ShapeDtypeStruct((bsz, n, d), F32),
                   jax.ShapeDtypeStruct((bsz, n, nr), F32)],
        compiler_params=pltpu.CompilerParams(vmem_limit_bytes=VMEM_LIMIT),
        name="merge_postnorm_router",
    )(x, y_ssm, o, gates, mod, w_glu_a, w_glu_b, w_attn_o, w_out, ln, wr_hi, wr_lo, br)


def _route_kernel(lt_ref, e_ref, w_ref, r_ref, cnt_ref, run_ref):
    i = pl.program_id(0)
    tn = lt_ref.shape[1]
    n_experts = run_ref.shape[0]

    @pl.when(i == 0)
    def _():
        run_ref[...] = jnp.zeros_like(run_ref)

    gl = lt_ref[0:N_EXPERT_GROUPS, :]
    gmax = jnp.max(gl, axis=0, keepdims=True)
    gi = lax.broadcasted_iota(jnp.int32, gl.shape, 0)
    gidx = jnp.min(jnp.where(gl == gmax, gi, N_EXPERT_GROUPS), axis=0, keepdims=True)
    gw = 1.0 / jnp.sum(jnp.exp(gl - gmax), axis=0, keepdims=True)
    epg = EXPERTS_PER_GROUP
    e_in = lt_ref[8:8 + epg, :]
    for g in range(1, N_EXPERT_GROUPS):
        e_in = jnp.where(gidx == g, lt_ref[8 + g * epg:8 + (g + 1) * epg, :], e_in)
    ei = lax.broadcasted_iota(jnp.int32, e_in.shape, 0)
    v0 = jnp.max(e_in, axis=0, keepdims=True)
    i0 = jnp.min(jnp.where(e_in == v0, ei, epg), axis=0, keepdims=True)
    rest = jnp.where(ei == i0, -jnp.inf, e_in)
    v1 = jnp.max(rest, axis=0, keepdims=True)
    i1 = jnp.min(jnp.where(rest == v1, ei, epg), axis=0, keepdims=True)
    t = jnp.exp(v1 - v0)
    w0 = gw / (1.0 + t)
    w1 = gw * t / (1.0 + t)
    e0 = gidx * epg + i0
    e1 = gidx * epg + i1
    zi = jnp.zeros_like(e0)
    e_ref[...] = jnp.concatenate([e0, e1] + [zi] * 6, axis=0)
    w_ref[...] = jnp.concatenate([w0, w1] + [jnp.zeros_like(w0)] * 6, axis=0)

    both = jnp.concatenate([e0, e1], axis=1)
    hit = lax.broadcasted_iota(jnp.int32, (n_experts, 2 * tn), 0) == both
    tri = (lax.broadcasted_iota(jnp.int32, (2 * tn, 2 * tn), 0)
           <= lax.broadcasted_iota(jnp.int32, (2 * tn, 2 * tn), 1))
    pref = _dot(jnp.where(hit, 1.0, 0.0).astype(BF16), jnp.where(tri, 1.0, 0.0).astype(BF16))
    run = run_ref[:, 0:1]
    rank = jnp.sum(jnp.where(hit, pref + run, 0.0), axis=0, keepdims=True) - 1.0
    rank = rank.astype(jnp.int32)
    r_ref[...] = jnp.concatenate([rank[:, :tn], rank[:, tn:]] + [zi] * 6, axis=0)
    run_new = jnp.broadcast_to(run + pref[:, 2 * tn - 1:2 * tn], run_ref.shape)
    run_ref[...] = run_new
    cnt_ref[...] = run_new.astype(jnp.int32)


def _route(logits_t, n_experts):
    rows, n = logits_t.shape
    tn = min(n, 256)
    tile = pl.BlockSpec((8, tn), lambda i: (0, i))
    return pl.pallas_call(
        _route_kernel,
        grid=(n // tn,),
        in_specs=[pl.BlockSpec((rows, tn), lambda i: (0, i))],
        out_specs=[tile, tile, tile, pl.BlockSpec((n_experts, LANE), lambda i: (0, 0))],
        out_shape=[jax.ShapeDtypeStruct((8, n), jnp.int32), jax.ShapeDtypeStruct((8, n), F32),
                   jax.ShapeDtypeStruct((8, n), jnp.int32), jax.ShapeDtypeStruct((n_experts, LANE), jnp.int32)],
        scratch_shapes=[pltpu.VMEM((n_experts, LANE), F32)],
        compiler_params=pltpu.CompilerParams(dimension_semantics=("arbitrary",)),
        name="route_top2_rank",
    )(logits_t)


def _dispatch_kernel(pos_ref, starts_ref, ends_ref, h_ref, x_hbm, zbuf, sem, zsem,
                     *, n_tok, n_experts, tm_e):
    i = pl.program_id(0)
    tm = h_ref.shape[0]
    base = i * tm

    @pl.when(i == 0)
    def _():
        zbuf[...] = jnp.zeros_like(zbuf)

        def tail_copy(e):
            start = pl.multiple_of(ends_ref[e] - tm_e, tm_e)
            return pltpu.make_async_copy(zbuf, x_hbm.at[pl.ds(start, tm_e), :], zsem)

        def fill(e, c):
            @pl.when(ends_ref[e] > starts_ref[e])
            def _():
                tail_copy(e).start()
            return c

        def fill_wait(e, c):
            @pl.when(ends_ref[e] > starts_ref[e])
            def _():
                tail_copy(e).wait()
            return c

        def free_copy(t):
            return pltpu.make_async_copy(zbuf, x_hbm.at[pl.ds(pl.multiple_of(t * tm_e, tm_e), tm_e), :], zsem)

        def free_fill(t, c):
            free_copy(t).start()
            return c

        def free_wait(t, c):
            free_copy(t).wait()
            return c

        first_free = ends_ref[n_experts - 1] // tm_e
        n_tiles = x_hbm.shape[0] // tm_e
        lax.fori_loop(0, n_experts, fill, 0)
        lax.fori_loop(first_free, n_tiles, free_fill, 0)
        lax.fori_loop(0, n_experts, fill_wait, 0)
        lax.fori_loop(first_free, n_tiles, free_wait, 0)

    def row_copy(r, slot):
        p = pos_ref[slot * n_tok + base + r]
        return pltpu.make_async_copy(h_ref.at[pl.ds(r, 1), :], x_hbm.at[pl.ds(p, 1), :], sem)

    def issue(r, c):
        row_copy(r, 0).start()
        row_copy(r, 1).start()
        return c

    def drain(r, c):
        row_copy(r, 0).wait()
        row_copy(r, 1).wait()
        return c

    lax.fori_loop(0, tm, issue, 0, unroll=DMA_UNROLL)
    lax.fori_loop(0, tm, drain, 0, unroll=DMA_UNROLL)


def _dispatch(h2, pos, starts, ends, n_rows, tm_e):
    n_tok, d = h2.shape
    tm = min(n_tok, 512)
    grid_spec = pltpu.PrefetchScalarGridSpec(
        num_scalar_prefetch=3,
        grid=(n_tok // tm,),
        in_specs=[pl.BlockSpec((tm, d), lambda i, *_: (i, 0))],
        out_specs=pl.BlockSpec(memory_space=pl.ANY),
        scratch_shapes=[pltpu.VMEM((tm_e, d), F32), pltpu.SemaphoreType.DMA(()), pltpu.SemaphoreType.DMA(())],
    )
    return pl.pallas_call(
        functools.partial(_dispatch_kernel, n_tok=n_tok, n_experts=starts.shape[0], tm_e=tm_e),
        grid_spec=grid_spec,
        out_shape=jax.ShapeDtypeStruct((n_rows, d), F32),
        compiler_params=pltpu.CompilerParams(dimension_semantics=("arbitrary",), vmem_limit_bytes=VMEM_LIMIT),
        name="dispatch_rows",
    )(pos, starts, ends, h2)


def _expert_kernel(tile_e_ref, used_ref, x_ref, wg_ref, wu_ref, wd_ref, y_ref, wg_b, wu_b, wd_b):
    i = pl.program_id(0)
    used = i < used_ref[0]
    fresh = jnp.logical_or(i == 0, tile_e_ref[i] != tile_e_ref[jnp.maximum(i - 1, 0)])

    @pl.when(jnp.logical_and(used, fresh))
    def _():
        wg_b[...] = wg_ref[0].astype(BF16)
        wu_b[...] = wu_ref[0].astype(BF16)
        wd_b[...] = wd_ref[0].astype(BF16)

    @pl.when(used)
    def _():
        xb = x_ref[...].astype(BF16)
        a = _dot(xb, wg_b[...])
        b = _dot(xb, wu_b[...])
        y_ref[...] = _dot((a * jax.nn.sigmoid(a) * b).astype(BF16), wd_b[...])

    @pl.when(jnp.logical_not(used))
    def _():
        y_ref[...] = jnp.zeros_like(y_ref)


def _experts(x_sorted, tile_expert, n_used, w_gate, w_up, w_down, tm):
    n_rows, d = x_sorted.shape
    d_e = w_gate.shape[2]
    last = lambda i, used: jnp.minimum(i, used[0] - 1)
    grid_spec = pltpu.PrefetchScalarGridSpec(
        num_scalar_prefetch=2,
        grid=(n_rows // tm,),
        in_specs=[pl.BlockSpec((tm, d), lambda i, te, used: (last(i, used), 0)),
                  pl.BlockSpec((1, d, d_e), lambda i, te, used: (te[i], 0, 0)),
                  pl.BlockSpec((1, d, d_e), lambda i, te, used: (te[i], 0, 0)),
                  pl.BlockSpec((1, d_e, d), lambda i, te, used: (te[i], 0, 0))],
        out_specs=pl.BlockSpec((tm, d), lambda i, te, used: (i, 0)),
        scratch_shapes=[pltpu.VMEM((d, d_e), BF16), pltpu.VMEM((d, d_e), BF16), pltpu.VMEM((d_e, d), BF16)],
    )
    return pl.pallas_call(
        _expert_kernel,
        grid_spec=grid_spec,
        out_shape=jax.ShapeDtypeStruct((n_rows, d), F32),
        compiler_params=pltpu.CompilerParams(dimension_semantics=("arbitrary",), vmem_limit_bytes=VMEM_LIMIT),
        name="routed_experts",
    )(tile_expert, n_used, x_sorted, w_gate, w_up, w_down)


def _final_kernel(pos_ref, x1_ref, w_ref, mod_ref, ln_ref, y_hbm, o_ref, ybuf, sem,
                  *, alpha, n_tok):
    i = pl.program_id(0)
    tm = x1_ref.shape[0]
    base = i * tm

    def row_copy(r, slot):
        p = pos_ref[slot * n_tok + base + r]
        return pltpu.make_async_copy(y_hbm.at[pl.ds(p, 1), :], ybuf.at[slot, pl.ds(r, 1), :], sem)

    def issue(r, c):
        row_copy(r, 0).start()
        row_copy(r, 1).start()
        return c

    def drain(r, c):
        row_copy(r, 0).wait()
        row_copy(r, 1).wait()
        return c

    lax.fori_loop(0, tm, issue, 0, unroll=DMA_UNROLL)
    lax.fori_loop(0, tm, drain, 0, unroll=DMA_UNROLL)
    moe = w_ref[:, 0:1] * ybuf[0] + w_ref[:, 1:2] * ybuf[1]
    z = alpha * x1_ref[...] + mod_ref[0] * moe
    o_ref[...] = _layer_norm(z) * ln_ref[0:1, :] + ln_ref[1:2, :]


def _final(x1, pos, wts, y_sorted, g2, ln, alpha, seq):
    n_tok, d = x1.shape
    tm = min(seq, 256)
    per_b = seq // tm
    grid_spec = pltpu.PrefetchScalarGridSpec(
        num_scalar_prefetch=1,
        grid=(n_tok // tm,),
        in_specs=[pl.BlockSpec((tm, d), lambda i, *_: (i, 0)),
                  pl.BlockSpec((tm, 2), lambda i, *_: (i, 0)),
                  pl.BlockSpec((1, 1, d), lambda i, *_: (i // per_b, 0, 0)),
                  pl.BlockSpec((2, d), lambda i, *_: (0, 0)),
                  pl.BlockSpec(memory_space=pl.ANY)],
        out_specs=pl.BlockSpec((tm, d), lambda i, *_: (i, 0)),
        scratch_shapes=[pltpu.VMEM((2, tm, d), F32), pltpu.SemaphoreType.DMA(())],
    )
    return pl.pallas_call(
        functools.partial(_final_kernel, alpha=alpha, n_tok=n_tok),
        grid_spec=grid_spec,
        out_shape=jax.ShapeDtypeStruct((n_tok, d), F32),
        compiler_params=pltpu.CompilerParams(vmem_limit_bytes=VMEM_LIMIT),
        name="combine_postnorm",
    )(pos, x1, wts, g2, ln, y_sorted)


def _tile_plan(counts, n_rows, tm):
    n_experts = counts.shape[0]
    padded = ((counts + tm - 1) // tm) * tm
    ends = jnp.cumsum(padded).astype(jnp.int32)
    starts = ends - padded
    tile_start = jnp.arange(n_rows // tm, dtype=jnp.int32) * tm
    tile_e = jnp.sum((tile_start[:, None] >= ends[None, :]).astype(jnp.int32), axis=1)
    n_used = ends[-1:] // tm
    tile_e = jnp.minimum(tile_e, jnp.max(jnp.where(counts > 0, jnp.arange(n_experts, dtype=jnp.int32), 0)))
    return starts, ends, tile_e, n_used


def kernel(x, c, ctx, c_ctx, w_mod, b_mod, w_in, s5_a_re, s5_a_im, s5_log_dt, s5_b_re, s5_b_im, s5_c_re, s5_c_im, s5_d, w_glu_a, w_glu_b, q_gain, k_gain, w_attn_o, w_out, ln1_g, ln1_b, w_router_group, b_router_group, w_router_expert, b_router_expert, w_exp_gate, w_exp_up, w_exp_down, ln2_g, ln2_b):
    bsz, n_lat, d = x.shape
    n_ctx = ctx.shape[1]
    assert w_mod.shape[0] == DEPTH == 1 and bsz + 1 <= 8
    alpha = (2.0 * DEPTH) ** 0.25
    d_s5 = s5_d.shape[1]
    d_q = w_attn_o.shape[1]
    d_kv = N_KV_HEADS * HEAD_DIM
    n_experts = w_exp_gate.shape[1]

    cond = jnp.zeros((8, d), F32).at[:bsz].set(c).at[bsz].set(c_ctx)
    mod = _adaln(cond, w_mod[0], b_mod[0]).reshape(8, 6, d)
    mod_lat = mod[:bsz]
    mod_ctx = jnp.broadcast_to(mod[bsz:bsz + 1], (bsz, 6, d))

    w_in_b = w_in[0].astype(BF16)
    cos, sin = _rope_tables(n_lat)
    qg, kg = q_gain[0].reshape(1, HEAD_DIM), k_gain[0].reshape(1, HEAD_DIM)
    dims = dict(d_s5=d_s5, d_q=d_q, d_kv=d_kv)
    u, q, k, v, gates = _inproj(x, mod_lat[:, 0:2], w_in_b, cos, sin, qg, kg, latent=True, **dims)
    uc, kc, vc = _inproj(ctx, mod_ctx[:, 0:2], w_in_b, cos[:n_ctx], sin[:n_ctx], qg, kg, latent=False, **dims)

    w_t, w_z, w_c, a_chunk = _s5_weights(s5_a_re[0], s5_a_im[0], s5_log_dt[0], s5_b_re[0], s5_b_im[0],
                                         s5_c_re[0], s5_c_im[0])
    y_ssm = _s5_branch(u, uc, w_t, w_z, w_c, a_chunk, s5_d[0])

    o = _attention(q, jnp.concatenate([k, kc], axis=1), jnp.concatenate([v, vc], axis=1))

    n_r = 8 + n_experts
    n_r_pad = ((n_r + LANE - 1) // LANE) * LANE
    w_r = jnp.zeros((d, n_r_pad), F32).at[:, :N_EXPERT_GROUPS].set(w_router_group[0])
    w_r = w_r.at[:, 8:n_r].set(w_router_expert[0])
    b_r = jnp.zeros((1, n_r_pad), F32).at[0, :N_EXPERT_GROUPS].set(b_router_group[0])
    b_r = b_r.at[0, 8:n_r].set(b_router_expert[0])
    wr_hi, wr_lo = _split_bf16(w_r)
    ln1 = jnp.stack([ln1_g[0], ln1_b[0]])
    x1, h2, logits = _merge(x, y_ssm, o, gates, mod_lat[:, 2:5], w_glu_a[0].astype(BF16),
                            w_glu_b[0].astype(BF16), w_attn_o[0].astype(BF16), w_out[0].astype(BF16),
                            ln1, wr_hi, wr_lo, b_r, alpha)

    n_tok = bsz * n_lat
    tm_e = EXPERT_TILE
    n_rows = 2 * n_tok + n_experts * tm_e
    logits_t = logits.reshape(n_tok, n_r_pad)[:, :n_r].T
    eid, wts, rank, counts = _route(logits_t, n_experts)
    starts, ends, tile_e, n_used = _tile_plan(counts[:, 0], n_rows, tm_e)
    own = eid[:2, :, None] == jnp.arange(n_experts, dtype=jnp.int32)
    pos = (jnp.sum(jnp.where(own, starts, 0), axis=-1) + rank[:2]).reshape(-1)
    x_sorted = _dispatch(h2.reshape(n_tok, d), pos, starts, ends, n_rows, tm_e)
    y_sorted = _experts(x_sorted, tile_e, n_used, w_exp_gate[0], w_exp_up[0], w_exp_down[0], tm_e)
    ln2 = jnp.stack([ln2_g[0], ln2_b[0]])
    out = _final(x1.reshape(n_tok, d), pos, wts[:2].T, y_sorted, mod_lat[:, 5:6], ln2, alpha, n_lat)
    return out.reshape(bsz, n_lat, d)
```

```python
import functools
import math

import jax
import jax.numpy as jnp
from jax import lax
from jax.experimental import pallas as pl
from jax.experimental.pallas import tpu as pltpu

GRID_W = 64
S5_GROUP_CH = 16
S5_STATE = 64
HEAD_DIM = 128
N_KV_HEADS = 2
ROPE_THETA = 10000.0
N_EXPERT_GROUPS = 4
EXPERTS_PER_GROUP = 8
NORM_EPS = 1e-6
DEPTH = 1

S5_CHUNK = 16
S5_GROUP_BLOCK = 8
EXPERT_TILE = 256
DMA_UNROLL = 8
ATTN_Q_TILE = 256
ATTN_KV_CHUNK = 3072
LANE = 128
VMEM_LIMIT = 56 * 1024 * 1024

F32 = jnp.float32
BF16 = jnp.bfloat16


def _layer_norm(x):
    mu = jnp.mean(x, axis=-1, keepdims=True)
    xc = x - mu
    var = jnp.mean(xc * xc, axis=-1, keepdims=True)
    return xc * lax.rsqrt(var + NORM_EPS)


def _split_bf16(a):
    hi = a.astype(BF16)
    lo = (a - hi.astype(F32)).astype(BF16)
    return hi, lo


def _dot(a, b):
    return jnp.dot(a, b, preferred_element_type=F32)


def _dot3(a_hi, a_lo, b_hi, b_lo):
    return _dot(a_hi, b_hi) + _dot(a_hi, b_lo) + _dot(a_lo, b_hi)


def _adaln_kernel(c_ref, w_ref, b_ref, o_ref):
    c = c_ref[...]
    s = c * jax.nn.sigmoid(c)
    s_hi, s_lo = _split_bf16(s)
    w_hi, w_lo = _split_bf16(w_ref[...])
    o_ref[...] = _dot3(s_hi, s_lo, w_hi, w_lo) + b_ref[...]


def _adaln(cond, w, b):
    rows, d = cond.shape
    n = w.shape[1]
    tn = min(n, 1024)
    return pl.pallas_call(
        _adaln_kernel,
        grid=(n // tn,),
        in_specs=[pl.BlockSpec((rows, d), lambda j: (0, 0)),
                  pl.BlockSpec((d, tn), lambda j: (0, j)),
                  pl.BlockSpec((1, tn), lambda j: (0, j))],
        out_specs=pl.BlockSpec((rows, tn), lambda j: (0, j)),
        out_shape=jax.ShapeDtypeStruct((rows, n), F32),
        compiler_params=pltpu.CompilerParams(vmem_limit_bytes=VMEM_LIMIT),
        name="adaln",
    )(cond, w, b.reshape(1, n))


def _rms_rope(t, gain, cos, sin_signed, first_half, scale):
    r = lax.rsqrt(jnp.mean(t * t, axis=-1, keepdims=True) + NORM_EPS)
    tn = t * r * gain
    if cos is not None:
        partner = jnp.where(first_half, pltpu.roll(tn, HEAD_DIM - 32, 1), pltpu.roll(tn, 32, 1))
        tn = tn * cos + partner * sin_signed
    if scale != 1.0:
        tn = tn * scale
    return tn


def _inproj_kernel(x_ref, mod_ref, w_ref, cos_ref, sin_ref, qg_ref, kg_ref, *out_refs,
                   d_s5, d_q, d_kv, latent):
    x = x_ref[0]
    shift = mod_ref[0, 0:1, :]
    scale = mod_ref[0, 1:2, :]
    h = (_layer_norm(x) * (1.0 + scale) + shift).astype(BF16)
    o_q = d_s5
    o_k = o_q + d_q
    o_v = o_k + d_kv
    o_g = o_v + d_kv
    if latent:
        u_ref, q_ref, k_ref, v_ref, g_ref = out_refs
        cos = cos_ref[...]
        sin = sin_ref[...]
        lane = lax.broadcasted_iota(jnp.int32, cos.shape, 1)
        first_half = (lane % 64) < 32
    else:
        u_ref, k_ref, v_ref = out_refs
        cos = sin = first_half = None
    u_ref[0] = _dot(h, w_ref[:, 0:d_s5])
    if latent:
        q = _dot(h, w_ref[:, o_q:o_k])
        q_scale = HEAD_DIM ** -0.5 * math.log2(math.e)
        for hd in range(d_q // HEAD_DIM):
            sl = slice(hd * HEAD_DIM, (hd + 1) * HEAD_DIM)
            q_ref[0, :, sl] = _rms_rope(q[:, sl], qg_ref[...], cos, sin, first_half, q_scale).astype(BF16)
    k = _dot(h, w_ref[:, o_k:o_v])
    for hd in range(d_kv // HEAD_DIM):
        sl = slice(hd * HEAD_DIM, (hd + 1) * HEAD_DIM)
        k_ref[0, :, sl] = _rms_rope(k[:, sl], kg_ref[...], cos, sin, first_half, 1.0).astype(BF16)
    v_ref[0] = _dot(h, w_ref[:, o_v:o_g]).astype(BF16)
    if latent:
        g_ref[0] = jax.nn.sigmoid(_dot(h, w_ref[:, o_g:])).astype(BF16)


def _inproj(x, mod, w_in, cos, sin, q_gain, k_gain, *, d_s5, d_q, d_kv, latent):
    bsz, n, d = x.shape
    n_in = w_in.shape[1]
    d_gate = n_in - d_s5 - d_q - 2 * d_kv
    tm = min(n, 512)
    row = lambda w: pl.BlockSpec((1, tm, w), lambda b, i: (b, i, 0))
    out_shape = [jax.ShapeDtypeStruct((bsz, n, d_s5), F32)]
    out_specs = [row(d_s5)]
    if latent:
        out_shape.append(jax.ShapeDtypeStruct((bsz, n, d_q), BF16))
        out_specs.append(row(d_q))
    out_shape += [jax.ShapeDtypeStruct((bsz, n, d_kv), BF16)] * 2
    out_specs += [row(d_kv)] * 2
    if latent:
        out_shape.append(jax.ShapeDtypeStruct((bsz, n, d_gate), BF16))
        out_specs.append(row(d_gate))
    return pl.pallas_call(
        functools.partial(_inproj_kernel, d_s5=d_s5, d_q=d_q, d_kv=d_kv, latent=latent),
        grid=(bsz, n // tm),
        in_specs=[row(d),
                  pl.BlockSpec((1, 2, d), lambda b, i: (b, 0, 0)),
                  pl.BlockSpec((d, n_in), lambda b, i: (0, 0), pipeline_mode=pl.Buffered(1)),
                  pl.BlockSpec((tm, HEAD_DIM), lambda b, i: (i, 0)),
                  pl.BlockSpec((tm, HEAD_DIM), lambda b, i: (i, 0)),
                  pl.BlockSpec((1, HEAD_DIM), lambda b, i: (0, 0)),
                  pl.BlockSpec((1, HEAD_DIM), lambda b, i: (0, 0))],
        out_specs=out_specs,
        out_shape=out_shape,
        compiler_params=pltpu.CompilerParams(vmem_limit_bytes=VMEM_LIMIT),
        name="inproj_latent" if latent else "inproj_context",
    )(x, mod, w_in, cos, sin, q_gain, k_gain)


def _rope_tables(n_lat):
    rows = n_lat // GRID_W
    axis_dim = HEAD_DIM // 2
    inv = ROPE_THETA ** (-jnp.arange(0, axis_dim, 2, dtype=F32) / axis_dim)
    ang_r = jnp.arange(rows, dtype=F32)[:, None] * inv
    ang_c = jnp.arange(GRID_W, dtype=F32)[:, None] * inv
    per_row = lambda t: jnp.repeat(t, GRID_W, axis=0)
    per_col = lambda t: jnp.tile(t, (rows, 1))
    cos_r, sin_r = per_row(jnp.cos(ang_r)), per_row(jnp.sin(ang_r))
    cos_c, sin_c = per_col(jnp.cos(ang_c)), per_col(jnp.sin(ang_c))
    cos = jnp.concatenate([cos_r, cos_r, cos_c, cos_c], axis=1)
    sin = jnp.concatenate([-sin_r, sin_r, -sin_c, sin_c], axis=1)
    return cos, sin


def _s5_weights(a_re, a_im, log_dt, b_re, b_im, c_re, c_im):
    hp = lax.Precision.HIGHEST
    lc = S5_CHUNK
    dt = jnp.exp(log_dt)[..., None]
    lam_re, lam_im = a_re * dt, a_im * dt
    ea = jnp.exp(lam_re)
    ab_re, ab_im = ea * jnp.cos(lam_im), ea * jnp.sin(lam_im)
    den = a_re * a_re + a_im * a_im
    nr, ni = ab_re - 1.0, ab_im
    rr = (nr * a_re + ni * a_im) / den
    ri = (ni * a_re - nr * a_im) / den
    bb_re = rr[..., None] * b_re - ri[..., None] * b_im
    bb_im = rr[..., None] * b_im + ri[..., None] * b_re
    kk = jnp.arange(lc + 1, dtype=F32)[:, None, None, None]
    pk_mag = jnp.exp(kk * lam_re)
    pk_re, pk_im = pk_mag * jnp.cos(kk * lam_im), pk_mag * jnp.sin(kk * lam_im)
    akb_re = pk_re[:lc, ..., None] * bb_re - pk_im[:lc, ..., None] * bb_im
    akb_im = pk_re[:lc, ..., None] * bb_im + pk_im[:lc, ..., None] * bb_re
    kern = (jnp.einsum('dgop,kdgpi->kdgoi', c_re, akb_re, precision=hp)
            - jnp.einsum('dgop,kdgpi->kdgoi', c_im, akb_im, precision=hp))
    lag = jnp.arange(lc)[None, :] - jnp.arange(lc)[:, None]
    ks = jnp.arange(lc)[:, None, None]
    diag_f = (lag[None] == ks).astype(F32)
    diag_r = (-lag[None] == ks).astype(F32)
    n_g = a_re.shape[1]
    w_t = (jnp.einsum('kst,kgoi->gsito', diag_f, kern[:, 0], precision=hp)
           + jnp.einsum('kst,kgoi->gsito', diag_r, kern[:, 1], precision=hp))
    w_t = w_t.reshape(n_g, lc * S5_GROUP_CH, lc * S5_GROUP_CH)
    zf_re, zf_im = akb_re[::-1, 0], akb_im[::-1, 0]
    zr_re, zr_im = akb_re[:, 1], akb_im[:, 1]
    w_z = jnp.concatenate([zf_re, zr_re, zf_im, zr_im], axis=2)
    w_z = w_z.transpose(1, 0, 3, 2).reshape(n_g, lc * S5_GROUP_CH, 4 * S5_STATE)
    pf_re, pf_im = pk_re[1:, 0], pk_im[1:, 0]
    pr_re, pr_im = pk_re[lc:0:-1, 1], pk_im[lc:0:-1, 1]

    def cpow(cr, ci, pr, pi):
        return cr[None] * pr[:, :, None, :] - ci[None] * pi[:, :, None, :], \
               cr[None] * pi[:, :, None, :] + ci[None] * pr[:, :, None, :]

    cf_re, cf_im = cpow(c_re[0], c_im[0], pf_re, pf_im)
    cr_re, cr_im = cpow(c_re[1], c_im[1], pr_re, pr_im)
    w_c = jnp.concatenate([cf_re, cr_re, -cf_im, -cr_im], axis=3)
    w_c = w_c.transpose(1, 3, 0, 2).reshape(n_g, 4 * S5_STATE, lc * S5_GROUP_CH)
    a_chunk = jnp.concatenate([pk_re[lc, 0], pk_re[lc, 1], pk_im[lc, 0], pk_im[lc, 1]], axis=1)
    return w_t.astype(BF16), w_z.astype(BF16), w_c.astype(BF16), a_chunk


def _regroup_rows(n_rows):
    return min(n_rows, 64)


def _s5_kernel(ul_ref, uc_ref, wt_ref, wz_ref, wc_ref, a_ref, d_ref, y_ref,
               lhs_l, lhs_c, zl_re, zl_im, zc_re, zc_im, yg_ref, *, nb, ncl, ncc):
    gb = S5_GROUP_BLOCK
    lc, ch = S5_CHUNK, S5_GROUP_CH
    per_tile = LANE // ch
    nl = nb * ncl
    ncx = nb * ncc
    half = 2 * S5_STATE
    lane_blk = lambda rows: lax.broadcasted_iota(jnp.int32, (rows, LANE), 1) // ch

    def block_transpose(arrs, blk):
        n = len(arrs)
        rolled = []
        for k in range(n):
            w = arrs[k]
            for g in range(1, n):
                w = jnp.where(blk == g, arrs[(g + k) % n], w)
            rolled.append(w if k == 0 else pltpu.roll(w, ch * k, 1))
        outs = []
        for b in range(n):
            o = rolled[0]
            for k in range(1, n):
                o = jnp.where(blk == (b + k) % n, rolled[k], o)
            outs.append(o)
        return outs

    def gather_chunks(src_ref, dst_ref, n_chunks):
        rb = _regroup_rows(n_chunks)
        blk = lane_blk(rb)

        def step(i, carry):
            r0 = pl.multiple_of(i * rb, rb)
            for hh in range(lc // per_tile):
                ut = [src_ref[pl.ds(r0 * lc + hh * per_tile + j, rb, stride=lc), :] for j in range(per_tile)]
                for g, out in enumerate(block_transpose(ut, blk)):
                    dst_ref[g, pl.ds(r0, rb), hh * LANE:(hh + 1) * LANE] = out.astype(BF16)
            return carry

        lax.fori_loop(0, n_chunks // rb, step, 0)

    gather_chunks(ul_ref, lhs_l, nl)
    gather_chunks(uc_ref, lhs_c, ncx)

    for g in range(gb):
        zl = _dot(lhs_l[g], wz_ref[g])
        zl_re[pl.ds(g, nl, stride=gb), :] = zl[:, :half]
        zl_im[pl.ds(g, nl, stride=gb), :] = zl[:, half:]
        zc = _dot(lhs_c[g], wz_ref[g])
        zc_re[pl.ds(g, ncx, stride=gb), :] = zc[:, :half]
        zc_im[pl.ds(g, ncx, stride=gb), :] = zc[:, half:]
    a_re = a_ref[:, :half]
    a_im = a_ref[:, half:]
    fwd = lax.broadcasted_iota(jnp.int32, (gb, half), 1) < S5_STATE

    def advance(h_re, h_im, z_re, z_im):
        return a_re * h_re - a_im * h_im + z_re, a_re * h_im + a_im * h_re + z_im

    def ctx_step(i, carry):
        out = []
        for b in range(nb):
            sl_f = pl.ds(pl.multiple_of((b * ncc + i) * gb, gb), gb)
            sl_r = pl.ds(pl.multiple_of((b * ncc + ncc - 1 - i) * gb, gb), gb)
            z_re = jnp.where(fwd, zc_re[sl_f, :], zc_re[sl_r, :])
            z_im = jnp.where(fwd, zc_im[sl_f, :], zc_im[sl_r, :])
            out.extend(advance(carry[2 * b], carry[2 * b + 1], z_re, z_im))
        return tuple(out)

    def lat_step(i, carry):
        out = []
        for b in range(nb):
            h_re, h_im = carry[2 * b], carry[2 * b + 1]
            sl_f = pl.ds(pl.multiple_of((b * ncl + i) * gb, gb), gb)
            sl_r = pl.ds(pl.multiple_of((b * ncl + ncl - 1 - i) * gb, gb), gb)
            f_re, f_im, r_re, r_im = zl_re[sl_f, :], zl_im[sl_f, :], zl_re[sl_r, :], zl_im[sl_r, :]
            zl_re[sl_f, :] = jnp.where(fwd, h_re, f_re)
            zl_im[sl_f, :] = jnp.where(fwd, h_im, f_im)
            zl_re[sl_r, :] = jnp.where(fwd, r_re, h_re)
            zl_im[sl_r, :] = jnp.where(fwd, r_im, h_im)
            out.extend(advance(h_re, h_im, jnp.where(fwd, f_re, r_re), jnp.where(fwd, f_im, r_im)))
        return tuple(out)

    zero = jnp.zeros((gb, half), F32)
    carry = lax.fori_loop(0, ncc, ctx_step, (zero,) * (2 * nb))
    lax.fori_loop(0, ncl, lat_step, carry)
    for g in range(gb):
        rows = pl.ds(g, nl, stride=gb)
        h_in = jnp.concatenate([zl_re[rows, :], zl_im[rows, :]], axis=1).astype(BF16)
        yg_ref[g] = _dot(lhs_l[g], wt_ref[g]) + _dot(h_in, wc_ref[g])

    rb = _regroup_rows(nl)
    blk = lane_blk(rb)
    d_row = d_ref[0]

    def scatter_step(i, carry):
        r0 = pl.multiple_of(i * rb, rb)
        for hh in range(lc // per_tile):
            yt = [yg_ref[g, pl.ds(r0, rb), hh * LANE:(hh + 1) * LANE] for g in range(gb)]
            for j, out in enumerate(block_transpose(yt, blk)):
                rows = pl.ds(r0 * lc + hh * per_tile + j, rb, stride=lc)
                y_ref[rows, :] = out + ul_ref[rows, :] * d_row
        return carry

    lax.fori_loop(0, nl // rb, scatter_step, 0)


def _s5_branch(u, uc, w_t, w_z, w_c, a_chunk, s5_d):
    bsz, n, width = u.shape
    n_ctx = uc.shape[1]
    lc, ch = S5_CHUNK, S5_GROUP_CH
    n_g = width // ch
    ncl, ncc = n // lc, n_ctx // lc
    gb = S5_GROUP_BLOCK
    assert ncl % 2 == 0 and n_g % gb == 0 and gb * ch == LANE
    kw = lc * ch
    nl, ncx = bsz * ncl, bsz * ncc
    assert nl % _regroup_rows(nl) == 0 and ncx % _regroup_rows(ncx) == 0
    once = pl.Buffered(1)
    slab = lambda rows: pl.BlockSpec((rows, LANE), lambda i: (0, i), pipeline_mode=once)
    blk3 = lambda r, c: pl.BlockSpec((gb, r, c), lambda i: (i, 0, 0))
    half = 2 * S5_STATE
    y = pl.pallas_call(
        functools.partial(_s5_kernel, nb=bsz, ncl=ncl, ncc=ncc),
        grid=(n_g // gb,),
        in_specs=[slab(bsz * n), slab(bsz * n_ctx), blk3(kw, kw), blk3(kw, 2 * half), blk3(2 * half, kw),
                  pl.BlockSpec((gb, 2 * half), lambda i: (i, 0)),
                  pl.BlockSpec((1, 1, LANE), lambda i: (i, 0, 0))],
        out_specs=slab(bsz * n),
        out_shape=jax.ShapeDtypeStruct((bsz * n, width), F32),
        scratch_shapes=[pltpu.VMEM((gb, nl, kw), BF16), pltpu.VMEM((gb, ncx, kw), BF16),
                        pltpu.VMEM((gb * nl, half), F32), pltpu.VMEM((gb * nl, half), F32),
                        pltpu.VMEM((gb * ncx, half), F32), pltpu.VMEM((gb * ncx, half), F32),
                        pltpu.VMEM((gb, nl, kw), F32)],
        compiler_params=pltpu.CompilerParams(vmem_limit_bytes=VMEM_LIMIT),
        name="s5_chunked_scan",
    )(u.reshape(bsz * n, width), uc.reshape(bsz * n_ctx, width), w_t, w_z, w_c, a_chunk,
      s5_d.reshape(n_g // gb, 1, LANE))
    return y.reshape(bsz, n, width)


def _attn_kernel(q_ref, k_ref, v_ref, kc_ref, vc_ref, o_ref, k_all, v_all, *, tk, group):
    tq = q_ref.shape[1]
    n_lat, n_ctx = k_ref.shape[1], kc_ref.shape[1]
    n_k = n_lat + n_ctx

    @pl.when(pl.program_id(2) == 0)
    def _():
        k_all[0:n_lat, :] = k_ref[0]
        k_all[n_lat:n_k, :] = kc_ref[0]
        v_all[0:n_lat, 0:HEAD_DIM] = v_ref[0]
        v_all[n_lat:n_k, 0:HEAD_DIM] = vc_ref[0]
        v_all[:, HEAD_DIM:] = jnp.ones((n_k, HEAD_DIM), BF16)

    qs = [q_ref[0, :, h * HEAD_DIM:(h + 1) * HEAD_DIM] for h in range(group)]

    def body(c, carry):
        start = pl.multiple_of(c * tk, tk)
        ks = k_all[pl.ds(start, tk), :]
        vs = v_all[pl.ds(start, tk), :]
        out = []
        score = lambda h: lax.dot_general(qs[h], ks, (((1,), (1,)), ((), ())), preferred_element_type=F32)
        s_next = score(0)
        for h in range(group):
            m, acc = carry[2 * h:2 * h + 2]
            s = s_next
            if h + 1 < group:
                s_next = score(h + 1)
            m_new = jnp.maximum(m, jnp.max(s, axis=-1, keepdims=True))
            p = jnp.exp2(s - m_new)
            alpha = jnp.exp2(m - m_new)
            acc = alpha * acc + _dot(p.astype(BF16), vs)
            out.extend((m_new, acc))
        return tuple(out)

    init = (jnp.full((tq, 1), -jnp.inf, F32), jnp.zeros((tq, 2 * HEAD_DIM), F32)) * group
    fin = lax.fori_loop(0, n_k // tk, body, init)
    for h in range(group):
        acc = fin[2 * h + 1]
        o_ref[0, :, h * HEAD_DIM:(h + 1) * HEAD_DIM] = (acc[:, :HEAD_DIM] / acc[:, HEAD_DIM:]).astype(o_ref.dtype)


def _pick_divisor(n, pref):
    best = LANE
    for t in range(LANE, pref + 1, LANE):
        if n % t == 0:
            best = t
    return best


def _attention(q, k, v, kc, vc):
    bsz, n, dq = q.shape
    n_c, dkv = kc.shape[1], k.shape[2]
    n_kv = dkv // HEAD_DIM
    group = dq // dkv
    tq = min(n, ATTN_Q_TILE)
    tk = _pick_divisor(n + n_c, ATTN_KV_CHUNK)
    kv_spec = lambda rows: pl.BlockSpec((1, rows, HEAD_DIM), lambda b, h, i: (b, 0, h))
    return pl.pallas_call(
        functools.partial(_attn_kernel, tk=tk, group=group),
        grid=(bsz, n_kv, n // tq),
        in_specs=[pl.BlockSpec((1, tq, group * HEAD_DIM), lambda b, h, i: (b, i, h)),
                  kv_spec(n), kv_spec(n), kv_spec(n_c), kv_spec(n_c)],
        out_specs=pl.BlockSpec((1, tq, group * HEAD_DIM), lambda b, h, i: (b, i, h)),
        out_shape=jax.ShapeDtypeStruct((bsz, n, dq), BF16),
        scratch_shapes=[pltpu.VMEM((n + n_c, HEAD_DIM), BF16), pltpu.VMEM((n + n_c, 2 * HEAD_DIM), BF16)],
        compiler_params=pltpu.CompilerParams(dimension_semantics=("arbitrary",) * 3, vmem_limit_bytes=VMEM_LIMIT),
        name="gqa_flash_attention",
    )(q, k, v, kc, vc)


def _merge_kernel(x_ref, y_ref, o_ref, g_ref, mod_ref, wa_ref, wb_ref, wo_ref, wout_ref,
                  ln_ref, wrh_ref, wrl_ref, br_ref, x1_ref, h2_ref, lg_ref, *, alpha):
    d = x_ref.shape[2]
    gact = jax.nn.gelu(y_ref[0]).astype(BF16)
    ssm = _dot(gact, wa_ref[...]) * jax.nn.sigmoid(_dot(gact, wb_ref[...]))
    att = _dot(o_ref[0], wo_ref[...])
    gate = g_ref[0].astype(F32)
    mixed = (gate[:, :d] * ssm + gate[:, d:] * att).astype(BF16)
    mix = _dot(mixed, wout_ref[...])
    g1 = mod_ref[0, 0:1, :]
    x1 = _layer_norm(alpha * x_ref[0] + g1 * mix) * ln_ref[0:1, :] + ln_ref[1:2, :]
    x1_ref[0] = x1
    h2 = _layer_norm(x1) * (1.0 + mod_ref[0, 2:3, :]) + mod_ref[0, 1:2, :]
    h2_ref[0] = h2
    h_hi, h_lo = _split_bf16(h2)
    lg_ref[0] = _dot3(h_hi, h_lo, wrh_ref[...], wrl_ref[...]) + br_ref[...]


def _merge(x, y_ssm, o, gates, mod, w_glu_a, w_glu_b, w_attn_o, w_out, ln, wr_hi, wr_lo, br, alpha):
    bsz, n, d = x.shape
    tm = min(n, 512)
    row = lambda w: pl.BlockSpec((1, tm, w), lambda b, i: (b, i, 0))
    const = lambda a: pl.BlockSpec(a.shape, lambda b, i: (0,) * a.ndim)
    nr = wr_hi.shape[1]
    return pl.pallas_call(
        functools.partial(_merge_kernel, alpha=alpha),
        grid=(bsz, n // tm),
        in_specs=[row(d), row(y_ssm.shape[2]), row(o.shape[2]), row(gates.shape[2]),
                  pl.BlockSpec((1, 3, d), lambda b, i: (b, 0, 0)),
                  const(w_glu_a), const(w_glu_b), const(w_attn_o), const(w_out), const(ln),
                  const(wr_hi), const(wr_lo), const(br)],
        out_specs=[row(d), row(d), row(nr)],
        out_shape=[jax.ShapeDtypeStruct((bsz, n, d), F32), jax.ShapeDtypeStruct((bsz, n, d), F32),
                   jax.ShapeDtypeStruct((bsz, n, nr), F32)],
        compiler_params=pltpu.CompilerParams(vmem_limit_bytes=VMEM_LIMIT),
        name="merge_postnorm_router",
    )(x, y_ssm, o, gates, mod, w_glu_a, w_glu_b, w_attn_o, w_out, ln, wr_hi, wr_lo, br)


def _route_kernel(lt_ref, e_ref, w_ref, r_ref, cnt_ref, run_ref):
    i = pl.program_id(0)
    tn = lt_ref.shape[1]
    n_experts = run_ref.shape[0]

    @pl.when(i == 0)
    def _():
        run_ref[...] = jnp.zeros_like(run_ref)

    gl = lt_ref[0:N_EXPERT_GROUPS, :]
    gmax = jnp.max(gl, axis=0, keepdims=True)
    gi = lax.broadcasted_iota(jnp.int32, gl.shape, 0)
    gidx = jnp.min(jnp.where(gl == gmax, gi, N_EXPERT_GROUPS), axis=0, keepdims=True)
    gw = 1.0 / jnp.sum(jnp.exp(gl - gmax), axis=0, keepdims=True)
    epg = EXPERTS_PER_GROUP
    e_in = lt_ref[8:8 + epg, :]
    for g in range(1, N_EXPERT_GROUPS):
        e_in = jnp.where(gidx == g, lt_ref[8 + g * epg:8 + (g + 1) * epg, :], e_in)
    ei = lax.broadcasted_iota(jnp.int32, e_in.shape, 0)
    v0 = jnp.max(e_in, axis=0, keepdims=True)
    i0 = jnp.min(jnp.where(e_in == v0, ei, epg), axis=0, keepdims=True)
    rest = jnp.where(ei == i0, -jnp.inf, e_in)
    v1 = jnp.max(rest, axis=0, keepdims=True)
    i1 = jnp.min(jnp.where(rest == v1, ei, epg), axis=0, keepdims=True)
    t = jnp.exp(v1 - v0)
    w0 = gw / (1.0 + t)
    w1 = gw * t / (1.0 + t)
    e0 = gidx * epg + i0
    e1 = gidx * epg + i1
    zi = jnp.zeros_like(e0)
    e_ref[...] = jnp.concatenate([e0, e1] + [zi] * 6, axis=0)
    w_ref[...] = jnp.concatenate([w0, w1] + [jnp.zeros_like(w0)] * 6, axis=0)

    both = jnp.concatenate([e0, e1], axis=1)
    hit = lax.broadcasted_iota(jnp.int32, (n_experts, 2 * tn), 0) == both
    tri = (lax.broadcasted_iota(jnp.int32, (2 * tn, 2 * tn), 0)
           <= lax.broadcasted_iota(jnp.int32, (2 * tn, 2 * tn), 1))
    pref = _dot(jnp.where(hit, 1.0, 0.0).astype(BF16), jnp.where(tri, 1.0, 0.0).astype(BF16))
    run = run_ref[:, 0:1]
    rank = jnp.sum(jnp.where(hit, pref + run, 0.0), axis=0, keepdims=True) - 1.0
    rank = rank.astype(jnp.int32)
    r_ref[...] = jnp.concatenate([rank[:, :tn], rank[:, tn:]] + [zi] * 6, axis=0)
    run_new = jnp.broadcast_to(run + pref[:, 2 * tn - 1:2 * tn], run_ref.shape)
    run_ref[...] = run_new
    cnt_ref[...] = run_new.astype(jnp.int32)


def _route(logits_t, n_experts):
    rows, n = logits_t.shape
    tn = min(n, 256)
    tile = pl.BlockSpec((8, tn), lambda i: (0, i))
    return pl.pallas_call(
        _route_kernel,
        grid=(n // tn,),
        in_specs=[pl.BlockSpec((rows, tn), lambda i: (0, i))],
        out_specs=[tile, tile, tile, pl.BlockSpec((n_experts, LANE), lambda i: (0, 0))],
        out_shape=[jax.ShapeDtypeStruct((8, n), jnp.int32), jax.ShapeDtypeStruct((8, n), F32),
                   jax.ShapeDtypeStruct((8, n), jnp.int32), jax.ShapeDtypeStruct((n_experts, LANE), jnp.int32)],
        scratch_shapes=[pltpu.VMEM((n_experts, LANE), F32)],
        compiler_params=pltpu.CompilerParams(dimension_semantics=("arbitrary",)),
        name="route_top2_rank",
    )(logits_t)


def _dispatch_kernel(pos_ref, starts_ref, ends_ref, h_ref, x_hbm, zbuf, sem, zsem,
                     *, n_tok, n_experts, tm_e):
    i = pl.program_id(0)
    tm = h_ref.shape[0]
    base = i * tm

    @pl.when(i == 0)
    def _():
        zbuf[...] = jnp.zeros_like(zbuf)

        def tail_copy(e):
            start = pl.multiple_of(ends_ref[e] - tm_e, tm_e)
            return pltpu.make_async_copy(zbuf, x_hbm.at[pl.ds(start, tm_e), :], zsem)

        def fill(e, c):
            @pl.when(ends_ref[e] > starts_ref[e])
            def _():
                tail_copy(e).start()
            return c

        def fill_wait(e, c):
            @pl.when(ends_ref[e] > starts_ref[e])
            def _():
                tail_copy(e).wait()
            return c

        def free_copy(t):
            return pltpu.make_async_copy(zbuf, x_hbm.at[pl.ds(pl.multiple_of(t * tm_e, tm_e), tm_e), :], zsem)

        def free_fill(t, c):
            free_copy(t).start()
            return c

        def free_wait(t, c):
            free_copy(t).wait()
            return c

        first_free = ends_ref[n_experts - 1] // tm_e
        n_tiles = x_hbm.shape[0] // tm_e
        lax.fori_loop(0, n_experts, fill, 0)
        lax.fori_loop(first_free, n_tiles, free_fill, 0)
        lax.fori_loop(0, n_experts, fill_wait, 0)
        lax.fori_loop(first_free, n_tiles, free_wait, 0)

    def row_copy(r, slot):
        p = pos_ref[slot * n_tok + base + r]
        return pltpu.make_async_copy(h_ref.at[pl.ds(r, 1), :], x_hbm.at[pl.ds(p, 1), :], sem)

    def issue(r, c):
        row_copy(r, 0).start()
        row_copy(r, 1).start()
        return c

    def drain(r, c):
        row_copy(r, 0).wait()
        row_copy(r, 1).wait()
        return c

    lax.fori_loop(0, tm, issue, 0, unroll=DMA_UNROLL)
    lax.fori_loop(0, tm, drain, 0, unroll=DMA_UNROLL)


def _dispatch(h2, pos, starts, ends, n_rows, tm_e):
    n_tok, d = h2.shape
    tm = min(n_tok, 512)
    grid_spec = pltpu.PrefetchScalarGridSpec(
        num_scalar_prefetch=3,
        grid=(n_tok // tm,),
        in_specs=[pl.BlockSpec((tm, d), lambda i, *_: (i, 0))],
        out_specs=pl.BlockSpec(memory_space=pl.ANY),
        scratch_shapes=[pltpu.VMEM((tm_e, d), F32), pltpu.SemaphoreType.DMA(()), pltpu.SemaphoreType.DMA(())],
    )
    return pl.pallas_call(
        functools.partial(_dispatch_kernel, n_tok=n_tok, n_experts=starts.shape[0], tm_e=tm_e),
        grid_spec=grid_spec,
        out_shape=jax.ShapeDtypeStruct((n_rows, d), F32),
        compiler_params=pltpu.CompilerParams(dimension_semantics=("arbitrary",), vmem_limit_bytes=VMEM_LIMIT),
        name="dispatch_rows",
    )(pos, starts, ends, h2)


def _expert_kernel(tile_e_ref, used_ref, x_ref, wg_ref, wu_ref, wd_ref, y_ref, wg_b, wu_b, wd_b):
    i = pl.program_id(0)
    used = i < used_ref[0]
    fresh = jnp.logical_or(i == 0, tile_e_ref[i] != tile_e_ref[jnp.maximum(i - 1, 0)])

    @pl.when(jnp.logical_and(used, fresh))
    def _():
        wg_b[...] = wg_ref[0].astype(BF16)
        wu_b[...] = wu_ref[0].astype(BF16)
        wd_b[...] = wd_ref[0].astype(BF16)

    @pl.when(used)
    def _():
        xb = x_ref[...].astype(BF16)
        a = _dot(xb, wg_b[...])
        b = _dot(xb, wu_b[...])
        y_ref[...] = _dot((a * jax.nn.sigmoid(a) * b).astype(BF16), wd_b[...])

    @pl.when(jnp.logical_not(used))
    def _():
        y_ref[...] = jnp.zeros_like(y_ref)


def _experts(x_sorted, tile_expert, n_used, w_gate, w_up, w_down, tm):
    n_rows, d = x_sorted.shape
    d_e = w_gate.shape[2]
    last = lambda i, used: jnp.minimum(i, used[0] - 1)
    grid_spec = pltpu.PrefetchScalarGridSpec(
        num_scalar_prefetch=2,
        grid=(n_rows // tm,),
        in_specs=[pl.BlockSpec((tm, d), lambda i, te, used: (last(i, used), 0)),
                  pl.BlockSpec((1, d, d_e), lambda i, te, used: (te[i], 0, 0)),
                  pl.BlockSpec((1, d, d_e), lambda i, te, used: (te[i], 0, 0)),
                  pl.BlockSpec((1, d_e, d), lambda i, te, used: (te[i], 0, 0))],
        out_specs=pl.BlockSpec((tm, d), lambda i, te, used: (i, 0)),
        scratch_shapes=[pltpu.VMEM((d, d_e), BF16), pltpu.VMEM((d, d_e), BF16), pltpu.VMEM((d_e, d), BF16)],
    )
    return pl.pallas_call(
        _expert_kernel,
        grid_spec=grid_spec,
        out_shape=jax.ShapeDtypeStruct((n_rows, d), F32),
        compiler_params=pltpu.CompilerParams(dimension_semantics=("arbitrary",), vmem_limit_bytes=VMEM_LIMIT),
        name="routed_experts",
    )(tile_expert, n_used, x_sorted, w_gate, w_up, w_down)


def _final_kernel(pos_ref, x1_ref, w_ref, mod_ref, ln_ref, y_hbm, o_ref, ybuf, sems, *, alpha, n_tok):
    i = pl.program_id(0)
    n_steps = pl.num_programs(0)
    tm = x1_ref.shape[0]

    def row_copy(step, r, slot):
        buf = step % 2
        p = pos_ref[slot * n_tok + step * tm + r]
        return pltpu.make_async_copy(y_hbm.at[pl.ds(p, 1), :], ybuf.at[buf, slot, pl.ds(r, 1), :], sems.at[buf])

    def issue_tile(step):
        def issue(r, c):
            row_copy(step, r, 0).start()
            row_copy(step, r, 1).start()
            return c
        lax.fori_loop(0, tm, issue, 0, unroll=DMA_UNROLL)

    @pl.when(i == 0)
    def _():
        issue_tile(i)

    @pl.when(i + 1 < n_steps)
    def _():
        issue_tile(i + 1)

    def drain(r, c):
        row_copy(i, r, 0).wait()
        row_copy(i, r, 1).wait()
        return c

    lax.fori_loop(0, tm, drain, 0, unroll=DMA_UNROLL)
    cur = i % 2
    moe = w_ref[:, 0:1] * ybuf[cur, 0] + w_ref[:, 1:2] * ybuf[cur, 1]
    z = alpha * x1_ref[...] + mod_ref[0] * moe
    o_ref[...] = _layer_norm(z) * ln_ref[0:1, :] + ln_ref[1:2, :]


def _final(x1, pos, wts, y_sorted, g2, ln, alpha, seq):
    n_tok, d = x1.shape
    tm = min(seq, 256)
    per_b = seq // tm
    grid_spec = pltpu.PrefetchScalarGridSpec(
        num_scalar_prefetch=1,
        grid=(n_tok // tm,),
        in_specs=[pl.BlockSpec((tm, d), lambda i, *_: (i, 0)),
                  pl.BlockSpec((tm, 2), lambda i, *_: (i, 0)),
                  pl.BlockSpec((1, 1, d), lambda i, *_: (i // per_b, 0, 0)),
                  pl.BlockSpec((2, d), lambda i, *_: (0, 0)),
                  pl.BlockSpec(memory_space=pl.ANY)],
        out_specs=pl.BlockSpec((tm, d), lambda i, *_: (i, 0)),
        scratch_shapes=[pltpu.VMEM((2, 2, tm, d), F32), pltpu.SemaphoreType.DMA((2,))],
    )
    return pl.pallas_call(
        functools.partial(_final_kernel, alpha=alpha, n_tok=n_tok),
        grid_spec=grid_spec,
        out_shape=jax.ShapeDtypeStruct((n_tok, d), F32),
        compiler_params=pltpu.CompilerParams(dimension_semantics=("arbitrary",), vmem_limit_bytes=VMEM_LIMIT),
        name="combine_postnorm",
    )(pos, x1, wts, g2, ln, y_sorted)


def _tile_plan(counts, n_rows, tm):
    n_experts = counts.shape[0]
    padded = ((counts + tm - 1) // tm) * tm
    ends = jnp.cumsum(padded).astype(jnp.int32)
    starts = ends - padded
    tile_start = jnp.arange(n_rows // tm, dtype=jnp.int32) * tm
    tile_e = jnp.sum((tile_start[:, None] >= ends[None, :]).astype(jnp.int32), axis=1)
    n_used = ends[-1:] // tm
    tile_e = jnp.minimum(tile_e, jnp.max(jnp.where(counts > 0, jnp.arange(n_experts, dtype=jnp.int32), 0)))
    return starts, ends, tile_e, n_used


def kernel(x, c, ctx, c_ctx, w_mod, b_mod, w_in, s5_a_re, s5_a_im, s5_log_dt, s5_b_re, s5_b_im, s5_c_re, s5_c_im, s5_d, w_glu_a, w_glu_b, q_gain, k_gain, w_attn_o, w_out, ln1_g, ln1_b, w_router_group, b_router_group, w_router_expert, b_router_expert, w_exp_gate, w_exp_up, w_exp_down, ln2_g, ln2_b):
    bsz, n_lat, d = x.shape
    n_ctx = ctx.shape[1]
    assert w_mod.shape[0] == DEPTH == 1 and bsz + 1 <= 8
    alpha = (2.0 * DEPTH) ** 0.25
    d_s5 = s5_d.shape[1]
    d_q = w_attn_o.shape[1]
    d_kv = N_KV_HEADS * HEAD_DIM
    n_experts = w_exp_gate.shape[1]

    cond = jnp.zeros((8, d), F32).at[:bsz].set(c).at[bsz].set(c_ctx)
    mod = _adaln(cond, w_mod[0], b_mod[0]).reshape(8, 6, d)
    mod_lat = mod[:bsz]
    mod_ctx = jnp.broadcast_to(mod[bsz:bsz + 1], (bsz, 6, d))

    w_in_b = w_in[0].astype(BF16)
    cos, sin = _rope_tables(n_lat)
    qg, kg = q_gain[0].reshape(1, HEAD_DIM), k_gain[0].reshape(1, HEAD_DIM)
    dims = dict(d_s5=d_s5, d_q=d_q, d_kv=d_kv)
    u, q, k, v, gates = _inproj(x, mod_lat[:, 0:2], w_in_b, cos, sin, qg, kg, latent=True, **dims)
    uc, kc, vc = _inproj(ctx, mod_ctx[:, 0:2], w_in_b, cos[:n_ctx], sin[:n_ctx], qg, kg, latent=False, **dims)

    w_t, w_z, w_c, a_chunk = _s5_weights(s5_a_re[0], s5_a_im[0], s5_log_dt[0], s5_b_re[0], s5_b_im[0],
                                         s5_c_re[0], s5_c_im[0])
    y_ssm = _s5_branch(u, uc, w_t, w_z, w_c, a_chunk, s5_d[0])

    o = _attention(q, k, v, kc, vc)

    n_r = 8 + n_experts
    n_r_pad = ((n_r + LANE - 1) // LANE) * LANE
    w_r = jnp.zeros((d, n_r_pad), F32).at[:, :N_EXPERT_GROUPS].set(w_router_group[0])
    w_r = w_r.at[:, 8:n_r].set(w_router_expert[0])
    b_r = jnp.zeros((1, n_r_pad), F32).at[0, :N_EXPERT_GROUPS].set(b_router_group[0])
    b_r = b_r.at[0, 8:n_r].set(b_router_expert[0])
    wr_hi, wr_lo = _split_bf16(w_r)
    ln1 = jnp.stack([ln1_g[0], ln1_b[0]])
    x1, h2, logits = _merge(x, y_ssm, o, gates, mod_lat[:, 2:5], w_glu_a[0].astype(BF16),
                            w_glu_b[0].astype(BF16), w_attn_o[0].astype(BF16), w_out[0].astype(BF16),
                            ln1, wr_hi, wr_lo, b_r, alpha)

    n_tok = bsz * n_lat
    tm_e = EXPERT_TILE
    n_rows = 2 * n_tok + n_experts * tm_e
    logits_t = logits.reshape(n_tok, n_r_pad)[:, :n_r].T
    eid, wts, rank, counts = _route(logits_t, n_experts)
    starts, ends, tile_e, n_used = _tile_plan(counts[:, 0], n_rows, tm_e)
    own = eid[:2, :, None] == jnp.arange(n_experts, dtype=jnp.int32)
    pos = (jnp.sum(jnp.where(own, starts, 0), axis=-1) + rank[:2]).reshape(-1)
    x_sorted = _dispatch(h2.reshape(n_tok, d), pos, starts, ends, n_rows, tm_e)
    y_sorted = _experts(x_sorted, tile_e, n_used, w_exp_gate[0], w_exp_up[0], w_exp_down[0], tm_e)
    ln2 = jnp.stack([ln2_g[0], ln2_b[0]])
    out = _final(x1.reshape(n_tok, d), pos, wts[:2].T, y_sorted, mod_lat[:, 5:6], ln2, alpha, n_lat)
    return out.reshape(bsz, n_lat, d)
```

```python
import functools
import math

import jax
import jax.numpy as jnp
from jax import lax
from jax.experimental import pallas as pl
from jax.experimental.pallas import tpu as pltpu

GRID_W = 64
S5_GROUP_CH = 16
S5_STATE = 64
HEAD_DIM = 128
N_KV_HEADS = 2
ROPE_THETA = 10000.0
N_EXPERT_GROUPS = 4
EXPERTS_PER_GROUP = 8
NORM_EPS = 1e-6
DEPTH = 1

S5_CHUNK = 16
S5_GROUP_BLOCK = 8
EXPERT_TILE = 256
DMA_UNROLL = 8
ATTN_Q_TILE = 512
ATTN_KV_CHUNK = 3072
MERGE_SUB_ROWS = 256
LANE = 128
VMEM_LIMIT = 56 * 1024 * 1024

F32 = jnp.float32
BF16 = jnp.bfloat16


def _layer_norm(x):
    mu = jnp.mean(x, axis=-1, keepdims=True)
    xc = x - mu
    var = jnp.mean(xc * xc, axis=-1, keepdims=True)
    return xc * lax.rsqrt(var + NORM_EPS)


def _split_bf16(a):
    hi = a.astype(BF16)
    lo = (a - hi.astype(F32)).astype(BF16)
    return hi, lo


def _dot(a, b):
    return jnp.dot(a, b, preferred_element_type=F32)


def _dot3(a_hi, a_lo, b_hi, b_lo):
    return _dot(a_hi, b_hi) + _dot(a_hi, b_lo) + _dot(a_lo, b_hi)


def _adaln_kernel(c_ref, w_ref, b_ref, o_ref):
    c = c_ref[...]
    s = c * jax.nn.sigmoid(c)
    s_hi, s_lo = _split_bf16(s)
    w_hi, w_lo = _split_bf16(w_ref[...])
    o_ref[...] = _dot3(s_hi, s_lo, w_hi, w_lo) + b_ref[...]


def _adaln(cond, w, b):
    rows, d = cond.shape
    n = w.shape[1]
    tn = min(n, 1024)
    return pl.pallas_call(
        _adaln_kernel,
        grid=(n // tn,),
        in_specs=[pl.BlockSpec((rows, d), lambda j: (0, 0)),
                  pl.BlockSpec((d, tn), lambda j: (0, j)),
                  pl.BlockSpec((1, tn), lambda j: (0, j))],
        out_specs=pl.BlockSpec((rows, tn), lambda j: (0, j)),
        out_shape=jax.ShapeDtypeStruct((rows, n), F32),
        compiler_params=pltpu.CompilerParams(vmem_limit_bytes=VMEM_LIMIT),
        name="adaln",
    )(cond, w, b.reshape(1, n))


def _rms_rope(t, gain, cos, sin_signed, first_half, scale):
    r = lax.rsqrt(jnp.mean(t * t, axis=-1, keepdims=True) + NORM_EPS)
    tn = t * r * gain
    if cos is not None:
        partner = jnp.where(first_half, pltpu.roll(tn, HEAD_DIM - 32, 1), pltpu.roll(tn, 32, 1))
        tn = tn * cos + partner * sin_signed
    if scale != 1.0:
        tn = tn * scale
    return tn


def _inproj_kernel(x_ref, mod_ref, w_ref, cos_ref, sin_ref, qg_ref, kg_ref, *out_refs,
                   d_s5, d_q, d_kv, latent):
    x = x_ref[0]
    shift = mod_ref[0, 0:1, :]
    scale = mod_ref[0, 1:2, :]
    h = (_layer_norm(x) * (1.0 + scale) + shift).astype(BF16)
    o_q = d_s5
    o_k = o_q + d_q
    o_v = o_k + d_kv
    o_g = o_v + d_kv
    if latent:
        u_ref, q_ref, k_ref, v_ref, g_ref = out_refs
        cos = cos_ref[...]
        sin = sin_ref[...]
        lane = lax.broadcasted_iota(jnp.int32, cos.shape, 1)
        first_half = (lane % 64) < 32
    else:
        u_ref, k_ref, v_ref = out_refs
        cos = sin = first_half = None
    k = _dot(h, w_ref[:, o_k:o_v])
    if latent:
        q = _dot(h, w_ref[:, o_q:o_k])
    for hd in range(d_kv // HEAD_DIM):
        sl = slice(hd * HEAD_DIM, (hd + 1) * HEAD_DIM)
        k_ref[0, :, sl] = _rms_rope(k[:, sl], kg_ref[...], cos, sin, first_half, 1.0).astype(BF16)
    u_ref[0] = _dot(h, w_ref[:, 0:d_s5])
    v_ref[0] = _dot(h, w_ref[:, o_v:o_g]).astype(BF16)
    if latent:
        d_half = (w_ref.shape[1] - o_g) // 2
        g_lo = _dot(h, w_ref[:, o_g:o_g + d_half])
        q_scale = HEAD_DIM ** -0.5 * math.log2(math.e)
        n_qh = d_q // HEAD_DIM
        for hd in range(n_qh // 2):
            sl = slice(hd * HEAD_DIM, (hd + 1) * HEAD_DIM)
            q_ref[0, :, sl] = _rms_rope(q[:, sl], qg_ref[...], cos, sin, first_half, q_scale).astype(BF16)
        g_hi = _dot(h, w_ref[:, o_g + d_half:])
        for hd in range(n_qh // 2, n_qh):
            sl = slice(hd * HEAD_DIM, (hd + 1) * HEAD_DIM)
            q_ref[0, :, sl] = _rms_rope(q[:, sl], qg_ref[...], cos, sin, first_half, q_scale).astype(BF16)
        g_ref[0, :, 0:d_half] = jax.nn.sigmoid(g_lo).astype(BF16)
        g_ref[0, :, d_half:] = jax.nn.sigmoid(g_hi).astype(BF16)


def _inproj(x, mod, w_in, cos, sin, q_gain, k_gain, *, d_s5, d_q, d_kv, latent):
    bsz, n, d = x.shape
    n_in = w_in.shape[1]
    d_gate = n_in - d_s5 - d_q - 2 * d_kv
    tm = min(n, 512)
    row = lambda w: pl.BlockSpec((1, tm, w), lambda b, i: (b, i, 0))
    out_shape = [jax.ShapeDtypeStruct((bsz, n, d_s5), F32)]
    out_specs = [row(d_s5)]
    if latent:
        out_shape.append(jax.ShapeDtypeStruct((bsz, n, d_q), BF16))
        out_specs.append(row(d_q))
    out_shape += [jax.ShapeDtypeStruct((bsz, n, d_kv), BF16)] * 2
    out_specs += [row(d_kv)] * 2
    if latent:
        out_shape.append(jax.ShapeDtypeStruct((bsz, n, d_gate), BF16))
        out_specs.append(row(d_gate))
    return pl.pallas_call(
        functools.partial(_inproj_kernel, d_s5=d_s5, d_q=d_q, d_kv=d_kv, latent=latent),
        grid=(bsz, n // tm),
        in_specs=[row(d),
                  pl.BlockSpec((1, 2, d), lambda b, i: (b, 0, 0)),
                  pl.BlockSpec((d, n_in), lambda b, i: (0, 0), pipeline_mode=pl.Buffered(1)),
                  pl.BlockSpec((tm, HEAD_DIM), lambda b, i: (i, 0)),
                  pl.BlockSpec((tm, HEAD_DIM), lambda b, i: (i, 0)),
                  pl.BlockSpec((1, HEAD_DIM), lambda b, i: (0, 0)),
                  pl.BlockSpec((1, HEAD_DIM), lambda b, i: (0, 0))],
        out_specs=out_specs,
        out_shape=out_shape,
        compiler_params=pltpu.CompilerParams(vmem_limit_bytes=VMEM_LIMIT),
        name="inproj_latent" if latent else "inproj_context",
    )(x, mod, w_in, cos, sin, q_gain, k_gain)


def _rope_tables(n_lat):
    rows = n_lat // GRID_W
    axis_dim = HEAD_DIM // 2
    inv = ROPE_THETA ** (-jnp.arange(0, axis_dim, 2, dtype=F32) / axis_dim)
    ang_r = jnp.arange(rows, dtype=F32)[:, None] * inv
    ang_c = jnp.arange(GRID_W, dtype=F32)[:, None] * inv
    per_row = lambda t: jnp.repeat(t, GRID_W, axis=0)
    per_col = lambda t: jnp.tile(t, (rows, 1))
    cos_r, sin_r = per_row(jnp.cos(ang_r)), per_row(jnp.sin(ang_r))
    cos_c, sin_c = per_col(jnp.cos(ang_c)), per_col(jnp.sin(ang_c))
    cos = jnp.concatenate([cos_r, cos_r, cos_c, cos_c], axis=1)
    sin = jnp.concatenate([-sin_r, sin_r, -sin_c, sin_c], axis=1)
    return cos, sin


def _s5_weights(a_re, a_im, log_dt, b_re, b_im, c_re, c_im):
    hp = lax.Precision.HIGHEST
    lc = S5_CHUNK
    dt = jnp.exp(log_dt)[..., None]
    lam_re, lam_im = a_re * dt, a_im * dt
    ea = jnp.exp(lam_re)
    ab_re, ab_im = ea * jnp.cos(lam_im), ea * jnp.sin(lam_im)
    den = a_re * a_re + a_im * a_im
    nr, ni = ab_re - 1.0, ab_im
    rr = (nr * a_re + ni * a_im) / den
    ri = (ni * a_re - nr * a_im) / den
    bb_re = rr[..., None] * b_re - ri[..., None] * b_im
    bb_im = rr[..., None] * b_im + ri[..., None] * b_re
    kk = jnp.arange(lc + 1, dtype=F32)[:, None, None, None]
    pk_mag = jnp.exp(kk * lam_re)
    pk_re, pk_im = pk_mag * jnp.cos(kk * lam_im), pk_mag * jnp.sin(kk * lam_im)
    n_g = a_re.shape[1]
    kw = lc * S5_GROUP_CH
    rows = lambda t: jnp.moveaxis(t, 0, 1).reshape(n_g, kw, t.shape[-1])
    bt_re, bt_im = jnp.swapaxes(bb_re, -1, -2), jnp.swapaxes(bb_im, -1, -2)
    pw_re, pw_im = pk_re[:lc, :, :, None, :], pk_im[:lc, :, :, None, :]
    akb_re = pw_re * bt_re - pw_im * bt_im
    akb_im = pw_re * bt_im + pw_im * bt_re
    kern = (jnp.einsum('dgop,kdgip->dgkoi', c_re, akb_re, precision=hp)
            - jnp.einsum('dgop,kdgip->dgkoi', c_im, akb_im, precision=hp))
    strip = jnp.concatenate([kern[1][:, :0:-1], kern[0][:, :1] + kern[1][:, :1], kern[0][:, 1:]], axis=1)
    strip = strip.reshape(n_g, (2 * lc - 1) * S5_GROUP_CH, S5_GROUP_CH)
    w_t = jnp.concatenate([strip[:, S5_GROUP_CH * (lc - 1 - s):S5_GROUP_CH * (lc - 1 - s) + kw]
                           for s in range(lc)], axis=2)
    w_t = jnp.swapaxes(w_t, 1, 2)
    w_z = jnp.concatenate([rows(akb_re[::-1, 0]), rows(akb_re[:, 1]),
                           rows(akb_im[::-1, 0]), rows(akb_im[:, 1])], axis=2)
    pf_re, pf_im = pk_re[1:, 0], pk_im[1:, 0]
    pr_re, pr_im = pk_re[lc:0:-1, 1], pk_im[lc:0:-1, 1]

    def cpow(cr, ci, pr, pi):
        return cr[None] * pr[:, :, None, :] - ci[None] * pi[:, :, None, :], \
               cr[None] * pi[:, :, None, :] + ci[None] * pr[:, :, None, :]

    cf_re, cf_im = cpow(c_re[0], c_im[0], pf_re, pf_im)
    cr_re, cr_im = cpow(c_re[1], c_im[1], pr_re, pr_im)
    w_c = jnp.concatenate([rows(cf_re), rows(cr_re), rows(-cf_im), rows(-cr_im)], axis=2)
    w_c = jnp.swapaxes(w_c, 1, 2)
    a_chunk = jnp.concatenate([pk_re[lc, 0], pk_re[lc, 1], pk_im[lc, 0], pk_im[lc, 1]], axis=1)
    return w_t.astype(BF16), w_z.astype(BF16), w_c.astype(BF16), a_chunk


def _regroup_rows(n_rows):
    return min(n_rows, 64)


def _s5_kernel(ul_ref, uc_ref, wt_ref, wz_ref, wc_ref, a_ref, d_ref, y_ref,
               lhs_l, lhs_c, zl_re, zl_im, zc_re, zc_im, yg_ref, *, nb, ncl, ncc):
    gb = S5_GROUP_BLOCK
    lc, ch = S5_CHUNK, S5_GROUP_CH
    per_tile = LANE // ch
    nl = nb * ncl
    ncx = nb * ncc
    half = 2 * S5_STATE
    lane_blk = lambda rows: lax.broadcasted_iota(jnp.int32, (rows, LANE), 1) // ch

    def block_transpose(arrs, blk):
        n = len(arrs)
        rolled = []
        for k in range(n):
            w = arrs[k]
            for g in range(1, n):
                w = jnp.where(blk == g, arrs[(g + k) % n], w)
            rolled.append(w if k == 0 else pltpu.roll(w, ch * k, 1))
        outs = []
        for b in range(n):
            o = rolled[0]
            for k in range(1, n):
                o = jnp.where(blk == (b + k) % n, rolled[k], o)
            outs.append(o)
        return outs

    def gather_chunks(src_ref, dst_ref, n_chunks):
        rb = _regroup_rows(n_chunks)
        blk = lane_blk(rb)

        def step(i, carry):
            r0 = pl.multiple_of(i * rb, rb)
            for hh in range(lc // per_tile):
                ut = [src_ref[pl.ds(r0 * lc + hh * per_tile + j, rb, stride=lc), :] for j in range(per_tile)]
                for g, out in enumerate(block_transpose(ut, blk)):
                    dst_ref[g, pl.ds(r0, rb), hh * LANE:(hh + 1) * LANE] = out.astype(BF16)
            return carry

        lax.fori_loop(0, n_chunks // rb, step, 0)

    gather_chunks(ul_ref, lhs_l, nl)
    gather_chunks(uc_ref, lhs_c, ncx)

    for g in range(gb):
        zl = _dot(lhs_l[g], wz_ref[g])
        zl_re[pl.ds(g, nl, stride=gb), :] = zl[:, :half]
        zl_im[pl.ds(g, nl, stride=gb), :] = zl[:, half:]
        zc = _dot(lhs_c[g], wz_ref[g])
        zc_re[pl.ds(g, ncx, stride=gb), :] = zc[:, :half]
        zc_im[pl.ds(g, ncx, stride=gb), :] = zc[:, half:]
    a_re = a_ref[:, :half]
    a_im = a_ref[:, half:]
    fwd = lax.broadcasted_iota(jnp.int32, (gb, half), 1) < S5_STATE

    def advance(h_re, h_im, z_re, z_im):
        return a_re * h_re - a_im * h_im + z_re, a_re * h_im + a_im * h_re + z_im

    def ctx_step(i, carry):
        out = []
        for b in range(nb):
            sl_f = pl.ds(pl.multiple_of((b * ncc + i) * gb, gb), gb)
            sl_r = pl.ds(pl.multiple_of((b * ncc + ncc - 1 - i) * gb, gb), gb)
            z_re = jnp.where(fwd, zc_re[sl_f, :], zc_re[sl_r, :])
            z_im = jnp.where(fwd, zc_im[sl_f, :], zc_im[sl_r, :])
            out.extend(advance(carry[2 * b], carry[2 * b + 1], z_re, z_im))
        return tuple(out)

    def lat_step(i, carry):
        out = []
        for b in range(nb):
            h_re, h_im = carry[2 * b], carry[2 * b + 1]
            sl_f = pl.ds(pl.multiple_of((b * ncl + i) * gb, gb), gb)
            sl_r = pl.ds(pl.multiple_of((b * ncl + ncl - 1 - i) * gb, gb), gb)
            f_re, f_im, r_re, r_im = zl_re[sl_f, :], zl_im[sl_f, :], zl_re[sl_r, :], zl_im[sl_r, :]
            zl_re[sl_f, :] = jnp.where(fwd, h_re, f_re)
            zl_im[sl_f, :] = jnp.where(fwd, h_im, f_im)
            zl_re[sl_r, :] = jnp.where(fwd, r_re, h_re)
            zl_im[sl_r, :] = jnp.where(fwd, r_im, h_im)
            out.extend(advance(h_re, h_im, jnp.where(fwd, f_re, r_re), jnp.where(fwd, f_im, r_im)))
        return tuple(out)

    zero = jnp.zeros((gb, half), F32)
    carry = lax.fori_loop(0, ncc, ctx_step, (zero,) * (2 * nb))
    lax.fori_loop(0, ncl, lat_step, carry)
    for g in range(gb):
        rows = pl.ds(g, nl, stride=gb)
        h_in = jnp.concatenate([zl_re[rows, :], zl_im[rows, :]], axis=1).astype(BF16)
        yg_ref[g] = _dot(lhs_l[g], wt_ref[g]) + _dot(h_in, wc_ref[g])

    rb = _regroup_rows(nl)
    blk = lane_blk(rb)
    d_row = d_ref[0]

    def scatter_step(i, carry):
        r0 = pl.multiple_of(i * rb, rb)
        for hh in range(lc // per_tile):
            yt = [yg_ref[g, pl.ds(r0, rb), hh * LANE:(hh + 1) * LANE] for g in range(gb)]
            for j, out in enumerate(block_transpose(yt, blk)):
                rows = pl.ds(r0 * lc + hh * per_tile + j, rb, stride=lc)
                y_ref[rows, :] = out + ul_ref[rows, :] * d_row
        return carry

    lax.fori_loop(0, nl // rb, scatter_step, 0)


def _s5_branch(u, uc, w_t, w_z, w_c, a_chunk, s5_d):
    bsz, n, width = u.shape
    n_ctx = uc.shape[1]
    lc, ch = S5_CHUNK, S5_GROUP_CH
    n_g = width // ch
    ncl, ncc = n // lc, n_ctx // lc
    gb = S5_GROUP_BLOCK
    assert ncl % 2 == 0 and n_g % gb == 0 and gb * ch == LANE
    kw = lc * ch
    nl, ncx = bsz * ncl, bsz * ncc
    assert nl % _regroup_rows(nl) == 0 and ncx % _regroup_rows(ncx) == 0
    once = pl.Buffered(1)
    slab = lambda rows: pl.BlockSpec((rows, LANE), lambda i: (0, i), pipeline_mode=once)
    blk3 = lambda r, c: pl.BlockSpec((gb, r, c), lambda i: (i, 0, 0))
    half = 2 * S5_STATE
    y = pl.pallas_call(
        functools.partial(_s5_kernel, nb=bsz, ncl=ncl, ncc=ncc),
        grid=(n_g // gb,),
        in_specs=[slab(bsz * n), slab(bsz * n_ctx), blk3(kw, kw), blk3(kw, 2 * half), blk3(2 * half, kw),
                  pl.BlockSpec((gb, 2 * half), lambda i: (i, 0)),
                  pl.BlockSpec((1, 1, LANE), lambda i: (i, 0, 0))],
        out_specs=slab(bsz * n),
        out_shape=jax.ShapeDtypeStruct((bsz * n, width), F32),
        scratch_shapes=[pltpu.VMEM((gb, nl, kw), BF16), pltpu.VMEM((gb, ncx, kw), BF16),
                        pltpu.VMEM((gb * nl, half), F32), pltpu.VMEM((gb * nl, half), F32),
                        pltpu.VMEM((gb * ncx, half), F32), pltpu.VMEM((gb * ncx, half), F32),
                        pltpu.VMEM((gb, nl, kw), F32)],
        compiler_params=pltpu.CompilerParams(vmem_limit_bytes=VMEM_LIMIT),
        name="s5_chunked_scan",
    )(u.reshape(bsz * n, width), uc.reshape(bsz * n_ctx, width), w_t, w_z, w_c, a_chunk,
      s5_d.reshape(n_g // gb, 1, LANE))
    return y.reshape(bsz, n, width)


def _attn_kernel(q_ref, k_ref, v_ref, kc_ref, vc_ref, o_ref, k_all, v_all, *, tk, group):
    tq = q_ref.shape[1]
    n_lat, n_ctx = k_ref.shape[1], kc_ref.shape[1]
    n_k = n_lat + n_ctx

    @pl.when(pl.program_id(2) == 0)
    def _():
        k_all[0:n_lat, :] = k_ref[0]
        k_all[n_lat:n_k, :] = kc_ref[0]
        v_all[0:n_lat, 0:HEAD_DIM] = v_ref[0]
        v_all[n_lat:n_k, 0:HEAD_DIM] = vc_ref[0]
        v_all[:, HEAD_DIM:] = jnp.ones((n_k, HEAD_DIM), BF16)

    qs = [q_ref[0, :, h * HEAD_DIM:(h + 1) * HEAD_DIM] for h in range(group)]

    def body(c, carry):
        start = pl.multiple_of(c * tk, tk)
        ks = k_all[pl.ds(start, tk), :]
        vs = v_all[pl.ds(start, tk), :]
        out = []
        score = lambda h: lax.dot_general(qs[h], ks, (((1,), (1,)), ((), ())), preferred_element_type=F32)
        s_next = score(0)
        for h in range(group):
            m, acc = carry[2 * h:2 * h + 2]
            s = s_next
            if h + 1 < group:
                s_next = score(h + 1)
            m_new = jnp.maximum(m, jnp.max(s, axis=-1, keepdims=True))
            p = jnp.exp2(s - m_new)
            alpha = jnp.exp2(m - m_new)
            acc = alpha * acc + _dot(p.astype(BF16), vs)
            out.extend((m_new, acc))
        return tuple(out)

    init = (jnp.full((tq, 1), -jnp.inf, F32), jnp.zeros((tq, 2 * HEAD_DIM), F32)) * group
    fin = lax.fori_loop(0, n_k // tk, body, init)
    for h in range(group):
        acc = fin[2 * h + 1]
        o_ref[0, :, h * HEAD_DIM:(h + 1) * HEAD_DIM] = (acc[:, :HEAD_DIM] / acc[:, HEAD_DIM:]).astype(o_ref.dtype)


def _pick_divisor(n, pref):
    best = LANE
    for t in range(LANE, pref + 1, LANE):
        if n % t == 0:
            best = t
    return best


def _attention(q, k, v, kc, vc):
    bsz, n, dq = q.shape
    n_c, dkv = kc.shape[1], k.shape[2]
    n_kv = dkv // HEAD_DIM
    group = dq // dkv
    tq = min(n, ATTN_Q_TILE)
    tk = _pick_divisor(n + n_c, ATTN_KV_CHUNK)
    kv_spec = lambda rows: pl.BlockSpec((1, rows, HEAD_DIM), lambda b, h, i: (b, 0, h))
    return pl.pallas_call(
        functools.partial(_attn_kernel, tk=tk, group=group),
        grid=(bsz, n_kv, n // tq),
        in_specs=[pl.BlockSpec((1, tq, group * HEAD_DIM), lambda b, h, i: (b, i, h)),
                  kv_spec(n), kv_spec(n), kv_spec(n_c), kv_spec(n_c)],
        out_specs=pl.BlockSpec((1, tq, group * HEAD_DIM), lambda b, h, i: (b, i, h)),
        out_shape=jax.ShapeDtypeStruct((bsz, n, dq), BF16),
        scratch_shapes=[pltpu.VMEM((n + n_c, HEAD_DIM), BF16), pltpu.VMEM((n + n_c, 2 * HEAD_DIM), BF16)],
        compiler_params=pltpu.CompilerParams(dimension_semantics=("arbitrary",) * 3, vmem_limit_bytes=VMEM_LIMIT),
        name="gqa_flash_attention",
    )(q, k, v, kc, vc)


def _merge_kernel(x_ref, y_ref, o_ref, g_ref, mod_ref, wa_ref, wb_ref, wo_ref, wout_ref,
                  ln_ref, wrh_ref, wrl_ref, br_ref, x1_ref, h2_ref, lg_ref, *, alpha):
    d = x_ref.shape[2]
    tm = x_ref.shape[1]
    g1 = mod_ref[0, 0:1, :]
    blocks = [slice(r, r + MERGE_SUB_ROWS) for r in range(0, tm, MERGE_SUB_ROWS)]
    stage1 = []
    for rows in blocks:
        att = _dot(o_ref[0, rows, :], wo_ref[...])
        gact = jax.nn.gelu(y_ref[0, rows, :]).astype(BF16)
        stage1.append((att, _dot(gact, wa_ref[...]), _dot(gact, wb_ref[...])))
    stage2 = []
    for rows, (att, a, b) in zip(blocks, stage1):
        gate = g_ref[0, rows, :].astype(F32)
        mixed = (gate[:, :d] * (a * jax.nn.sigmoid(b)) + gate[:, d:] * att).astype(BF16)
        stage2.append(_dot(mixed, wout_ref[...]))
    for rows, mix in zip(blocks, stage2):
        x1 = _layer_norm(alpha * x_ref[0, rows, :] + g1 * mix) * ln_ref[0:1, :] + ln_ref[1:2, :]
        x1_ref[0, rows, :] = x1
        h2 = _layer_norm(x1) * (1.0 + mod_ref[0, 2:3, :]) + mod_ref[0, 1:2, :]
        h2_ref[0, rows, :] = h2
        h_hi, h_lo = _split_bf16(h2)
        lg_ref[0, rows, :] = _dot3(h_hi, h_lo, wrh_ref[...], wrl_ref[...]) + br_ref[...]


def _merge(x, y_ssm, o, gates, mod, w_glu_a, w_glu_b, w_attn_o, w_out, ln, wr_hi, wr_lo, br, alpha):
    bsz, n, d = x.shape
    tm = min(n, 512)
    row = lambda w: pl.BlockSpec((1, tm, w), lambda b, i: (b, i, 0))
    const = lambda a: pl.BlockSpec(a.shape, lambda b, i: (0,) * a.ndim)
    nr = wr_hi.shape[1]
    return pl.pallas_call(
        functools.partial(_merge_kernel, alpha=alpha),
        grid=(bsz, n // tm),
        in_specs=[row(d), row(y_ssm.shape[2]), row(o.shape[2]), row(gates.shape[2]),
                  pl.BlockSpec((1, 3, d), lambda b, i: (b, 0, 0)),
                  const(w_glu_a), const(w_glu_b), const(w_attn_o), const(w_out), const(ln),
                  const(wr_hi), const(wr_lo), const(br)],
        out_specs=[row(d), row(d), row(nr)],
        out_shape=[jax.ShapeDtypeStruct((bsz, n, d), F32), jax.ShapeDtypeStruct((bsz, n, d), F32),
                   jax.ShapeDtypeStruct((bsz, n, nr), F32)],
        compiler_params=pltpu.CompilerParams(vmem_limit_bytes=VMEM_LIMIT),
        name="merge_postnorm_router",
    )(x, y_ssm, o, gates, mod, w_glu_a, w_glu_b, w_attn_o, w_out, ln, wr_hi, wr_lo, br)


def _route_kernel(lt_ref, e_ref, w_ref, r_ref, cnt_ref, run_ref):
    i = pl.program_id(0)
    tn = lt_ref.shape[1]
    n_experts = run_ref.shape[0]

    @pl.when(i == 0)
    def _():
        run_ref[...] = jnp.zeros_like(run_ref)

    gl = lt_ref[0:N_EXPERT_GROUPS, :]
    gmax = jnp.max(gl, axis=0, keepdims=True)
    gi = lax.broadcasted_iota(jnp.int32, gl.shape, 0)
    gidx = jnp.min(jnp.where(gl == gmax, gi, N_EXPERT_GROUPS), axis=0, keepdims=True)
    gw = 1.0 / jnp.sum(jnp.exp(gl - gmax), axis=0, keepdims=True)
    epg = EXPERTS_PER_GROUP
    e_in = lt_ref[8:8 + epg, :]
    for g in range(1, N_EXPERT_GROUPS):
        e_in = jnp.where(gidx == g, lt_ref[8 + g * epg:8 + (g + 1) * epg, :], e_in)
    ei = lax.broadcasted_iota(jnp.int32, e_in.shape, 0)
    v0 = jnp.max(e_in, axis=0, keepdims=True)
    i0 = jnp.min(jnp.where(e_in == v0, ei, epg), axis=0, keepdims=True)
    rest = jnp.where(ei == i0, -jnp.inf, e_in)
    v1 = jnp.max(rest, axis=0, keepdims=True)
    i1 = jnp.min(jnp.where(rest == v1, ei, epg), axis=0, keepdims=True)
    t = jnp.exp(v1 - v0)
    w0 = gw / (1.0 + t)
    w1 = gw * t / (1.0 + t)
    e0 = gidx * epg + i0
    e1 = gidx * epg + i1
    zi = jnp.zeros_like(e0)
    e_ref[...] = jnp.concatenate([e0, e1] + [zi] * 6, axis=0)
    w_ref[...] = jnp.concatenate([w0, w1] + [jnp.zeros_like(w0)] * 6, axis=0)

    both = jnp.concatenate([e0, e1], axis=1)
    hit = lax.broadcasted_iota(jnp.int32, (n_experts, 2 * tn), 0) == both
    tri = (lax.broadcasted_iota(jnp.int32, (2 * tn, 2 * tn), 0)
           <= lax.broadcasted_iota(jnp.int32, (2 * tn, 2 * tn), 1))
    pref = _dot(jnp.where(hit, 1.0, 0.0).astype(BF16), jnp.where(tri, 1.0, 0.0).astype(BF16))
    run = run_ref[:, 0:1]
    rank = jnp.sum(jnp.where(hit, pref + run, 0.0), axis=0, keepdims=True) - 1.0
    rank = rank.astype(jnp.int32)
    r_ref[...] = jnp.concatenate([rank[:, :tn], rank[:, tn:]] + [zi] * 6, axis=0)
    run_new = jnp.broadcast_to(run + pref[:, 2 * tn - 1:2 * tn], run_ref.shape)
    run_ref[...] = run_new
    cnt_ref[...] = run_new.astype(jnp.int32)


def _route(logits_t, n_experts):
    rows, n = logits_t.shape
    tn = min(n, 256)
    tile = pl.BlockSpec((8, tn), lambda i: (0, i))
    return pl.pallas_call(
        _route_kernel,
        grid=(n // tn,),
        in_specs=[pl.BlockSpec((rows, tn), lambda i: (0, i))],
        out_specs=[tile, tile, tile, pl.BlockSpec((n_experts, LANE), lambda i: (0, 0))],
        out_shape=[jax.ShapeDtypeStruct((8, n), jnp.int32), jax.ShapeDtypeStruct((8, n), F32),
                   jax.ShapeDtypeStruct((8, n), jnp.int32), jax.ShapeDtypeStruct((n_experts, LANE), jnp.int32)],
        scratch_shapes=[pltpu.VMEM((n_experts, LANE), F32)],
        compiler_params=pltpu.CompilerParams(dimension_semantics=("arbitrary",)),
        name="route_top2_rank",
    )(logits_t)


def _dispatch_kernel(pos_ref, starts_ref, ends_ref, h_ref, x_hbm, zbuf, sem, zsem,
                     *, n_tok, n_experts, tm_e):
    i = pl.program_id(0)
    tm = h_ref.shape[0]
    base = i * tm

    @pl.when(i == 0)
    def _():
        zbuf[...] = jnp.zeros_like(zbuf)

        def tail_copy(e):
            start = pl.multiple_of(ends_ref[e] - tm_e, tm_e)
            return pltpu.make_async_copy(zbuf, x_hbm.at[pl.ds(start, tm_e), :], zsem)

        def fill(e, c):
            @pl.when(ends_ref[e] > starts_ref[e])
            def _():
                tail_copy(e).start()
            return c

        def fill_wait(e, c):
            @pl.when(ends_ref[e] > starts_ref[e])
            def _():
                tail_copy(e).wait()
            return c

        def free_copy(t):
            return pltpu.make_async_copy(zbuf, x_hbm.at[pl.ds(pl.multiple_of(t * tm_e, tm_e), tm_e), :], zsem)

        def free_fill(t, c):
            free_copy(t).start()
            return c

        def free_wait(t, c):
            free_copy(t).wait()
            return c

        first_free = ends_ref[n_experts - 1] // tm_e
        n_tiles = x_hbm.shape[0] // tm_e
        lax.fori_loop(0, n_experts, fill, 0)
        lax.fori_loop(first_free, n_tiles, free_fill, 0)
        lax.fori_loop(0, n_experts, fill_wait, 0)
        lax.fori_loop(first_free, n_tiles, free_wait, 0)

    def row_copy(r, slot):
        p = pos_ref[slot * n_tok + base + r]
        return pltpu.make_async_copy(h_ref.at[pl.ds(r, 1), :], x_hbm.at[pl.ds(p, 1), :], sem)

    def issue(r, c):
        row_copy(r, 0).start()
        row_copy(r, 1).start()
        return c

    def drain(r, c):
        row_copy(r, 0).wait()
        row_copy(r, 1).wait()
        return c

    lax.fori_loop(0, tm, issue, 0, unroll=DMA_UNROLL)
    lax.fori_loop(0, tm, drain, 0, unroll=DMA_UNROLL)


def _dispatch(h2, pos, starts, ends, n_rows, tm_e):
    n_tok, d = h2.shape
    tm = min(n_tok, 512)
    grid_spec = pltpu.PrefetchScalarGridSpec(
        num_scalar_prefetch=3,
        grid=(n_tok // tm,),
        in_specs=[pl.BlockSpec((tm, d), lambda i, *_: (i, 0))],
        out_specs=pl.BlockSpec(memory_space=pl.ANY),
        scratch_shapes=[pltpu.VMEM((tm_e, d), F32), pltpu.SemaphoreType.DMA(()), pltpu.SemaphoreType.DMA(())],
    )
    return pl.pallas_call(
        functools.partial(_dispatch_kernel, n_tok=n_tok, n_experts=starts.shape[0], tm_e=tm_e),
        grid_spec=grid_spec,
        out_shape=jax.ShapeDtypeStruct((n_rows, d), F32),
        compiler_params=pltpu.CompilerParams(dimension_semantics=("arbitrary",), vmem_limit_bytes=VMEM_LIMIT),
        name="dispatch_rows",
    )(pos, starts, ends, h2)


def _expert_kernel(tile_e_ref, used_ref, x_ref, wg_ref, wu_ref, wd_ref, y_ref, wg_b, wu_b, wd_b):
    i = pl.program_id(0)
    used = i < used_ref[0]
    fresh = jnp.logical_or(i == 0, tile_e_ref[i] != tile_e_ref[jnp.maximum(i - 1, 0)])

    @pl.when(jnp.logical_and(used, fresh))
    def _():
        wg_b[...] = wg_ref[0].astype(BF16)
        wu_b[...] = wu_ref[0].astype(BF16)
        wd_b[...] = wd_ref[0].astype(BF16)

    @pl.when(used)
    def _():
        xb = x_ref[...].astype(BF16)
        a = _dot(xb, wg_b[...])
        b = _dot(xb, wu_b[...])
        y_ref[...] = _dot((a * jax.nn.sigmoid(a) * b).astype(BF16), wd_b[...])

    @pl.when(jnp.logical_not(used))
    def _():
        y_ref[...] = jnp.zeros_like(y_ref)


def _experts(x_sorted, tile_expert, n_used, w_gate, w_up, w_down, tm):
    n_rows, d = x_sorted.shape
    d_e = w_gate.shape[2]
    last = lambda i, used: jnp.minimum(i, used[0] - 1)
    grid_spec = pltpu.PrefetchScalarGridSpec(
        num_scalar_prefetch=2,
        grid=(n_rows // tm,),
        in_specs=[pl.BlockSpec((tm, d), lambda i, te, used: (last(i, used), 0)),
                  pl.BlockSpec((1, d, d_e), lambda i, te, used: (te[i], 0, 0)),
                  pl.BlockSpec((1, d, d_e), lambda i, te, used: (te[i], 0, 0)),
                  pl.BlockSpec((1, d_e, d), lambda i, te, used: (te[i], 0, 0))],
        out_specs=pl.BlockSpec((tm, d), lambda i, te, used: (i, 0)),
        scratch_shapes=[pltpu.VMEM((d, d_e), BF16), pltpu.VMEM((d, d_e), BF16), pltpu.VMEM((d_e, d), BF16)],
    )
    return pl.pallas_call(
        _expert_kernel,
        grid_spec=grid_spec,
        out_shape=jax.ShapeDtypeStruct((n_rows, d), F32),
        compiler_params=pltpu.CompilerParams(dimension_semantics=("arbitrary",), vmem_limit_bytes=VMEM_LIMIT),
        name="routed_experts",
    )(tile_expert, n_used, x_sorted, w_gate, w_up, w_down)


def _final_kernel(pos_ref, x1_ref, w_ref, mod_ref, ln_ref, y_hbm, o_ref, ybuf, sems, *, alpha, n_tok):
    i = pl.program_id(0)
    n_steps = pl.num_programs(0)
    tm = x1_ref.shape[0]

    def row_copy(step, r, slot):
        buf = step % 2
        p = pos_ref[slot * n_tok + step * tm + r]
        return pltpu.make_async_copy(y_hbm.at[pl.ds(p, 1), :], ybuf.at[buf, slot, pl.ds(r, 1), :], sems.at[buf])

    def issue_tile(step):
        def issue(r, c):
            row_copy(step, r, 0).start()
            row_copy(step, r, 1).start()
            return c
        lax.fori_loop(0, tm, issue, 0, unroll=DMA_UNROLL)

    @pl.when(i == 0)
    def _():
        issue_tile(i)

    @pl.when(i + 1 < n_steps)
    def _():
        issue_tile(i + 1)

    def drain(r, c):
        row_copy(i, r, 0).wait()
        row_copy(i, r, 1).wait()
        return c

    lax.fori_loop(0, tm, drain, 0, unroll=DMA_UNROLL)
    cur = i % 2
    moe = w_ref[:, 0:1] * ybuf[cur, 0] + w_ref[:, 1:2] * ybuf[cur, 1]
    z = alpha * x1_ref[...] + mod_ref[0] * moe
    o_ref[...] = _layer_norm(z) * ln_ref[0:1, :] + ln_ref[1:2, :]


def _final(x1, pos, wts, y_sorted, g2, ln, alpha, seq):
    n_tok, d = x1.shape
    tm = min(seq, 256)
    per_b = seq // tm
    grid_spec = pltpu.PrefetchScalarGridSpec(
        num_scalar_prefetch=1,
        grid=(n_tok // tm,),
        in_specs=[pl.BlockSpec((tm, d), lambda i, *_: (i, 0)),
                  pl.BlockSpec((tm, 2), lambda i, *_: (i, 0)),
                  pl.BlockSpec((1, 1, d), lambda i, *_: (i // per_b, 0, 0)),
                  pl.BlockSpec((2, d), lambda i, *_: (0, 0)),
                  pl.BlockSpec(memory_space=pl.ANY)],
        out_specs=pl.BlockSpec((tm, d), lambda i, *_: (i, 0)),
        scratch_shapes=[pltpu.VMEM((2, 2, tm, d), F32), pltpu.SemaphoreType.DMA((2,))],
    )
    return pl.pallas_call(
        functools.partial(_final_kernel, alpha=alpha, n_tok=n_tok),
        grid_spec=grid_spec,
        out_shape=jax.ShapeDtypeStruct((n_tok, d), F32),
        compiler_params=pltpu.CompilerParams(dimension_semantics=("arbitrary",), vmem_limit_bytes=VMEM_LIMIT),
        name="combine_postnorm",
    )(pos, x1, wts, g2, ln, y_sorted)


def _tile_plan(counts, n_rows, tm):
    n_experts = counts.shape[0]
    padded = ((counts + tm - 1) // tm) * tm
    ends = jnp.cumsum(padded).astype(jnp.int32)
    starts = ends - padded
    tile_start = jnp.arange(n_rows // tm, dtype=jnp.int32) * tm
    tile_e = jnp.sum((tile_start[:, None] >= ends[None, :]).astype(jnp.int32), axis=1)
    n_used = ends[-1:] // tm
    tile_e = jnp.minimum(tile_e, jnp.max(jnp.where(counts > 0, jnp.arange(n_experts, dtype=jnp.int32), 0)))
    return starts, ends, tile_e, n_used


def kernel(x, c, ctx, c_ctx, w_mod, b_mod, w_in, s5_a_re, s5_a_im, s5_log_dt, s5_b_re, s5_b_im, s5_c_re, s5_c_im, s5_d, w_glu_a, w_glu_b, q_gain, k_gain, w_attn_o, w_out, ln1_g, ln1_b, w_router_group, b_router_group, w_router_expert, b_router_expert, w_exp_gate, w_exp_up, w_exp_down, ln2_g, ln2_b):
    bsz, n_lat, d = x.shape
    n_ctx = ctx.shape[1]
    assert w_mod.shape[0] == DEPTH == 1 and bsz + 1 <= 8
    alpha = (2.0 * DEPTH) ** 0.25
    d_s5 = s5_d.shape[1]
    d_q = w_attn_o.shape[1]
    d_kv = N_KV_HEADS * HEAD_DIM
    n_experts = w_exp_gate.shape[1]

    cond = jnp.zeros((8, d), F32).at[:bsz].set(c).at[bsz].set(c_ctx)
    mod = _adaln(cond, w_mod[0], b_mod[0]).reshape(8, 6, d)
    mod_lat = mod[:bsz]
    mod_ctx = jnp.broadcast_to(mod[bsz:bsz + 1], (bsz, 6, d))

    w_in_b = w_in[0].astype(BF16)
    cos, sin = _rope_tables(n_lat)
    qg, kg = q_gain[0].reshape(1, HEAD_DIM), k_gain[0].reshape(1, HEAD_DIM)
    dims = dict(d_s5=d_s5, d_q=d_q, d_kv=d_kv)
    u, q, k, v, gates = _inproj(x, mod_lat[:, 0:2], w_in_b, cos, sin, qg, kg, latent=True, **dims)
    uc, kc, vc = _inproj(ctx, mod_ctx[:, 0:2], w_in_b, cos[:n_ctx], sin[:n_ctx], qg, kg, latent=False, **dims)

    w_t, w_z, w_c, a_chunk = _s5_weights(s5_a_re[0], s5_a_im[0], s5_log_dt[0], s5_b_re[0], s5_b_im[0],
                                         s5_c_re[0], s5_c_im[0])
    y_ssm = _s5_branch(u, uc, w_t, w_z, w_c, a_chunk, s5_d[0])

    o = _attention(q, k, v, kc, vc)

    n_r = 8 + n_experts
    n_r_pad = ((n_r + LANE - 1) // LANE) * LANE
    w_r = jnp.zeros((d, n_r_pad), F32).at[:, :N_EXPERT_GROUPS].set(w_router_group[0])
    w_r = w_r.at[:, 8:n_r].set(w_router_expert[0])
    b_r = jnp.zeros((1, n_r_pad), F32).at[0, :N_EXPERT_GROUPS].set(b_router_group[0])
    b_r = b_r.at[0, 8:n_r].set(b_router_expert[0])
    wr_hi, wr_lo = _split_bf16(w_r)
    ln1 = jnp.stack([ln1_g[0], ln1_b[0]])
    x1, h2, logits = _merge(x, y_ssm, o, gates, mod_lat[:, 2:5], w_glu_a[0].astype(BF16),
                            w_glu_b[0].astype(BF16), w_attn_o[0].astype(BF16), w_out[0].astype(BF16),
                            ln1, wr_hi, wr_lo, b_r, alpha)

    n_tok = bsz * n_lat
    tm_e = EXPERT_TILE
    n_rows = 2 * n_tok + n_experts * tm_e
    logits_t = logits.reshape(n_tok, n_r_pad)[:, :n_r].T
    eid, wts, rank, counts = _route(logits_t, n_experts)
    starts, ends, tile_e, n_used = _tile_plan(counts[:, 0], n_rows, tm_e)
    own = eid[:2, :, None] == jnp.arange(n_experts, dtype=jnp.int32)
    pos = (jnp.sum(jnp.where(own, starts, 0), axis=-1) + rank[:2]).reshape(-1)
    x_sorted = _dispatch(h2.reshape(n_tok, d), pos, starts, ends, n_rows, tm_e)
    y_sorted = _experts(x_sorted, tile_e, n_used, w_exp_gate[0], w_exp_up[0], w_exp_down[0], tm_e)
    ln2 = jnp.stack([ln2_g[0], ln2_b[0]])
    out = _final(x1.reshape(n_tok, d), pos, wts[:2].T, y_sorted, mod_lat[:, 5:6], ln2, alpha, n_lat)
    return out.reshape(bsz, n_lat, d)
```

```python
import functools
import math

import jax
import jax.numpy as jnp
from jax import lax
from jax.experimental import pallas as pl
from jax.experimental.pallas import tpu as pltpu

GRID_W = 64
S5_GROUP_CH = 16
S5_STATE = 64
HEAD_DIM = 128
N_KV_HEADS = 2
ROPE_THETA = 10000.0
N_EXPERT_GROUPS = 4
EXPERTS_PER_GROUP = 8
NORM_EPS = 1e-6
DEPTH = 1

S5_CHUNK = 16
S5_GROUP_BLOCK = 8
EXPERT_TILE = 256
DMA_UNROLL = 8
ATTN_Q_TILE = 512
ATTN_KV_CHUNK = 3072
MERGE_SUB_ROWS = 256
LANE = 128
VMEM_LIMIT = 56 * 1024 * 1024

F32 = jnp.float32
BF16 = jnp.bfloat16


def _layer_norm(x):
    mu = jnp.mean(x, axis=-1, keepdims=True)
    xc = x - mu
    var = jnp.mean(xc * xc, axis=-1, keepdims=True)
    return xc * lax.rsqrt(var + NORM_EPS)


def _split_bf16(a):
    hi = a.astype(BF16)
    lo = (a - hi.astype(F32)).astype(BF16)
    return hi, lo


def _dot(a, b):
    return jnp.dot(a, b, preferred_element_type=F32)


def _store_row_tiles(ref, lead, r0, val):
    n, d = val.shape
    rt = d // LANE
    for s in range(rt):
        ref[lead + (pl.ds(r0 * rt + s, n, stride=rt), slice(None))] = val[:, s * LANE:(s + 1) * LANE]


def _load_row_tiles(ref, lead, r0, n, rt):
    return jnp.concatenate([ref[lead + (pl.ds(r0 * rt + s, n, stride=rt), slice(None))] for s in range(rt)],
                           axis=1)


def _dot3(a_hi, a_lo, b_hi, b_lo):
    return _dot(a_hi, b_hi) + _dot(a_hi, b_lo) + _dot(a_lo, b_hi)


def _adaln_kernel(c_ref, w_ref, b_ref, o_ref):
    c = c_ref[...]
    s = c * jax.nn.sigmoid(c)
    s_hi, s_lo = _split_bf16(s)
    w_hi, w_lo = _split_bf16(w_ref[...])
    o_ref[...] = _dot3(s_hi, s_lo, w_hi, w_lo) + b_ref[...]


def _adaln(cond, w, b):
    rows, d = cond.shape
    n = w.shape[1]
    tn = min(n, 1024)
    return pl.pallas_call(
        _adaln_kernel,
        grid=(n // tn,),
        in_specs=[pl.BlockSpec((rows, d), lambda j: (0, 0)),
                  pl.BlockSpec((d, tn), lambda j: (0, j)),
                  pl.BlockSpec((1, tn), lambda j: (0, j))],
        out_specs=pl.BlockSpec((rows, tn), lambda j: (0, j)),
        out_shape=jax.ShapeDtypeStruct((rows, n), F32),
        compiler_params=pltpu.CompilerParams(vmem_limit_bytes=VMEM_LIMIT),
        name="adaln",
    )(cond, w, b.reshape(1, n))


def _rms_rope(t, gain, cos, sin_signed, first_half, scale):
    r = lax.rsqrt(jnp.mean(t * t, axis=-1, keepdims=True) + NORM_EPS)
    tn = t * r * gain
    if cos is not None:
        partner = jnp.where(first_half, pltpu.roll(tn, HEAD_DIM - 32, 1), pltpu.roll(tn, 32, 1))
        tn = tn * cos + partner * sin_signed
    if scale != 1.0:
        tn = tn * scale
    return tn


def _inproj_kernel(x_ref, mod_ref, w_ref, cos_ref, sin_ref, qg_ref, kg_ref, *out_refs,
                   d_s5, d_q, d_kv, latent):
    x = x_ref[0]
    shift = mod_ref[0, 0:1, :]
    scale = mod_ref[0, 1:2, :]
    h = (_layer_norm(x) * (1.0 + scale) + shift).astype(BF16)
    o_q = d_s5
    o_k = o_q + d_q
    o_v = o_k + d_kv
    o_g = o_v + d_kv
    if latent:
        u_ref, q_ref, k_ref, v_ref, g_ref = out_refs
        cos = cos_ref[...]
        sin = sin_ref[...]
        lane = lax.broadcasted_iota(jnp.int32, cos.shape, 1)
        first_half = (lane % 64) < 32
    else:
        u_ref, k_ref, v_ref = out_refs
        cos = sin = first_half = None
    k = _dot(h, w_ref[:, o_k:o_v])
    if latent:
        q = _dot(h, w_ref[:, o_q:o_k])
    for hd in range(d_kv // HEAD_DIM):
        sl = slice(hd * HEAD_DIM, (hd + 1) * HEAD_DIM)
        k_ref[0, :, sl] = _rms_rope(k[:, sl], kg_ref[...], cos, sin, first_half, 1.0).astype(BF16)
    u_ref[0] = _dot(h, w_ref[:, 0:d_s5])
    v_ref[0] = _dot(h, w_ref[:, o_v:o_g]).astype(BF16)
    if latent:
        d_half = (w_ref.shape[1] - o_g) // 2
        g_lo = _dot(h, w_ref[:, o_g:o_g + d_half])
        q_scale = HEAD_DIM ** -0.5 * math.log2(math.e)
        n_qh = d_q // HEAD_DIM
        for hd in range(n_qh // 2):
            sl = slice(hd * HEAD_DIM, (hd + 1) * HEAD_DIM)
            q_ref[0, :, sl] = _rms_rope(q[:, sl], qg_ref[...], cos, sin, first_half, q_scale).astype(BF16)
        g_hi = _dot(h, w_ref[:, o_g + d_half:])
        for hd in range(n_qh // 2, n_qh):
            sl = slice(hd * HEAD_DIM, (hd + 1) * HEAD_DIM)
            q_ref[0, :, sl] = _rms_rope(q[:, sl], qg_ref[...], cos, sin, first_half, q_scale).astype(BF16)
        g_ref[0, :, 0:d_half] = jax.nn.sigmoid(g_lo).astype(BF16)
        g_ref[0, :, d_half:] = jax.nn.sigmoid(g_hi).astype(BF16)


def _inproj(x, mod, w_in, cos, sin, q_gain, k_gain, *, d_s5, d_q, d_kv, latent):
    bsz, n, d = x.shape
    n_in = w_in.shape[1]
    d_gate = n_in - d_s5 - d_q - 2 * d_kv
    tm = min(n, 512)
    row = lambda w: pl.BlockSpec((1, tm, w), lambda b, i: (b, i, 0))
    out_shape = [jax.ShapeDtypeStruct((bsz, n, d_s5), F32)]
    out_specs = [row(d_s5)]
    if latent:
        out_shape.append(jax.ShapeDtypeStruct((bsz, n, d_q), BF16))
        out_specs.append(row(d_q))
    out_shape += [jax.ShapeDtypeStruct((bsz, n, d_kv), BF16)] * 2
    out_specs += [row(d_kv)] * 2
    if latent:
        out_shape.append(jax.ShapeDtypeStruct((bsz, n, d_gate), BF16))
        out_specs.append(row(d_gate))
    return pl.pallas_call(
        functools.partial(_inproj_kernel, d_s5=d_s5, d_q=d_q, d_kv=d_kv, latent=latent),
        grid=(bsz, n // tm),
        in_specs=[row(d),
                  pl.BlockSpec((1, 2, d), lambda b, i: (b, 0, 0)),
                  pl.BlockSpec((d, n_in), lambda b, i: (0, 0), pipeline_mode=pl.Buffered(1)),
                  pl.BlockSpec((tm, HEAD_DIM), lambda b, i: (i, 0)),
                  pl.BlockSpec((tm, HEAD_DIM), lambda b, i: (i, 0)),
                  pl.BlockSpec((1, HEAD_DIM), lambda b, i: (0, 0)),
                  pl.BlockSpec((1, HEAD_DIM), lambda b, i: (0, 0))],
        out_specs=out_specs,
        out_shape=out_shape,
        compiler_params=pltpu.CompilerParams(vmem_limit_bytes=VMEM_LIMIT),
        name="inproj_latent" if latent else "inproj_context",
    )(x, mod, w_in, cos, sin, q_gain, k_gain)


def _rope_tables(n_lat):
    rows = n_lat // GRID_W
    axis_dim = HEAD_DIM // 2
    inv = ROPE_THETA ** (-jnp.arange(0, axis_dim, 2, dtype=F32) / axis_dim)
    ang_r = jnp.arange(rows, dtype=F32)[:, None] * inv
    ang_c = jnp.arange(GRID_W, dtype=F32)[:, None] * inv
    per_row = lambda t: jnp.repeat(t, GRID_W, axis=0)
    per_col = lambda t: jnp.tile(t, (rows, 1))
    cos_r, sin_r = per_row(jnp.cos(ang_r)), per_row(jnp.sin(ang_r))
    cos_c, sin_c = per_col(jnp.cos(ang_c)), per_col(jnp.sin(ang_c))
    cos = jnp.concatenate([cos_r, cos_r, cos_c, cos_c], axis=1)
    sin = jnp.concatenate([-sin_r, sin_r, -sin_c, sin_c], axis=1)
    return cos, sin


def _s5_weights(a_re, a_im, log_dt, b_re, b_im, c_re, c_im):
    hp = lax.Precision.HIGHEST
    lc = S5_CHUNK
    dt = jnp.exp(log_dt)[..., None]
    lam_re, lam_im = a_re * dt, a_im * dt
    ea = jnp.exp(lam_re)
    ab_re, ab_im = ea * jnp.cos(lam_im), ea * jnp.sin(lam_im)
    den = a_re * a_re + a_im * a_im
    nr, ni = ab_re - 1.0, ab_im
    rr = (nr * a_re + ni * a_im) / den
    ri = (ni * a_re - nr * a_im) / den
    bb_re = rr[..., None] * b_re - ri[..., None] * b_im
    bb_im = rr[..., None] * b_im + ri[..., None] * b_re
    kk = jnp.arange(lc + 1, dtype=F32)[:, None, None, None]
    pk_mag = jnp.exp(kk * lam_re)
    pk_re, pk_im = pk_mag * jnp.cos(kk * lam_im), pk_mag * jnp.sin(kk * lam_im)
    n_g = a_re.shape[1]
    kw = lc * S5_GROUP_CH
    rows = lambda t: jnp.moveaxis(t, 0, 1).reshape(n_g, kw, t.shape[-1])
    bt_re, bt_im = jnp.swapaxes(bb_re, -1, -2), jnp.swapaxes(bb_im, -1, -2)
    pw_re, pw_im = pk_re[:lc, :, :, None, :], pk_im[:lc, :, :, None, :]
    akb_re = pw_re * bt_re - pw_im * bt_im
    akb_im = pw_re * bt_im + pw_im * bt_re
    kern = (jnp.einsum('dgop,kdgip->dgkoi', c_re, akb_re, precision=hp)
            - jnp.einsum('dgop,kdgip->dgkoi', c_im, akb_im, precision=hp))
    strip = jnp.concatenate([kern[1][:, :0:-1], kern[0][:, :1] + kern[1][:, :1], kern[0][:, 1:]], axis=1)
    strip = strip.reshape(n_g, (2 * lc - 1) * S5_GROUP_CH, S5_GROUP_CH)
    w_t = jnp.concatenate([strip[:, S5_GROUP_CH * (lc - 1 - s):S5_GROUP_CH * (lc - 1 - s) + kw]
                           for s in range(lc)], axis=2)
    w_t = jnp.swapaxes(w_t, 1, 2)
    w_z = jnp.concatenate([rows(akb_re[::-1, 0]), rows(akb_re[:, 1]),
                           rows(akb_im[::-1, 0]), rows(akb_im[:, 1])], axis=2)
    pf_re, pf_im = pk_re[1:, 0], pk_im[1:, 0]
    pr_re, pr_im = pk_re[lc:0:-1, 1], pk_im[lc:0:-1, 1]

    def cpow(cr, ci, pr, pi):
        return cr[None] * pr[:, :, None, :] - ci[None] * pi[:, :, None, :], \
               cr[None] * pi[:, :, None, :] + ci[None] * pr[:, :, None, :]

    cf_re, cf_im = cpow(c_re[0], c_im[0], pf_re, pf_im)
    cr_re, cr_im = cpow(c_re[1], c_im[1], pr_re, pr_im)
    w_c = jnp.concatenate([rows(cf_re), rows(cr_re), rows(-cf_im), rows(-cr_im)], axis=2)
    w_c = jnp.swapaxes(w_c, 1, 2)
    a_chunk = jnp.concatenate([pk_re[lc, 0], pk_re[lc, 1], pk_im[lc, 0], pk_im[lc, 1]], axis=1)
    return w_t.astype(BF16), w_z.astype(BF16), w_c.astype(BF16), a_chunk


def _regroup_rows(n_rows):
    return min(n_rows, 64)


def _s5_kernel(ul_ref, uc_ref, wt_ref, wz_ref, wc_ref, a_ref, d_ref, y_ref,
               lhs_l, lhs_c, zl_re, zl_im, zc_re, zc_im, yg_ref, *, nb, ncl, ncc):
    gb = S5_GROUP_BLOCK
    lc, ch = S5_CHUNK, S5_GROUP_CH
    per_tile = LANE // ch
    nl = nb * ncl
    ncx = nb * ncc
    half = 2 * S5_STATE
    lane_blk = lambda rows: lax.broadcasted_iota(jnp.int32, (rows, LANE), 1) // ch

    def block_transpose(arrs, blk):
        n = len(arrs)
        rolled = []
        for k in range(n):
            w = arrs[k]
            for g in range(1, n):
                w = jnp.where(blk == g, arrs[(g + k) % n], w)
            rolled.append(w if k == 0 else pltpu.roll(w, ch * k, 1))
        outs = []
        for b in range(n):
            o = rolled[0]
            for k in range(1, n):
                o = jnp.where(blk == (b + k) % n, rolled[k], o)
            outs.append(o)
        return outs

    def gather_chunks(src_ref, dst_ref, n_chunks):
        rb = _regroup_rows(n_chunks)
        blk = lane_blk(rb)

        def step(i, carry):
            r0 = pl.multiple_of(i * rb, rb)
            for hh in range(lc // per_tile):
                ut = [src_ref[pl.ds(r0 * lc + hh * per_tile + j, rb, stride=lc), :] for j in range(per_tile)]
                for g, out in enumerate(block_transpose(ut, blk)):
                    dst_ref[g, pl.ds(r0, rb), hh * LANE:(hh + 1) * LANE] = out.astype(BF16)
            return carry

        lax.fori_loop(0, n_chunks // rb, step, 0)

    gather_chunks(ul_ref, lhs_l, nl)
    gather_chunks(uc_ref, lhs_c, ncx)

    for g in range(gb):
        zl = _dot(lhs_l[g], wz_ref[g])
        zl_re[pl.ds(g, nl, stride=gb), :] = zl[:, :half]
        zl_im[pl.ds(g, nl, stride=gb), :] = zl[:, half:]
        zc = _dot(lhs_c[g], wz_ref[g])
        zc_re[pl.ds(g, ncx, stride=gb), :] = zc[:, :half]
        zc_im[pl.ds(g, ncx, stride=gb), :] = zc[:, half:]
    a_re = a_ref[:, :half]
    a_im = a_ref[:, half:]
    fwd = lax.broadcasted_iota(jnp.int32, (gb, half), 1) < S5_STATE

    def advance(h_re, h_im, z_re, z_im):
        return a_re * h_re - a_im * h_im + z_re, a_re * h_im + a_im * h_re + z_im

    def ctx_step(i, carry):
        out = []
        for b in range(nb):
            sl_f = pl.ds(pl.multiple_of((b * ncc + i) * gb, gb), gb)
            sl_r = pl.ds(pl.multiple_of((b * ncc + ncc - 1 - i) * gb, gb), gb)
            z_re = jnp.where(fwd, zc_re[sl_f, :], zc_re[sl_r, :])
            z_im = jnp.where(fwd, zc_im[sl_f, :], zc_im[sl_r, :])
            out.extend(advance(carry[2 * b], carry[2 * b + 1], z_re, z_im))
        return tuple(out)

    def lat_step(i, carry):
        out = []
        for b in range(nb):
            h_re, h_im = carry[2 * b], carry[2 * b + 1]
            sl_f = pl.ds(pl.multiple_of((b * ncl + i) * gb, gb), gb)
            sl_r = pl.ds(pl.multiple_of((b * ncl + ncl - 1 - i) * gb, gb), gb)
            f_re, f_im, r_re, r_im = zl_re[sl_f, :], zl_im[sl_f, :], zl_re[sl_r, :], zl_im[sl_r, :]
            zl_re[sl_f, :] = jnp.where(fwd, h_re, f_re)
            zl_im[sl_f, :] = jnp.where(fwd, h_im, f_im)
            zl_re[sl_r, :] = jnp.where(fwd, r_re, h_re)
            zl_im[sl_r, :] = jnp.where(fwd, r_im, h_im)
            out.extend(advance(h_re, h_im, jnp.where(fwd, f_re, r_re), jnp.where(fwd, f_im, r_im)))
        return tuple(out)

    zero = jnp.zeros((gb, half), F32)
    carry = lax.fori_loop(0, ncc, ctx_step, (zero,) * (2 * nb))
    lax.fori_loop(0, ncl, lat_step, carry)
    for g in range(gb):
        rows = pl.ds(g, nl, stride=gb)
        h_in = jnp.concatenate([zl_re[rows, :], zl_im[rows, :]], axis=1).astype(BF16)
        yg_ref[g] = _dot(lhs_l[g], wt_ref[g]) + _dot(h_in, wc_ref[g])

    rb = _regroup_rows(nl)
    blk = lane_blk(rb)
    d_row = d_ref[0]

    def scatter_step(i, carry):
        r0 = pl.multiple_of(i * rb, rb)
        for hh in range(lc // per_tile):
            yt = [yg_ref[g, pl.ds(r0, rb), hh * LANE:(hh + 1) * LANE] for g in range(gb)]
            for j, out in enumerate(block_transpose(yt, blk)):
                rows = pl.ds(r0 * lc + hh * per_tile + j, rb, stride=lc)
                y_ref[rows, :] = out + ul_ref[rows, :] * d_row
        return carry

    lax.fori_loop(0, nl // rb, scatter_step, 0)


def _s5_branch(u, uc, w_t, w_z, w_c, a_chunk, s5_d):
    bsz, n, width = u.shape
    n_ctx = uc.shape[1]
    lc, ch = S5_CHUNK, S5_GROUP_CH
    n_g = width // ch
    ncl, ncc = n // lc, n_ctx // lc
    gb = S5_GROUP_BLOCK
    assert ncl % 2 == 0 and n_g % gb == 0 and gb * ch == LANE
    kw = lc * ch
    nl, ncx = bsz * ncl, bsz * ncc
    assert nl % _regroup_rows(nl) == 0 and ncx % _regroup_rows(ncx) == 0
    once = pl.Buffered(1)
    slab = lambda rows: pl.BlockSpec((rows, LANE), lambda i: (0, i), pipeline_mode=once)
    blk3 = lambda r, c: pl.BlockSpec((gb, r, c), lambda i: (i, 0, 0))
    half = 2 * S5_STATE
    y = pl.pallas_call(
        functools.partial(_s5_kernel, nb=bsz, ncl=ncl, ncc=ncc),
        grid=(n_g // gb,),
        in_specs=[slab(bsz * n), slab(bsz * n_ctx), blk3(kw, kw), blk3(kw, 2 * half), blk3(2 * half, kw),
                  pl.BlockSpec((gb, 2 * half), lambda i: (i, 0)),
                  pl.BlockSpec((1, 1, LANE), lambda i: (i, 0, 0))],
        out_specs=slab(bsz * n),
        out_shape=jax.ShapeDtypeStruct((bsz * n, width), F32),
        scratch_shapes=[pltpu.VMEM((gb, nl, kw), BF16), pltpu.VMEM((gb, ncx, kw), BF16),
                        pltpu.VMEM((gb * nl, half), F32), pltpu.VMEM((gb * nl, half), F32),
                        pltpu.VMEM((gb * ncx, half), F32), pltpu.VMEM((gb * ncx, half), F32),
                        pltpu.VMEM((gb, nl, kw), F32)],
        compiler_params=pltpu.CompilerParams(vmem_limit_bytes=VMEM_LIMIT),
        name="s5_chunked_scan",
    )(u.reshape(bsz * n, width), uc.reshape(bsz * n_ctx, width), w_t, w_z, w_c, a_chunk,
      s5_d.reshape(n_g // gb, 1, LANE))
    return y.reshape(bsz, n, width)


def _attn_kernel(q_ref, k_ref, v_ref, kc_ref, vc_ref, o_ref, k_all, v_all, *, tk, group):
    tq = q_ref.shape[1]
    n_lat, n_ctx = k_ref.shape[1], kc_ref.shape[1]
    n_k = n_lat + n_ctx

    @pl.when(pl.program_id(2) == 0)
    def _():
        k_all[0:n_lat, :] = k_ref[0]
        k_all[n_lat:n_k, :] = kc_ref[0]
        v_all[0:n_lat, 0:HEAD_DIM] = v_ref[0]
        v_all[n_lat:n_k, 0:HEAD_DIM] = vc_ref[0]
        v_all[:, HEAD_DIM:] = jnp.ones((n_k, HEAD_DIM), BF16)

    qs = [q_ref[0, :, h * HEAD_DIM:(h + 1) * HEAD_DIM] for h in range(group)]

    def body(c, carry):
        start = pl.multiple_of(c * tk, tk)
        ks = k_all[pl.ds(start, tk), :]
        vs = v_all[pl.ds(start, tk), :]
        out = []
        score = lambda h: lax.dot_general(qs[h], ks, (((1,), (1,)), ((), ())), preferred_element_type=F32)
        s_next = score(0)
        for h in range(group):
            m, acc = carry[2 * h:2 * h + 2]
            s = s_next
            if h + 1 < group:
                s_next = score(h + 1)
            m_new = jnp.maximum(m, jnp.max(s, axis=-1, keepdims=True))
            p = jnp.exp2(s - m_new)
            alpha = jnp.exp2(m - m_new)
            acc = alpha * acc + _dot(p.astype(BF16), vs)
            out.extend((m_new, acc))
        return tuple(out)

    init = (jnp.full((tq, 1), -jnp.inf, F32), jnp.zeros((tq, 2 * HEAD_DIM), F32)) * group
    fin = lax.fori_loop(0, n_k // tk, body, init)
    for h in range(group):
        acc = fin[2 * h + 1]
        o_ref[0, :, h * HEAD_DIM:(h + 1) * HEAD_DIM] = (acc[:, :HEAD_DIM] / acc[:, HEAD_DIM:]).astype(o_ref.dtype)


def _pick_divisor(n, pref):
    best = LANE
    for t in range(LANE, pref + 1, LANE):
        if n % t == 0:
            best = t
    return best


def _attention(q, k, v, kc, vc):
    bsz, n, dq = q.shape
    n_c, dkv = kc.shape[1], k.shape[2]
    n_kv = dkv // HEAD_DIM
    group = dq // dkv
    tq = min(n, ATTN_Q_TILE)
    tk = _pick_divisor(n + n_c, ATTN_KV_CHUNK)
    kv_spec = lambda rows: pl.BlockSpec((1, rows, HEAD_DIM), lambda b, h, i: (b, 0, h))
    return pl.pallas_call(
        functools.partial(_attn_kernel, tk=tk, group=group),
        grid=(bsz, n_kv, n // tq),
        in_specs=[pl.BlockSpec((1, tq, group * HEAD_DIM), lambda b, h, i: (b, i, h)),
                  kv_spec(n), kv_spec(n), kv_spec(n_c), kv_spec(n_c)],
        out_specs=pl.BlockSpec((1, tq, group * HEAD_DIM), lambda b, h, i: (b, i, h)),
        out_shape=jax.ShapeDtypeStruct((bsz, n, dq), BF16),
        scratch_shapes=[pltpu.VMEM((n + n_c, HEAD_DIM), BF16), pltpu.VMEM((n + n_c, 2 * HEAD_DIM), BF16)],
        compiler_params=pltpu.CompilerParams(dimension_semantics=("arbitrary",) * 3, vmem_limit_bytes=VMEM_LIMIT),
        name="gqa_flash_attention",
    )(q, k, v, kc, vc)


def _merge_kernel(x_ref, y_ref, o_ref, g_ref, mod_ref, wa_ref, wb_ref, wo_ref, wout_ref,
                  ln_ref, wrh_ref, wrl_ref, br_ref, x1_ref, h2_ref, lg_ref, *, alpha):
    d = x_ref.shape[2]
    tm = x_ref.shape[1]
    g1 = mod_ref[0, 0:1, :]
    blocks = [slice(r, r + MERGE_SUB_ROWS) for r in range(0, tm, MERGE_SUB_ROWS)]
    stage1 = []
    for rows in blocks:
        att = _dot(o_ref[0, rows, :], wo_ref[...])
        gact = jax.nn.gelu(y_ref[0, rows, :]).astype(BF16)
        stage1.append((att, _dot(gact, wa_ref[...]), _dot(gact, wb_ref[...])))
    stage2 = []
    for rows, (att, a, b) in zip(blocks, stage1):
        gate = g_ref[0, rows, :].astype(F32)
        mixed = (gate[:, :d] * (a * jax.nn.sigmoid(b)) + gate[:, d:] * att).astype(BF16)
        stage2.append(_dot(mixed, wout_ref[...]))
    for rows, mix in zip(blocks, stage2):
        x1 = _layer_norm(alpha * x_ref[0, rows, :] + g1 * mix) * ln_ref[0:1, :] + ln_ref[1:2, :]
        x1_ref[0, rows, :] = x1
        h2 = _layer_norm(x1) * (1.0 + mod_ref[0, 2:3, :]) + mod_ref[0, 1:2, :]
        _store_row_tiles(h2_ref, (0,), rows.start, h2)
        h_hi, h_lo = _split_bf16(h2)
        lg_ref[0, rows, :] = _dot3(h_hi, h_lo, wrh_ref[...], wrl_ref[...]) + br_ref[...]


def _merge(x, y_ssm, o, gates, mod, w_glu_a, w_glu_b, w_attn_o, w_out, ln, wr_hi, wr_lo, br, alpha):
    bsz, n, d = x.shape
    tm = min(n, 512)
    row = lambda w: pl.BlockSpec((1, tm, w), lambda b, i: (b, i, 0))
    const = lambda a: pl.BlockSpec(a.shape, lambda b, i: (0,) * a.ndim)
    nr = wr_hi.shape[1]
    rt = d // LANE
    return pl.pallas_call(
        functools.partial(_merge_kernel, alpha=alpha),
        grid=(bsz, n // tm),
        in_specs=[row(d), row(y_ssm.shape[2]), row(o.shape[2]), row(gates.shape[2]),
                  pl.BlockSpec((1, 3, d), lambda b, i: (b, 0, 0)),
                  const(w_glu_a), const(w_glu_b), const(w_attn_o), const(w_out), const(ln),
                  const(wr_hi), const(wr_lo), const(br)],
        out_specs=[row(d), pl.BlockSpec((1, tm * rt, LANE), lambda b, i: (b, i, 0)), row(nr)],
        out_shape=[jax.ShapeDtypeStruct((bsz, n, d), F32), jax.ShapeDtypeStruct((bsz, n * rt, LANE), F32),
                   jax.ShapeDtypeStruct((bsz, n, nr), F32)],
        compiler_params=pltpu.CompilerParams(vmem_limit_bytes=VMEM_LIMIT),
        name="merge_postnorm_router",
    )(x, y_ssm, o, gates, mod, w_glu_a, w_glu_b, w_attn_o, w_out, ln, wr_hi, wr_lo, br)


def _route_kernel(lt_ref, e_ref, w_ref, r_ref, cnt_ref, run_ref):
    i = pl.program_id(0)
    tn = lt_ref.shape[1]
    n_experts = run_ref.shape[0]

    @pl.when(i == 0)
    def _():
        run_ref[...] = jnp.zeros_like(run_ref)

    gl = lt_ref[0:N_EXPERT_GROUPS, :]
    gmax = jnp.max(gl, axis=0, keepdims=True)
    gi = lax.broadcasted_iota(jnp.int32, gl.shape, 0)
    gidx = jnp.min(jnp.where(gl == gmax, gi, N_EXPERT_GROUPS), axis=0, keepdims=True)
    gw = 1.0 / jnp.sum(jnp.exp(gl - gmax), axis=0, keepdims=True)
    epg = EXPERTS_PER_GROUP
    e_in = lt_ref[8:8 + epg, :]
    for g in range(1, N_EXPERT_GROUPS):
        e_in = jnp.where(gidx == g, lt_ref[8 + g * epg:8 + (g + 1) * epg, :], e_in)
    ei = lax.broadcasted_iota(jnp.int32, e_in.shape, 0)
    v0 = jnp.max(e_in, axis=0, keepdims=True)
    i0 = jnp.min(jnp.where(e_in == v0, ei, epg), axis=0, keepdims=True)
    rest = jnp.where(ei == i0, -jnp.inf, e_in)
    v1 = jnp.max(rest, axis=0, keepdims=True)
    i1 = jnp.min(jnp.where(rest == v1, ei, epg), axis=0, keepdims=True)
    t = jnp.exp(v1 - v0)
    w0 = gw / (1.0 + t)
    w1 = gw * t / (1.0 + t)
    e0 = gidx * epg + i0
    e1 = gidx * epg + i1
    zi = jnp.zeros_like(e0)
    e_ref[...] = jnp.concatenate([e0, e1] + [zi] * 6, axis=0)
    w_ref[...] = jnp.concatenate([w0, w1] + [jnp.zeros_like(w0)] * 6, axis=0)

    both = jnp.concatenate([e0, e1], axis=1)
    hit = lax.broadcasted_iota(jnp.int32, (n_experts, 2 * tn), 0) == both
    tri = (lax.broadcasted_iota(jnp.int32, (2 * tn, 2 * tn), 0)
           <= lax.broadcasted_iota(jnp.int32, (2 * tn, 2 * tn), 1))
    pref = _dot(jnp.where(hit, 1.0, 0.0).astype(BF16), jnp.where(tri, 1.0, 0.0).astype(BF16))
    run = run_ref[:, 0:1]
    rank = jnp.sum(jnp.where(hit, pref + run, 0.0), axis=0, keepdims=True) - 1.0
    rank = rank.astype(jnp.int32)
    r_ref[...] = jnp.concatenate([rank[:, :tn], rank[:, tn:]] + [zi] * 6, axis=0)
    run_new = jnp.broadcast_to(run + pref[:, 2 * tn - 1:2 * tn], run_ref.shape)
    run_ref[...] = run_new
    cnt_ref[...] = run_new.astype(jnp.int32)


def _route(logits_t, n_experts):
    rows, n = logits_t.shape
    tn = min(n, 256)
    tile = pl.BlockSpec((8, tn), lambda i: (0, i))
    return pl.pallas_call(
        _route_kernel,
        grid=(n // tn,),
        in_specs=[pl.BlockSpec((rows, tn), lambda i: (0, i))],
        out_specs=[tile, tile, tile, pl.BlockSpec((n_experts, LANE), lambda i: (0, 0))],
        out_shape=[jax.ShapeDtypeStruct((8, n), jnp.int32), jax.ShapeDtypeStruct((8, n), F32),
                   jax.ShapeDtypeStruct((8, n), jnp.int32), jax.ShapeDtypeStruct((n_experts, LANE), jnp.int32)],
        scratch_shapes=[pltpu.VMEM((n_experts, LANE), F32)],
        compiler_params=pltpu.CompilerParams(dimension_semantics=("arbitrary",)),
        name="route_top2_rank",
    )(logits_t)


def _dispatch_kernel(pos_ref, starts_ref, ends_ref, h_ref, x_hbm, zbuf, sem, zsem,
                     *, n_tok, n_experts, tm_e, rt):
    i = pl.program_id(0)
    tm = h_ref.shape[0] // rt
    base = i * tm
    span = tm_e * rt

    @pl.when(i == 0)
    def _():
        zbuf[...] = jnp.zeros_like(zbuf)

        def tail_copy(e):
            start = pl.multiple_of((ends_ref[e] - tm_e) * rt, span)
            return pltpu.make_async_copy(zbuf, x_hbm.at[pl.ds(start, span), :], zsem)

        def fill(e, c):
            @pl.when(ends_ref[e] > starts_ref[e])
            def _():
                tail_copy(e).start()
            return c

        def fill_wait(e, c):
            @pl.when(ends_ref[e] > starts_ref[e])
            def _():
                tail_copy(e).wait()
            return c

        def free_copy(t):
            return pltpu.make_async_copy(zbuf, x_hbm.at[pl.ds(pl.multiple_of(t * span, span), span), :], zsem)

        def free_fill(t, c):
            free_copy(t).start()
            return c

        def free_wait(t, c):
            free_copy(t).wait()
            return c

        first_free = ends_ref[n_experts - 1] // tm_e
        n_tiles = x_hbm.shape[0] // span
        lax.fori_loop(0, n_experts, fill, 0)
        lax.fori_loop(first_free, n_tiles, free_fill, 0)
        lax.fori_loop(0, n_experts, fill_wait, 0)
        lax.fori_loop(first_free, n_tiles, free_wait, 0)

    def row_copy(r, slot):
        p = pl.multiple_of(pos_ref[slot * n_tok + base + r], rt)
        src = h_ref.at[pl.ds(pl.multiple_of(r * rt, rt), rt), :]
        return pltpu.make_async_copy(src, x_hbm.at[pl.ds(p, rt), :], sem)

    def issue(r, c):
        row_copy(r, 0).start()
        row_copy(r, 1).start()
        return c

    def drain(r, c):
        row_copy(r, 0).wait()
        row_copy(r, 1).wait()
        return c

    lax.fori_loop(0, tm, issue, 0, unroll=DMA_UNROLL)
    lax.fori_loop(0, tm, drain, 0, unroll=DMA_UNROLL)


def _dispatch(h2, pos, starts, ends, n_rows, tm_e, rt):
    n_tok = h2.shape[0] // rt
    tm = min(n_tok, 512)
    grid_spec = pltpu.PrefetchScalarGridSpec(
        num_scalar_prefetch=3,
        grid=(n_tok // tm,),
        in_specs=[pl.BlockSpec((tm * rt, LANE), lambda i, *_: (i, 0))],
        out_specs=pl.BlockSpec(memory_space=pl.ANY),
        scratch_shapes=[pltpu.VMEM((tm_e * rt, LANE), F32), pltpu.SemaphoreType.DMA(()),
                        pltpu.SemaphoreType.DMA(())],
    )
    return pl.pallas_call(
        functools.partial(_dispatch_kernel, n_tok=n_tok, n_experts=starts.shape[0], tm_e=tm_e, rt=rt),
        grid_spec=grid_spec,
        out_shape=jax.ShapeDtypeStruct((n_rows * rt, LANE), F32),
        compiler_params=pltpu.CompilerParams(dimension_semantics=("arbitrary",), vmem_limit_bytes=VMEM_LIMIT),
        name="dispatch_rows",
    )(pos, starts, ends, h2)


def _expert_kernel(tile_e_ref, used_ref, x_ref, wg_ref, wu_ref, wd_ref, y_ref, wg_b, wu_b, wd_b):
    i = pl.program_id(0)
    used = i < used_ref[0]
    fresh = jnp.logical_or(i == 0, tile_e_ref[i] != tile_e_ref[jnp.maximum(i - 1, 0)])

    @pl.when(jnp.logical_and(used, fresh))
    def _():
        wg_b[...] = wg_ref[0].astype(BF16)
        wu_b[...] = wu_ref[0].astype(BF16)
        wd_b[...] = wd_ref[0].astype(BF16)

    @pl.when(used)
    def _():
        rt = wg_b.shape[0] // LANE
        tm = x_ref.shape[0] // rt
        xb = _load_row_tiles(x_ref, (), 0, tm, rt).astype(BF16)
        a = _dot(xb, wg_b[...])
        b = _dot(xb, wu_b[...])
        _store_row_tiles(y_ref, (), 0, _dot((a * jax.nn.sigmoid(a) * b).astype(BF16), wd_b[...]))

    @pl.when(jnp.logical_not(used))
    def _():
        y_ref[...] = jnp.zeros_like(y_ref)


def _experts(x_sorted, tile_expert, n_used, w_gate, w_up, w_down, tm):
    d, d_e = w_gate.shape[1], w_gate.shape[2]
    rt = d // LANE
    n_rows = x_sorted.shape[0] // rt
    last = lambda i, used: jnp.minimum(i, used[0] - 1)
    grid_spec = pltpu.PrefetchScalarGridSpec(
        num_scalar_prefetch=2,
        grid=(n_rows // tm,),
        in_specs=[pl.BlockSpec((tm * rt, LANE), lambda i, te, used: (last(i, used), 0)),
                  pl.BlockSpec((1, d, d_e), lambda i, te, used: (te[i], 0, 0)),
                  pl.BlockSpec((1, d, d_e), lambda i, te, used: (te[i], 0, 0)),
                  pl.BlockSpec((1, d_e, d), lambda i, te, used: (te[i], 0, 0))],
        out_specs=pl.BlockSpec((tm * rt, LANE), lambda i, te, used: (i, 0)),
        scratch_shapes=[pltpu.VMEM((d, d_e), BF16), pltpu.VMEM((d, d_e), BF16), pltpu.VMEM((d_e, d), BF16)],
    )
    return pl.pallas_call(
        _expert_kernel,
        grid_spec=grid_spec,
        out_shape=jax.ShapeDtypeStruct((n_rows * rt, LANE), F32),
        compiler_params=pltpu.CompilerParams(dimension_semantics=("arbitrary",), vmem_limit_bytes=VMEM_LIMIT),
        name="routed_experts",
    )(tile_expert, n_used, x_sorted, w_gate, w_up, w_down)


def _final_kernel(pos_ref, x1_ref, w_ref, mod_ref, ln_ref, y_hbm, o_ref, ybuf, sems, *, alpha, n_tok):
    i = pl.program_id(0)
    n_steps = pl.num_programs(0)
    tm, d = x1_ref.shape
    rt = d // LANE

    def row_copy(step, r, slot):
        buf = step % 2
        p = pl.multiple_of(pos_ref[slot * n_tok + step * tm + r], rt)
        dst = ybuf.at[2 * buf + slot, pl.ds(pl.multiple_of(r * rt, rt), rt), :]
        return pltpu.make_async_copy(y_hbm.at[pl.ds(p, rt), :], dst, sems.at[buf])

    def issue_tile(step):
        def issue(r, c):
            row_copy(step, r, 0).start()
            row_copy(step, r, 1).start()
            return c
        lax.fori_loop(0, tm, issue, 0, unroll=DMA_UNROLL)

    @pl.when(i == 0)
    def _():
        issue_tile(i)

    @pl.when(i + 1 < n_steps)
    def _():
        issue_tile(i + 1)

    def drain(r, c):
        row_copy(i, r, 0).wait()
        row_copy(i, r, 1).wait()
        return c

    lax.fori_loop(0, tm, drain, 0, unroll=DMA_UNROLL)
    cur = 2 * (i % 2)
    moe = (w_ref[:, 0:1] * _load_row_tiles(ybuf, (cur,), 0, tm, rt)
           + w_ref[:, 1:2] * _load_row_tiles(ybuf, (cur + 1,), 0, tm, rt))
    z = alpha * x1_ref[...] + mod_ref[0] * moe
    o_ref[...] = _layer_norm(z) * ln_ref[0:1, :] + ln_ref[1:2, :]


def _final(x1, pos, wts, y_sorted, g2, ln, alpha, seq):
    n_tok, d = x1.shape
    rt = d // LANE
    tm = min(seq, 256)
    per_b = seq // tm
    grid_spec = pltpu.PrefetchScalarGridSpec(
        num_scalar_prefetch=1,
        grid=(n_tok // tm,),
        in_specs=[pl.BlockSpec((tm, d), lambda i, *_: (i, 0)),
                  pl.BlockSpec((tm, 2), lambda i, *_: (i, 0)),
                  pl.BlockSpec((1, 1, d), lambda i, *_: (i // per_b, 0, 0)),
                  pl.BlockSpec((2, d), lambda i, *_: (0, 0)),
                  pl.BlockSpec(memory_space=pl.ANY)],
        out_specs=pl.BlockSpec((tm, d), lambda i, *_: (i, 0)),
        scratch_shapes=[pltpu.VMEM((4, tm * rt, LANE), F32), pltpu.SemaphoreType.DMA((2,))],
    )
    return pl.pallas_call(
        functools.partial(_final_kernel, alpha=alpha, n_tok=n_tok),
        grid_spec=grid_spec,
        out_shape=jax.ShapeDtypeStruct((n_tok, d), F32),
        compiler_params=pltpu.CompilerParams(dimension_semantics=("arbitrary",), vmem_limit_bytes=VMEM_LIMIT),
        name="combine_postnorm",
    )(pos, x1, wts, g2, ln, y_sorted)


def _tile_plan(counts, n_rows, tm):
    n_experts = counts.shape[0]
    padded = ((counts + tm - 1) // tm) * tm
    ends = jnp.cumsum(padded).astype(jnp.int32)
    starts = ends - padded
    tile_start = jnp.arange(n_rows // tm, dtype=jnp.int32) * tm
    tile_e = jnp.sum((tile_start[:, None] >= ends[None, :]).astype(jnp.int32), axis=1)
    n_used = ends[-1:] // tm
    tile_e = jnp.minimum(tile_e, jnp.max(jnp.where(counts > 0, jnp.arange(n_experts, dtype=jnp.int32), 0)))
    return starts, ends, tile_e, n_used


def kernel(x, c, ctx, c_ctx, w_mod, b_mod, w_in, s5_a_re, s5_a_im, s5_log_dt, s5_b_re, s5_b_im, s5_c_re, s5_c_im, s5_d, w_glu_a, w_glu_b, q_gain, k_gain, w_attn_o, w_out, ln1_g, ln1_b, w_router_group, b_router_group, w_router_expert, b_router_expert, w_exp_gate, w_exp_up, w_exp_down, ln2_g, ln2_b):
    bsz, n_lat, d = x.shape
    n_ctx = ctx.shape[1]
    assert w_mod.shape[0] == DEPTH == 1 and bsz + 1 <= 8
    alpha = (2.0 * DEPTH) ** 0.25
    d_s5 = s5_d.shape[1]
    d_q = w_attn_o.shape[1]
    d_kv = N_KV_HEADS * HEAD_DIM
    n_experts = w_exp_gate.shape[1]

    cond = jnp.zeros((8, d), F32).at[:bsz].set(c).at[bsz].set(c_ctx)
    mod = _adaln(cond, w_mod[0], b_mod[0]).reshape(8, 6, d)
    mod_lat = mod[:bsz]
    mod_ctx = jnp.broadcast_to(mod[bsz:bsz + 1], (bsz, 6, d))

    w_in_b = w_in[0].astype(BF16)
    cos, sin = _rope_tables(n_lat)
    qg, kg = q_gain[0].reshape(1, HEAD_DIM), k_gain[0].reshape(1, HEAD_DIM)
    dims = dict(d_s5=d_s5, d_q=d_q, d_kv=d_kv)
    u, q, k, v, gates = _inproj(x, mod_lat[:, 0:2], w_in_b, cos, sin, qg, kg, latent=True, **dims)
    uc, kc, vc = _inproj(ctx, mod_ctx[:, 0:2], w_in_b, cos[:n_ctx], sin[:n_ctx], qg, kg, latent=False, **dims)

    w_t, w_z, w_c, a_chunk = _s5_weights(s5_a_re[0], s5_a_im[0], s5_log_dt[0], s5_b_re[0], s5_b_im[0],
                                         s5_c_re[0], s5_c_im[0])
    y_ssm = _s5_branch(u, uc, w_t, w_z, w_c, a_chunk, s5_d[0])

    o = _attention(q, k, v, kc, vc)

    n_r = 8 + n_experts
    n_r_pad = ((n_r + LANE - 1) // LANE) * LANE
    w_r = jnp.zeros((d, n_r_pad), F32).at[:, :N_EXPERT_GROUPS].set(w_router_group[0])
    w_r = w_r.at[:, 8:n_r].set(w_router_expert[0])
    b_r = jnp.zeros((1, n_r_pad), F32).at[0, :N_EXPERT_GROUPS].set(b_router_group[0])
    b_r = b_r.at[0, 8:n_r].set(b_router_expert[0])
    wr_hi, wr_lo = _split_bf16(w_r)
    ln1 = jnp.stack([ln1_g[0], ln1_b[0]])
    x1, h2, logits = _merge(x, y_ssm, o, gates, mod_lat[:, 2:5], w_glu_a[0].astype(BF16),
                            w_glu_b[0].astype(BF16), w_attn_o[0].astype(BF16), w_out[0].astype(BF16),
                            ln1, wr_hi, wr_lo, b_r, alpha)

    n_tok = bsz * n_lat
    tm_e = EXPERT_TILE
    n_rows = 2 * n_tok + n_experts * tm_e
    logits_t = logits.reshape(n_tok, n_r_pad)[:, :n_r].T
    eid, wts, rank, counts = _route(logits_t, n_experts)
    starts, ends, tile_e, n_used = _tile_plan(counts[:, 0], n_rows, tm_e)
    own = eid[:2, :, None] == jnp.arange(n_experts, dtype=jnp.int32)
    rt = d // LANE
    pos = ((jnp.sum(jnp.where(own, starts, 0), axis=-1) + rank[:2]) * rt).reshape(-1)
    x_sorted = _dispatch(h2.reshape(n_tok * rt, LANE), pos, starts, ends, n_rows, tm_e, rt)
    y_sorted = _experts(x_sorted, tile_e, n_used, w_exp_gate[0], w_exp_up[0], w_exp_down[0], tm_e)
    ln2 = jnp.stack([ln2_g[0], ln2_b[0]])
    out = _final(x1.reshape(n_tok, d), pos, wts[:2].T, y_sorted, mod_lat[:, 5:6], ln2, alpha, n_lat)
    return out.reshape(bsz, n_lat, d)
```

```python
import functools
import math

import jax
import jax.numpy as jnp
from jax import lax
from jax.experimental import pallas as pl
from jax.experimental.pallas import tpu as pltpu

GRID_W = 64
S5_GROUP_CH = 16
S5_STATE = 64
HEAD_DIM = 128
N_KV_HEADS = 2
ROPE_THETA = 10000.0
N_EXPERT_GROUPS = 4
EXPERTS_PER_GROUP = 8
NORM_EPS = 1e-6
DEPTH = 1

S5_CHUNK = 16
S5_GROUP_BLOCK = 8
EXPERT_TILE = 256
DMA_UNROLL = 8
ATTN_Q_TILE = 512
ATTN_KV_CHUNK = 3072
ATTN_ONES_ROWS = 16
MERGE_SUB_ROWS = 256
LANE = 128
VMEM_LIMIT = 56 * 1024 * 1024

F32 = jnp.float32
BF16 = jnp.bfloat16


def _layer_norm(x):
    mu = jnp.mean(x, axis=-1, keepdims=True)
    xc = x - mu
    var = jnp.mean(xc * xc, axis=-1, keepdims=True)
    return xc * lax.rsqrt(var + NORM_EPS)


def _split_bf16(a):
    hi = a.astype(BF16)
    lo = (a - hi.astype(F32)).astype(BF16)
    return hi, lo


def _dot(a, b):
    return jnp.dot(a, b, preferred_element_type=F32)


def _store_row_tiles(ref, lead, r0, val):
    n, d = val.shape
    rt = d // LANE
    for s in range(rt):
        ref[lead + (pl.ds(r0 * rt + s, n, stride=rt), slice(None))] = val[:, s * LANE:(s + 1) * LANE]


def _load_row_tiles(ref, lead, r0, n, rt):
    return jnp.concatenate([ref[lead + (pl.ds(r0 * rt + s, n, stride=rt), slice(None))] for s in range(rt)],
                           axis=1)


def _dot3(a_hi, a_lo, b_hi, b_lo):
    return _dot(a_hi, b_hi) + _dot(a_hi, b_lo) + _dot(a_lo, b_hi)


def _adaln_kernel(c_ref, w_ref, b_ref, o_ref):
    c = c_ref[...]
    s = c * jax.nn.sigmoid(c)
    s_hi, s_lo = _split_bf16(s)
    w_hi, w_lo = _split_bf16(w_ref[...])
    o_ref[...] = _dot3(s_hi, s_lo, w_hi, w_lo) + b_ref[...]


def _adaln(cond, w, b):
    rows, d = cond.shape
    n = w.shape[1]
    tn = min(n, 1024)
    return pl.pallas_call(
        _adaln_kernel,
        grid=(n // tn,),
        in_specs=[pl.BlockSpec((rows, d), lambda j: (0, 0)),
                  pl.BlockSpec((d, tn), lambda j: (0, j)),
                  pl.BlockSpec((1, tn), lambda j: (0, j))],
        out_specs=pl.BlockSpec((rows, tn), lambda j: (0, j)),
        out_shape=jax.ShapeDtypeStruct((rows, n), F32),
        compiler_params=pltpu.CompilerParams(vmem_limit_bytes=VMEM_LIMIT),
        name="adaln",
    )(cond, w, b.reshape(1, n))


def _rms_rope(t, gain, cos, sin_signed, first_half, scale):
    r = lax.rsqrt(jnp.mean(t * t, axis=-1, keepdims=True) + NORM_EPS)
    tn = t * r * gain
    if cos is not None:
        partner = jnp.where(first_half, pltpu.roll(tn, HEAD_DIM - 32, 1), pltpu.roll(tn, 32, 1))
        tn = tn * cos + partner * sin_signed
    if scale != 1.0:
        tn = tn * scale
    return tn


def _inproj_kernel(x_ref, mod_ref, w_ref, wvt_ref, cos_ref, sin_ref, qg_ref, kg_ref, *out_refs,
                   d_s5, d_q, d_kv, latent):
    x = x_ref[0]
    shift = mod_ref[0, 0:1, :]
    scale = mod_ref[0, 1:2, :]
    h = (_layer_norm(x) * (1.0 + scale) + shift).astype(BF16)
    o_q = d_s5
    o_k = o_q + d_q
    o_v = o_k + d_kv
    o_g = o_v + d_kv
    if latent:
        u_ref, q_ref, k_ref, v_ref, g_ref = out_refs
        cos = cos_ref[...]
        sin = sin_ref[...]
        lane = lax.broadcasted_iota(jnp.int32, cos.shape, 1)
        first_half = (lane % 64) < 32
    else:
        u_ref, k_ref, v_ref = out_refs
        cos = sin = first_half = None
    k = _dot(h, w_ref[:, o_k:o_v])
    if latent:
        q = _dot(h, w_ref[:, o_q:o_k])
    for hd in range(d_kv // HEAD_DIM):
        sl = slice(hd * HEAD_DIM, (hd + 1) * HEAD_DIM)
        k_ref[0, :, sl] = _rms_rope(k[:, sl], kg_ref[...], cos, sin, first_half, 1.0).astype(BF16)
    u_ref[0] = _dot(h, w_ref[:, 0:d_s5])
    v_ref[0] = lax.dot_general(wvt_ref[...], h, (((1,), (1,)), ((), ())),
                               preferred_element_type=F32).astype(BF16)
    if latent:
        d_half = (w_ref.shape[1] - o_g) // 2
        g_lo = _dot(h, w_ref[:, o_g:o_g + d_half])
        q_scale = HEAD_DIM ** -0.5 * math.log2(math.e)
        n_qh = d_q // HEAD_DIM
        for hd in range(n_qh // 2):
            sl = slice(hd * HEAD_DIM, (hd + 1) * HEAD_DIM)
            q_ref[0, :, sl] = _rms_rope(q[:, sl], qg_ref[...], cos, sin, first_half, q_scale).astype(BF16)
        g_hi = _dot(h, w_ref[:, o_g + d_half:])
        for hd in range(n_qh // 2, n_qh):
            sl = slice(hd * HEAD_DIM, (hd + 1) * HEAD_DIM)
            q_ref[0, :, sl] = _rms_rope(q[:, sl], qg_ref[...], cos, sin, first_half, q_scale).astype(BF16)
        g_ref[0, :, 0:d_half] = jax.nn.sigmoid(g_lo).astype(BF16)
        g_ref[0, :, d_half:] = jax.nn.sigmoid(g_hi).astype(BF16)


def _inproj(x, mod, w_in, w_vt, cos, sin, q_gain, k_gain, *, d_s5, d_q, d_kv, latent):
    bsz, n, d = x.shape
    n_in = w_in.shape[1]
    d_gate = n_in - d_s5 - d_q - 2 * d_kv
    tm = min(n, 512)
    row = lambda w: pl.BlockSpec((1, tm, w), lambda b, i: (b, i, 0))
    out_shape = [jax.ShapeDtypeStruct((bsz, n, d_s5), F32)]
    out_specs = [row(d_s5)]
    if latent:
        out_shape.append(jax.ShapeDtypeStruct((bsz, n, d_q), BF16))
        out_specs.append(row(d_q))
    out_shape += [jax.ShapeDtypeStruct((bsz, n, d_kv), BF16), jax.ShapeDtypeStruct((bsz, d_kv, n), BF16)]
    out_specs += [row(d_kv), pl.BlockSpec((1, d_kv, tm), lambda b, i: (b, 0, i))]
    if latent:
        out_shape.append(jax.ShapeDtypeStruct((bsz, n, d_gate), BF16))
        out_specs.append(row(d_gate))
    return pl.pallas_call(
        functools.partial(_inproj_kernel, d_s5=d_s5, d_q=d_q, d_kv=d_kv, latent=latent),
        grid=(bsz, n // tm),
        in_specs=[row(d),
                  pl.BlockSpec((1, 2, d), lambda b, i: (b, 0, 0)),
                  pl.BlockSpec((d, n_in), lambda b, i: (0, 0), pipeline_mode=pl.Buffered(1)),
                  pl.BlockSpec((d_kv, d), lambda b, i: (0, 0), pipeline_mode=pl.Buffered(1)),
                  pl.BlockSpec((tm, HEAD_DIM), lambda b, i: (i, 0)),
                  pl.BlockSpec((tm, HEAD_DIM), lambda b, i: (i, 0)),
                  pl.BlockSpec((1, HEAD_DIM), lambda b, i: (0, 0)),
                  pl.BlockSpec((1, HEAD_DIM), lambda b, i: (0, 0))],
        out_specs=out_specs,
        out_shape=out_shape,
        compiler_params=pltpu.CompilerParams(vmem_limit_bytes=VMEM_LIMIT),
        name="inproj_latent" if latent else "inproj_context",
    )(x, mod, w_in, w_vt, cos, sin, q_gain, k_gain)


def _rope_tables(n_lat):
    rows = n_lat // GRID_W
    axis_dim = HEAD_DIM // 2
    inv = ROPE_THETA ** (-jnp.arange(0, axis_dim, 2, dtype=F32) / axis_dim)
    ang_r = jnp.arange(rows, dtype=F32)[:, None] * inv
    ang_c = jnp.arange(GRID_W, dtype=F32)[:, None] * inv
    per_row = lambda t: jnp.repeat(t, GRID_W, axis=0)
    per_col = lambda t: jnp.tile(t, (rows, 1))
    cos_r, sin_r = per_row(jnp.cos(ang_r)), per_row(jnp.sin(ang_r))
    cos_c, sin_c = per_col(jnp.cos(ang_c)), per_col(jnp.sin(ang_c))
    cos = jnp.concatenate([cos_r, cos_r, cos_c, cos_c], axis=1)
    sin = jnp.concatenate([-sin_r, sin_r, -sin_c, sin_c], axis=1)
    return cos, sin


def _s5_weights(a_re, a_im, log_dt, b_re, b_im, c_re, c_im):
    hp = lax.Precision.HIGHEST
    lc = S5_CHUNK
    dt = jnp.exp(log_dt)[..., None]
    lam_re, lam_im = a_re * dt, a_im * dt
    ea = jnp.exp(lam_re)
    ab_re, ab_im = ea * jnp.cos(lam_im), ea * jnp.sin(lam_im)
    den = a_re * a_re + a_im * a_im
    nr, ni = ab_re - 1.0, ab_im
    rr = (nr * a_re + ni * a_im) / den
    ri = (ni * a_re - nr * a_im) / den
    bb_re = rr[..., None] * b_re - ri[..., None] * b_im
    bb_im = rr[..., None] * b_im + ri[..., None] * b_re
    kk = jnp.arange(lc + 1, dtype=F32)[:, None, None, None]
    pk_mag = jnp.exp(kk * lam_re)
    pk_re, pk_im = pk_mag * jnp.cos(kk * lam_im), pk_mag * jnp.sin(kk * lam_im)
    n_g = a_re.shape[1]
    kw = lc * S5_GROUP_CH
    rows = lambda t: jnp.moveaxis(t, 0, 1).reshape(n_g, kw, t.shape[-1])
    bt_re, bt_im = jnp.swapaxes(bb_re, -1, -2), jnp.swapaxes(bb_im, -1, -2)
    pw_re, pw_im = pk_re[:lc, :, :, None, :], pk_im[:lc, :, :, None, :]
    akb_re = pw_re * bt_re - pw_im * bt_im
    akb_im = pw_re * bt_im + pw_im * bt_re
    kern = (jnp.einsum('dgop,kdgip->dgkoi', c_re, akb_re, precision=hp)
            - jnp.einsum('dgop,kdgip->dgkoi', c_im, akb_im, precision=hp))
    strip = jnp.concatenate([kern[1][:, :0:-1], kern[0][:, :1] + kern[1][:, :1], kern[0][:, 1:]], axis=1)
    strip = strip.reshape(n_g, (2 * lc - 1) * S5_GROUP_CH, S5_GROUP_CH)
    w_t = jnp.concatenate([strip[:, S5_GROUP_CH * (lc - 1 - s):S5_GROUP_CH * (lc - 1 - s) + kw]
                           for s in range(lc)], axis=2)
    w_t = jnp.swapaxes(w_t, 1, 2)
    w_z = jnp.concatenate([rows(akb_re[::-1, 0]), rows(akb_re[:, 1]),
                           rows(akb_im[::-1, 0]), rows(akb_im[:, 1])], axis=2)
    pf_re, pf_im = pk_re[1:, 0], pk_im[1:, 0]
    pr_re, pr_im = pk_re[lc:0:-1, 1], pk_im[lc:0:-1, 1]

    def cpow(cr, ci, pr, pi):
        return cr[None] * pr[:, :, None, :] - ci[None] * pi[:, :, None, :], \
               cr[None] * pi[:, :, None, :] + ci[None] * pr[:, :, None, :]

    cf_re, cf_im = cpow(c_re[0], c_im[0], pf_re, pf_im)
    cr_re, cr_im = cpow(c_re[1], c_im[1], pr_re, pr_im)
    w_c = jnp.concatenate([rows(cf_re), rows(cr_re), rows(-cf_im), rows(-cr_im)], axis=2)
    w_c = jnp.swapaxes(w_c, 1, 2)
    a_chunk = jnp.concatenate([pk_re[lc, 0], pk_re[lc, 1], pk_im[lc, 0], pk_im[lc, 1]], axis=1)
    return w_t.astype(BF16), w_z.astype(BF16), w_c.astype(BF16), a_chunk


def _regroup_rows(n_rows):
    return min(n_rows, 64)


def _s5_kernel(ul_ref, uc_ref, wt_ref, wz_ref, wc_ref, a_ref, d_ref, y_ref,
               lhs_l, lhs_c, zl_re, zl_im, zc_re, zc_im, yg_ref, *, nb, ncl, ncc):
    gb = S5_GROUP_BLOCK
    lc, ch = S5_CHUNK, S5_GROUP_CH
    per_tile = LANE // ch
    nl = nb * ncl
    ncx = nb * ncc
    half = 2 * S5_STATE
    lane_blk = lambda rows: lax.broadcasted_iota(jnp.int32, (rows, LANE), 1) // ch

    def block_transpose(arrs, blk):
        n = len(arrs)
        rolled = []
        for k in range(n):
            w = arrs[k]
            for g in range(1, n):
                w = jnp.where(blk == g, arrs[(g + k) % n], w)
            rolled.append(w if k == 0 else pltpu.roll(w, ch * k, 1))
        outs = []
        for b in range(n):
            o = rolled[0]
            for k in range(1, n):
                o = jnp.where(blk == (b + k) % n, rolled[k], o)
            outs.append(o)
        return outs

    def gather_chunks(src_ref, dst_ref, n_chunks):
        rb = _regroup_rows(n_chunks)
        blk = lane_blk(rb)

        def step(i, carry):
            r0 = pl.multiple_of(i * rb, rb)
            for hh in range(lc // per_tile):
                ut = [src_ref[pl.ds(r0 * lc + hh * per_tile + j, rb, stride=lc), :] for j in range(per_tile)]
                for g, out in enumerate(block_transpose(ut, blk)):
                    dst_ref[g, pl.ds(r0, rb), hh * LANE:(hh + 1) * LANE] = out.astype(BF16)
            return carry

        lax.fori_loop(0, n_chunks // rb, step, 0)

    gather_chunks(ul_ref, lhs_l, nl)
    gather_chunks(uc_ref, lhs_c, ncx)

    for g in range(gb):
        zl = _dot(lhs_l[g], wz_ref[g])
        zl_re[pl.ds(g, nl, stride=gb), :] = zl[:, :half]
        zl_im[pl.ds(g, nl, stride=gb), :] = zl[:, half:]
        zc = _dot(lhs_c[g], wz_ref[g])
        zc_re[pl.ds(g, ncx, stride=gb), :] = zc[:, :half]
        zc_im[pl.ds(g, ncx, stride=gb), :] = zc[:, half:]
    a_re = a_ref[:, :half]
    a_im = a_ref[:, half:]
    fwd = lax.broadcasted_iota(jnp.int32, (gb, half), 1) < S5_STATE

    def advance(h_re, h_im, z_re, z_im):
        return a_re * h_re - a_im * h_im + z_re, a_re * h_im + a_im * h_re + z_im

    def ctx_step(i, carry):
        out = []
        for b in range(nb):
            sl_f = pl.ds(pl.multiple_of((b * ncc + i) * gb, gb), gb)
            sl_r = pl.ds(pl.multiple_of((b * ncc + ncc - 1 - i) * gb, gb), gb)
            z_re = jnp.where(fwd, zc_re[sl_f, :], zc_re[sl_r, :])
            z_im = jnp.where(fwd, zc_im[sl_f, :], zc_im[sl_r, :])
            out.extend(advance(carry[2 * b], carry[2 * b + 1], z_re, z_im))
        return tuple(out)

    def lat_step(i, carry):
        out = []
        for b in range(nb):
            h_re, h_im = carry[2 * b], carry[2 * b + 1]
            sl_f = pl.ds(pl.multiple_of((b * ncl + i) * gb, gb), gb)
            sl_r = pl.ds(pl.multiple_of((b * ncl + ncl - 1 - i) * gb, gb), gb)
            f_re, f_im, r_re, r_im = zl_re[sl_f, :], zl_im[sl_f, :], zl_re[sl_r, :], zl_im[sl_r, :]
            zl_re[sl_f, :] = jnp.where(fwd, h_re, f_re)
            zl_im[sl_f, :] = jnp.where(fwd, h_im, f_im)
            zl_re[sl_r, :] = jnp.where(fwd, r_re, h_re)
            zl_im[sl_r, :] = jnp.where(fwd, r_im, h_im)
            out.extend(advance(h_re, h_im, jnp.where(fwd, f_re, r_re), jnp.where(fwd, f_im, r_im)))
        return tuple(out)

    zero = jnp.zeros((gb, half), F32)
    carry = lax.fori_loop(0, ncc, ctx_step, (zero,) * (2 * nb))
    lax.fori_loop(0, ncl, lat_step, carry)
    for g in range(gb):
        rows = pl.ds(g, nl, stride=gb)
        h_in = jnp.concatenate([zl_re[rows, :], zl_im[rows, :]], axis=1).astype(BF16)
        yg_ref[g] = _dot(lhs_l[g], wt_ref[g]) + _dot(h_in, wc_ref[g])

    rb = _regroup_rows(nl)
    blk = lane_blk(rb)
    d_row = d_ref[0]

    def scatter_step(i, carry):
        r0 = pl.multiple_of(i * rb, rb)
        for hh in range(lc // per_tile):
            yt = [yg_ref[g, pl.ds(r0, rb), hh * LANE:(hh + 1) * LANE] for g in range(gb)]
            for j, out in enumerate(block_transpose(yt, blk)):
                rows = pl.ds(r0 * lc + hh * per_tile + j, rb, stride=lc)
                y_ref[rows, :] = out + ul_ref[rows, :] * d_row
        return carry

    lax.fori_loop(0, nl // rb, scatter_step, 0)


def _s5_branch(u, uc, w_t, w_z, w_c, a_chunk, s5_d):
    bsz, n, width = u.shape
    n_ctx = uc.shape[1]
    lc, ch = S5_CHUNK, S5_GROUP_CH
    n_g = width // ch
    ncl, ncc = n // lc, n_ctx // lc
    gb = S5_GROUP_BLOCK
    assert ncl % 2 == 0 and n_g % gb == 0 and gb * ch == LANE
    kw = lc * ch
    nl, ncx = bsz * ncl, bsz * ncc
    assert nl % _regroup_rows(nl) == 0 and ncx % _regroup_rows(ncx) == 0
    once = pl.Buffered(1)
    slab = lambda rows: pl.BlockSpec((rows, LANE), lambda i: (0, i), pipeline_mode=once)
    blk3 = lambda r, c: pl.BlockSpec((gb, r, c), lambda i: (i, 0, 0))
    half = 2 * S5_STATE
    y = pl.pallas_call(
        functools.partial(_s5_kernel, nb=bsz, ncl=ncl, ncc=ncc),
        grid=(n_g // gb,),
        in_specs=[slab(bsz * n), slab(bsz * n_ctx), blk3(kw, kw), blk3(kw, 2 * half), blk3(2 * half, kw),
                  pl.BlockSpec((gb, 2 * half), lambda i: (i, 0)),
                  pl.BlockSpec((1, 1, LANE), lambda i: (i, 0, 0))],
        out_specs=slab(bsz * n),
        out_shape=jax.ShapeDtypeStruct((bsz * n, width), F32),
        scratch_shapes=[pltpu.VMEM((gb, nl, kw), BF16), pltpu.VMEM((gb, ncx, kw), BF16),
                        pltpu.VMEM((gb * nl, half), F32), pltpu.VMEM((gb * nl, half), F32),
                        pltpu.VMEM((gb * ncx, half), F32), pltpu.VMEM((gb * ncx, half), F32),
                        pltpu.VMEM((gb, nl, kw), F32)],
        compiler_params=pltpu.CompilerParams(vmem_limit_bytes=VMEM_LIMIT),
        name="s5_chunked_scan",
    )(u.reshape(bsz * n, width), uc.reshape(bsz * n_ctx, width), w_t, w_z, w_c, a_chunk,
      s5_d.reshape(n_g // gb, 1, LANE))
    return y.reshape(bsz, n, width)


def _attn_kernel(q_ref, k_ref, vt_ref, kc_ref, vct_ref, o_ref, k_all, vt_all, *, tk, group):
    tq = q_ref.shape[1]
    n_lat, n_ctx = k_ref.shape[1], kc_ref.shape[1]
    n_k = n_lat + n_ctx
    ext = vt_all.shape[0]

    @pl.when(pl.program_id(2) == 0)
    def _():
        k_all[0:n_lat, :] = k_ref[0]
        k_all[n_lat:n_k, :] = kc_ref[0]
        vt_all[0:HEAD_DIM, 0:n_lat] = vt_ref[0]
        vt_all[0:HEAD_DIM, n_lat:n_k] = vct_ref[0]
        vt_all[HEAD_DIM:, :] = jnp.ones((ext - HEAD_DIM, n_k), BF16)

    qs = [q_ref[0, :, h * HEAD_DIM:(h + 1) * HEAD_DIM] for h in range(group)]

    def body(c, carry):
        start = pl.multiple_of(c * tk, tk)
        ks = k_all[pl.ds(start, tk), :]
        vts = vt_all[:, pl.ds(start, tk)]
        out = []
        score = lambda h: lax.dot_general(ks, qs[h], (((1,), (1,)), ((), ())), preferred_element_type=F32)
        s_next = score(0)
        for h in range(group):
            m, acc = carry[2 * h:2 * h + 2]
            s = s_next
            if h + 1 < group:
                s_next = score(h + 1)
            m_new = jnp.maximum(m, jnp.max(s, axis=0, keepdims=True))
            p = jnp.exp2(s - m_new)
            alpha = jnp.exp2(m - m_new)
            acc = alpha * acc + _dot(vts, p.astype(BF16))
            out.extend((m_new, acc))
        return tuple(out)

    init = (jnp.full((1, tq), -jnp.inf, F32), jnp.zeros((ext, tq), F32)) * group
    fin = lax.fori_loop(0, n_k // tk, body, init)
    for h in range(group):
        acc = fin[2 * h + 1]
        out_t = acc[:HEAD_DIM] / acc[HEAD_DIM:HEAD_DIM + 1]
        o_ref[0, :, h * HEAD_DIM:(h + 1) * HEAD_DIM] = out_t.T.astype(o_ref.dtype)


def _pick_divisor(n, pref):
    best = LANE
    for t in range(LANE, pref + 1, LANE):
        if n % t == 0:
            best = t
    return best


def _attention(q, k, vt, kc, vct):
    bsz, n, dq = q.shape
    n_c, dkv = kc.shape[1], k.shape[2]
    n_kv = dkv // HEAD_DIM
    group = dq // dkv
    tq = min(n, ATTN_Q_TILE)
    tk = _pick_divisor(n + n_c, ATTN_KV_CHUNK)
    k_spec = lambda rows: pl.BlockSpec((1, rows, HEAD_DIM), lambda b, h, i: (b, 0, h))
    vt_spec = lambda cols: pl.BlockSpec((1, HEAD_DIM, cols), lambda b, h, i: (b, h, 0))
    return pl.pallas_call(
        functools.partial(_attn_kernel, tk=tk, group=group),
        grid=(bsz, n_kv, n // tq),
        in_specs=[pl.BlockSpec((1, tq, group * HEAD_DIM), lambda b, h, i: (b, i, h)),
                  k_spec(n), vt_spec(n), k_spec(n_c), vt_spec(n_c)],
        out_specs=pl.BlockSpec((1, tq, group * HEAD_DIM), lambda b, h, i: (b, i, h)),
        out_shape=jax.ShapeDtypeStruct((bsz, n, dq), BF16),
        scratch_shapes=[pltpu.VMEM((n + n_c, HEAD_DIM), BF16),
                        pltpu.VMEM((HEAD_DIM + ATTN_ONES_ROWS, n + n_c), BF16)],
        compiler_params=pltpu.CompilerParams(dimension_semantics=("arbitrary",) * 3, vmem_limit_bytes=VMEM_LIMIT),
        name="gqa_flash_attention",
    )(q, k, vt, kc, vct)


def _merge_kernel(x_ref, y_ref, o_ref, g_ref, mod_ref, wa_ref, wb_ref, wo_ref, wout_ref,
                  ln_ref, wrh_ref, wrl_ref, br_ref, x1_ref, h2_ref, lg_ref, *, alpha):
    d = x_ref.shape[2]
    tm = x_ref.shape[1]
    g1 = mod_ref[0, 0:1, :]
    blocks = [slice(r, r + MERGE_SUB_ROWS) for r in range(0, tm, MERGE_SUB_ROWS)]
    stage1 = []
    for rows in blocks:
        att = _dot(o_ref[0, rows, :], wo_ref[...])
        gact = jax.nn.gelu(y_ref[0, rows, :]).astype(BF16)
        stage1.append((att, _dot(gact, wa_ref[...]), _dot(gact, wb_ref[...])))
    stage2 = []
    for rows, (att, a, b) in zip(blocks, stage1):
        gate = g_ref[0, rows, :].astype(F32)
        mixed = (gate[:, :d] * (a * jax.nn.sigmoid(b)) + gate[:, d:] * att).astype(BF16)
        stage2.append(_dot(mixed, wout_ref[...]))
    for rows, mix in zip(blocks, stage2):
        x1 = _layer_norm(alpha * x_ref[0, rows, :] + g1 * mix) * ln_ref[0:1, :] + ln_ref[1:2, :]
        x1_ref[0, rows, :] = x1
        h2 = _layer_norm(x1) * (1.0 + mod_ref[0, 2:3, :]) + mod_ref[0, 1:2, :]
        _store_row_tiles(h2_ref, (0,), rows.start, h2)
        h_hi, h_lo = _split_bf16(h2)
        lg_ref[0, rows, :] = _dot3(h_hi, h_lo, wrh_ref[...], wrl_ref[...]) + br_ref[...]


def _merge(x, y_ssm, o, gates, mod, w_glu_a, w_glu_b, w_attn_o, w_out, ln, wr_hi, wr_lo, br, alpha):
    bsz, n, d = x.shape
    tm = min(n, 512)
    row = lambda w: pl.BlockSpec((1, tm, w), lambda b, i: (b, i, 0))
    const = lambda a: pl.BlockSpec(a.shape, lambda b, i: (0,) * a.ndim)
    nr = wr_hi.shape[1]
    rt = d // LANE
    return pl.pallas_call(
        functools.partial(_merge_kernel, alpha=alpha),
        grid=(bsz, n // tm),
        in_specs=[row(d), row(y_ssm.shape[2]), row(o.shape[2]), row(gates.shape[2]),
                  pl.BlockSpec((1, 3, d), lambda b, i: (b, 0, 0)),
                  const(w_glu_a), const(w_glu_b), const(w_attn_o), const(w_out), const(ln),
                  const(wr_hi), const(wr_lo), const(br)],
        out_specs=[row(d), pl.BlockSpec((1, tm * rt, LANE), lambda b, i: (b, i, 0)), row(nr)],
        out_shape=[jax.ShapeDtypeStruct((bsz, n, d), F32), jax.ShapeDtypeStruct((bsz, n * rt, LANE), F32),
                   jax.ShapeDtypeStruct((bsz, n, nr), F32)],
        compiler_params=pltpu.CompilerParams(vmem_limit_bytes=VMEM_LIMIT),
        name="merge_postnorm_router",
    )(x, y_ssm, o, gates, mod, w_glu_a, w_glu_b, w_attn_o, w_out, ln, wr_hi, wr_lo, br)


def _route_kernel(lt_ref, e_ref, w_ref, r_ref, cnt_ref, run_ref):
    i = pl.program_id(0)
    tn = lt_ref.shape[1]
    n_experts = run_ref.shape[0]

    @pl.when(i == 0)
    def _():
        run_ref[...] = jnp.zeros_like(run_ref)

    gl = lt_ref[0:N_EXPERT_GROUPS, :]
    gmax = jnp.max(gl, axis=0, keepdims=True)
    gi = lax.broadcasted_iota(jnp.int32, gl.shape, 0)
    gidx = jnp.min(jnp.where(gl == gmax, gi, N_EXPERT_GROUPS), axis=0, keepdims=True)
    gw = 1.0 / jnp.sum(jnp.exp(gl - gmax), axis=0, keepdims=True)
    epg = EXPERTS_PER_GROUP
    e_in = lt_ref[8:8 + epg, :]
    for g in range(1, N_EXPERT_GROUPS):
        e_in = jnp.where(gidx == g, lt_ref[8 + g * epg:8 + (g + 1) * epg, :], e_in)
    ei = lax.broadcasted_iota(jnp.int32, e_in.shape, 0)
    v0 = jnp.max(e_in, axis=0, keepdims=True)
    i0 = jnp.min(jnp.where(e_in == v0, ei, epg), axis=0, keepdims=True)
    rest = jnp.where(ei == i0, -jnp.inf, e_in)
    v1 = jnp.max(rest, axis=0, keepdims=True)
    i1 = jnp.min(jnp.where(rest == v1, ei, epg), axis=0, keepdims=True)
    t = jnp.exp(v1 - v0)
    w0 = gw / (1.0 + t)
    w1 = gw * t / (1.0 + t)
    e0 = gidx * epg + i0
    e1 = gidx * epg + i1
    zi = jnp.zeros_like(e0)
    e_ref[...] = jnp.concatenate([e0, e1] + [zi] * 6, axis=0)
    w_ref[...] = jnp.concatenate([w0, w1] + [jnp.zeros_like(w0)] * 6, axis=0)

    both = jnp.concatenate([e0, e1], axis=1)
    hit = lax.broadcasted_iota(jnp.int32, (n_experts, 2 * tn), 0) == both
    tri = (lax.broadcasted_iota(jnp.int32, (2 * tn, 2 * tn), 0)
           <= lax.broadcasted_iota(jnp.int32, (2 * tn, 2 * tn), 1))
    pref = _dot(jnp.where(hit, 1.0, 0.0).astype(BF16), jnp.where(tri, 1.0, 0.0).astype(BF16))
    run = run_ref[:, 0:1]
    rank = jnp.sum(jnp.where(hit, pref + run, 0.0), axis=0, keepdims=True) - 1.0
    rank = rank.astype(jnp.int32)
    r_ref[...] = jnp.concatenate([rank[:, :tn], rank[:, tn:]] + [zi] * 6, axis=0)
    run_new = jnp.broadcast_to(run + pref[:, 2 * tn - 1:2 * tn], run_ref.shape)
    run_ref[...] = run_new
    cnt_ref[...] = run_new.astype(jnp.int32)


def _route(logits_t, n_experts):
    rows, n = logits_t.shape
    tn = min(n, 256)
    tile = pl.BlockSpec((8, tn), lambda i: (0, i))
    return pl.pallas_call(
        _route_kernel,
        grid=(n // tn,),
        in_specs=[pl.BlockSpec((rows, tn), lambda i: (0, i))],
        out_specs=[tile, tile, tile, pl.BlockSpec((n_experts, LANE), lambda i: (0, 0))],
        out_shape=[jax.ShapeDtypeStruct((8, n), jnp.int32), jax.ShapeDtypeStruct((8, n), F32),
                   jax.ShapeDtypeStruct((8, n), jnp.int32), jax.ShapeDtypeStruct((n_experts, LANE), jnp.int32)],
        scratch_shapes=[pltpu.VMEM((n_experts, LANE), F32)],
        compiler_params=pltpu.CompilerParams(dimension_semantics=("arbitrary",)),
        name="route_top2_rank",
    )(logits_t)


def _dispatch_kernel(pos_ref, starts_ref, ends_ref, h_ref, x_hbm, zbuf, sem, zsem,
                     *, n_tok, n_experts, tm_e, rt):
    i = pl.program_id(0)
    tm = h_ref.shape[0] // rt
    base = i * tm
    span = tm_e * rt

    @pl.when(i == 0)
    def _():
        zbuf[...] = jnp.zeros_like(zbuf)

        def tail_copy(e):
            start = pl.multiple_of((ends_ref[e] - tm_e) * rt, span)
            return pltpu.make_async_copy(zbuf, x_hbm.at[pl.ds(start, span), :], zsem)

        def fill(e, c):
            @pl.when(ends_ref[e] > starts_ref[e])
            def _():
                tail_copy(e).start()
            return c

        def fill_wait(e, c):
            @pl.when(ends_ref[e] > starts_ref[e])
            def _():
                tail_copy(e).wait()
            return c

        def free_copy(t):
            return pltpu.make_async_copy(zbuf, x_hbm.at[pl.ds(pl.multiple_of(t * span, span), span), :], zsem)

        def free_fill(t, c):
            free_copy(t).start()
            return c

        def free_wait(t, c):
            free_copy(t).wait()
            return c

        first_free = ends_ref[n_experts - 1] // tm_e
        n_tiles = x_hbm.shape[0] // span
        lax.fori_loop(0, n_experts, fill, 0)
        lax.fori_loop(first_free, n_tiles, free_fill, 0)
        lax.fori_loop(0, n_experts, fill_wait, 0)
        lax.fori_loop(first_free, n_tiles, free_wait, 0)

    def row_copy(r, slot):
        p = pl.multiple_of(pos_ref[slot * n_tok + base + r], rt)
        src = h_ref.at[pl.ds(pl.multiple_of(r * rt, rt), rt), :]
        return pltpu.make_async_copy(src, x_hbm.at[pl.ds(p, rt), :], sem)

    def issue(r, c):
        row_copy(r, 0).start()
        row_copy(r, 1).start()
        return c

    def drain(r, c):
        row_copy(r, 0).wait()
        row_copy(r, 1).wait()
        return c

    lax.fori_loop(0, tm, issue, 0, unroll=DMA_UNROLL)
    lax.fori_loop(0, tm, drain, 0, unroll=DMA_UNROLL)


def _dispatch(h2, pos, starts, ends, n_rows, tm_e, rt):
    n_tok = h2.shape[0] // rt
    tm = min(n_tok, 512)
    grid_spec = pltpu.PrefetchScalarGridSpec(
        num_scalar_prefetch=3,
        grid=(n_tok // tm,),
        in_specs=[pl.BlockSpec((tm * rt, LANE), lambda i, *_: (i, 0))],
        out_specs=pl.BlockSpec(memory_space=pl.ANY),
        scratch_shapes=[pltpu.VMEM((tm_e * rt, LANE), F32), pltpu.SemaphoreType.DMA(()),
                        pltpu.SemaphoreType.DMA(())],
    )
    return pl.pallas_call(
        functools.partial(_dispatch_kernel, n_tok=n_tok, n_experts=starts.shape[0], tm_e=tm_e, rt=rt),
        grid_spec=grid_spec,
        out_shape=jax.ShapeDtypeStruct((n_rows * rt, LANE), F32),
        compiler_params=pltpu.CompilerParams(dimension_semantics=("arbitrary",), vmem_limit_bytes=VMEM_LIMIT),
        name="dispatch_rows",
    )(pos, starts, ends, h2)


def _expert_kernel(tile_e_ref, used_ref, x_ref, wg_ref, wu_ref, wd_ref, y_ref, wg_b, wu_b, wd_b):
    i = pl.program_id(0)
    used = i < used_ref[0]
    fresh = jnp.logical_or(i == 0, tile_e_ref[i] != tile_e_ref[jnp.maximum(i - 1, 0)])

    @pl.when(jnp.logical_and(used, fresh))
    def _():
        wg_b[...] = wg_ref[0].astype(BF16)
        wu_b[...] = wu_ref[0].astype(BF16)
        wd_b[...] = wd_ref[0].astype(BF16)

    @pl.when(used)
    def _():
        rt = wg_b.shape[0] // LANE
        tm = x_ref.shape[0] // rt
        xb = _load_row_tiles(x_ref, (), 0, tm, rt).astype(BF16)
        a = _dot(xb, wg_b[...])
        b = _dot(xb, wu_b[...])
        _store_row_tiles(y_ref, (), 0, _dot((a * jax.nn.sigmoid(a) * b).astype(BF16), wd_b[...]))

    @pl.when(jnp.logical_not(used))
    def _():
        y_ref[...] = jnp.zeros_like(y_ref)


def _experts(x_sorted, tile_expert, n_used, w_gate, w_up, w_down, tm):
    d, d_e = w_gate.shape[1], w_gate.shape[2]
    rt = d // LANE
    n_rows = x_sorted.shape[0] // rt
    last = lambda i, used: jnp.minimum(i, used[0] - 1)
    grid_spec = pltpu.PrefetchScalarGridSpec(
        num_scalar_prefetch=2,
        grid=(n_rows // tm,),
        in_specs=[pl.BlockSpec((tm * rt, LANE), lambda i, te, used: (last(i, used), 0)),
                  pl.BlockSpec((1, d, d_e), lambda i, te, used: (te[i], 0, 0)),
                  pl.BlockSpec((1, d, d_e), lambda i, te, used: (te[i], 0, 0)),
                  pl.BlockSpec((1, d_e, d), lambda i, te, used: (te[i], 0, 0))],
        out_specs=pl.BlockSpec((tm * rt, LANE), lambda i, te, used: (i, 0)),
        scratch_shapes=[pltpu.VMEM((d, d_e), BF16), pltpu.VMEM((d, d_e), BF16), pltpu.VMEM((d_e, d), BF16)],
    )
    return pl.pallas_call(
        _expert_kernel,
        grid_spec=grid_spec,
        out_shape=jax.ShapeDtypeStruct((n_rows * rt, LANE), F32),
        compiler_params=pltpu.CompilerParams(dimension_semantics=("arbitrary",), vmem_limit_bytes=VMEM_LIMIT),
        name="routed_experts",
    )(tile_expert, n_used, x_sorted, w_gate, w_up, w_down)


def _final_kernel(pos_ref, x1_ref, w_ref, mod_ref, ln_ref, y_hbm, o_ref, ybuf, sems, *, alpha, n_tok):
    i = pl.program_id(0)
    n_steps = pl.num_programs(0)
    tm, d = x1_ref.shape
    rt = d // LANE

    def row_copy(step, r, slot):
        buf = step % 2
        p = pl.multiple_of(pos_ref[slot * n_tok + step * tm + r], rt)
        dst = ybuf.at[2 * buf + slot, pl.ds(pl.multiple_of(r * rt, rt), rt), :]
        return pltpu.make_async_copy(y_hbm.at[pl.ds(p, rt), :], dst, sems.at[buf])

    def issue_tile(step):
        def issue(r, c):
            row_copy(step, r, 0).start()
            row_copy(step, r, 1).start()
            return c
        lax.fori_loop(0, tm, issue, 0, unroll=DMA_UNROLL)

    @pl.when(i == 0)
    def _():
        issue_tile(i)

    @pl.when(i + 1 < n_steps)
    def _():
        issue_tile(i + 1)

    def drain(r, c):
        row_copy(i, r, 0).wait()
        row_copy(i, r, 1).wait()
        return c

    lax.fori_loop(0, tm, drain, 0, unroll=DMA_UNROLL)
    cur = 2 * (i % 2)
    moe = (w_ref[:, 0:1] * _load_row_tiles(ybuf, (cur,), 0, tm, rt)
           + w_ref[:, 1:2] * _load_row_tiles(ybuf, (cur + 1,), 0, tm, rt))
    z = alpha * x1_ref[...] + mod_ref[0] * moe
    o_ref[...] = _layer_norm(z) * ln_ref[0:1, :] + ln_ref[1:2, :]


def _final(x1, pos, wts, y_sorted, g2, ln, alpha, seq):
    n_tok, d = x1.shape
    rt = d // LANE
    tm = min(seq, 256)
    per_b = seq // tm
    grid_spec = pltpu.PrefetchScalarGridSpec(
        num_scalar_prefetch=1,
        grid=(n_tok // tm,),
        in_specs=[pl.BlockSpec((tm, d), lambda i, *_: (i, 0)),
                  pl.BlockSpec((tm, 2), lambda i, *_: (i, 0)),
                  pl.BlockSpec((1, 1, d), lambda i, *_: (i // per_b, 0, 0)),
                  pl.BlockSpec((2, d), lambda i, *_: (0, 0)),
                  pl.BlockSpec(memory_space=pl.ANY)],
        out_specs=pl.BlockSpec((tm, d), lambda i, *_: (i, 0)),
        scratch_shapes=[pltpu.VMEM((4, tm * rt, LANE), F32), pltpu.SemaphoreType.DMA((2,))],
    )
    return pl.pallas_call(
        functools.partial(_final_kernel, alpha=alpha, n_tok=n_tok),
        grid_spec=grid_spec,
        out_shape=jax.ShapeDtypeStruct((n_tok, d), F32),
        compiler_params=pltpu.CompilerParams(dimension_semantics=("arbitrary",), vmem_limit_bytes=VMEM_LIMIT),
        name="combine_postnorm",
    )(pos, x1, wts, g2, ln, y_sorted)


def _tile_plan(counts, n_rows, tm):
    n_experts = counts.shape[0]
    padded = ((counts + tm - 1) // tm) * tm
    ends = jnp.cumsum(padded).astype(jnp.int32)
    starts = ends - padded
    tile_start = jnp.arange(n_rows // tm, dtype=jnp.int32) * tm
    tile_e = jnp.sum((tile_start[:, None] >= ends[None, :]).astype(jnp.int32), axis=1)
    n_used = ends[-1:] // tm
    tile_e = jnp.minimum(tile_e, jnp.max(jnp.where(counts > 0, jnp.arange(n_experts, dtype=jnp.int32), 0)))
    return starts, ends, tile_e, n_used


def kernel(x, c, ctx, c_ctx, w_mod, b_mod, w_in, s5_a_re, s5_a_im, s5_log_dt, s5_b_re, s5_b_im, s5_c_re, s5_c_im, s5_d, w_glu_a, w_glu_b, q_gain, k_gain, w_attn_o, w_out, ln1_g, ln1_b, w_router_group, b_router_group, w_router_expert, b_router_expert, w_exp_gate, w_exp_up, w_exp_down, ln2_g, ln2_b):
    bsz, n_lat, d = x.shape
    n_ctx = ctx.shape[1]
    assert w_mod.shape[0] == DEPTH == 1 and bsz + 1 <= 8
    alpha = (2.0 * DEPTH) ** 0.25
    d_s5 = s5_d.shape[1]
    d_q = w_attn_o.shape[1]
    d_kv = N_KV_HEADS * HEAD_DIM
    n_experts = w_exp_gate.shape[1]

    cond = jnp.zeros((8, d), F32).at[:bsz].set(c).at[bsz].set(c_ctx)
    mod = _adaln(cond, w_mod[0], b_mod[0]).reshape(8, 6, d)
    mod_lat = mod[:bsz]
    mod_ctx = jnp.broadcast_to(mod[bsz:bsz + 1], (bsz, 6, d))

    w_in_b = w_in[0].astype(BF16)
    cos, sin = _rope_tables(n_lat)
    qg, kg = q_gain[0].reshape(1, HEAD_DIM), k_gain[0].reshape(1, HEAD_DIM)
    dims = dict(d_s5=d_s5, d_q=d_q, d_kv=d_kv)
    w_vt = w_in_b[:, d_s5 + d_q + d_kv:d_s5 + d_q + 2 * d_kv].T
    u, q, k, vt, gates = _inproj(x, mod_lat[:, 0:2], w_in_b, w_vt, cos, sin, qg, kg, latent=True, **dims)
    uc, kc, vct = _inproj(ctx, mod_ctx[:, 0:2], w_in_b, w_vt, cos[:n_ctx], sin[:n_ctx], qg, kg, latent=False,
                          **dims)

    w_t, w_z, w_c, a_chunk = _s5_weights(s5_a_re[0], s5_a_im[0], s5_log_dt[0], s5_b_re[0], s5_b_im[0],
                                         s5_c_re[0], s5_c_im[0])
    y_ssm = _s5_branch(u, uc, w_t, w_z, w_c, a_chunk, s5_d[0])

    o = _attention(q, k, vt, kc, vct)

    n_r = 8 + n_experts
    n_r_pad = ((n_r + LANE - 1) // LANE) * LANE
    w_r = jnp.zeros((d, n_r_pad), F32).at[:, :N_EXPERT_GROUPS].set(w_router_group[0])
    w_r = w_r.at[:, 8:n_r].set(w_router_expert[0])
    b_r = jnp.zeros((1, n_r_pad), F32).at[0, :N_EXPERT_GROUPS].set(b_router_group[0])
    b_r = b_r.at[0, 8:n_r].set(b_router_expert[0])
    wr_hi, wr_lo = _split_bf16(w_r)
    ln1 = jnp.stack([ln1_g[0], ln1_b[0]])
    x1, h2, logits = _merge(x, y_ssm, o, gates, mod_lat[:, 2:5], w_glu_a[0].astype(BF16),
                            w_glu_b[0].astype(BF16), w_attn_o[0].astype(BF16), w_out[0].astype(BF16),
                            ln1, wr_hi, wr_lo, b_r, alpha)

    n_tok = bsz * n_lat
    tm_e = EXPERT_TILE
    n_rows = 2 * n_tok + n_experts * tm_e
    logits_t = logits.reshape(n_tok, n_r_pad)[:, :n_r].T
    eid, wts, rank, counts = _route(logits_t, n_experts)
    starts, ends, tile_e, n_used = _tile_plan(counts[:, 0], n_rows, tm_e)
    own = eid[:2, :, None] == jnp.arange(n_experts, dtype=jnp.int32)
    rt = d // LANE
    pos = ((jnp.sum(jnp.where(own, starts, 0), axis=-1) + rank[:2]) * rt).reshape(-1)
    x_sorted = _dispatch(h2.reshape(n_tok * rt, LANE), pos, starts, ends, n_rows, tm_e, rt)
    y_sorted = _experts(x_sorted, tile_e, n_used, w_exp_gate[0], w_exp_up[0], w_exp_down[0], tm_e)
    ln2 = jnp.stack([ln2_g[0], ln2_b[0]])
    out = _final(x1.reshape(n_tok, d), pos, wts[:2].T, y_sorted, mod_lat[:, 5:6], ln2, alpha, n_lat)
    return out.reshape(bsz, n_lat, d)
```

```python
import functools
import math

import jax
import jax.numpy as jnp
from jax import lax
from jax.experimental import pallas as pl
from jax.experimental.pallas import tpu as pltpu

GRID_W = 64
S5_GROUP_CH = 16
S5_STATE = 64
HEAD_DIM = 128
N_KV_HEADS = 2
ROPE_THETA = 10000.0
N_EXPERT_GROUPS = 4
EXPERTS_PER_GROUP = 8
NORM_EPS = 1e-6
DEPTH = 1

S5_CHUNK = 16
S5_GROUP_BLOCK = 8
EXPERT_TILE = 256
DMA_UNROLL = 8
ATTN_Q_TILE = 512
ATTN_KV_CHUNK = 3072
MERGE_SUB_ROWS = 256
LANE = 128
VMEM_LIMIT = 56 * 1024 * 1024

F32 = jnp.float32
BF16 = jnp.bfloat16


def _layer_norm(x):
    mu = jnp.mean(x, axis=-1, keepdims=True)
    xc = x - mu
    var = jnp.mean(xc * xc, axis=-1, keepdims=True)
    return xc * lax.rsqrt(var + NORM_EPS)


def _split_bf16(a):
    hi = a.astype(BF16)
    lo = (a - hi.astype(F32)).astype(BF16)
    return hi, lo


def _dot(a, b):
    return jnp.dot(a, b, preferred_element_type=F32)


def _store_row_tiles(ref, lead, r0, val):
    n, d = val.shape
    rt = d // LANE
    for s in range(rt):
        ref[lead + (pl.ds(r0 * rt + s, n, stride=rt), slice(None))] = val[:, s * LANE:(s + 1) * LANE]


def _load_row_tiles(ref, lead, r0, n, rt):
    return jnp.concatenate([ref[lead + (pl.ds(r0 * rt + s, n, stride=rt), slice(None))] for s in range(rt)],
                           axis=1)


def _dot3(a_hi, a_lo, b_hi, b_lo):
    return _dot(a_hi, b_hi) + _dot(a_hi, b_lo) + _dot(a_lo, b_hi)


def _adaln_kernel(c_ref, w_ref, b_ref, o_ref):
    c = c_ref[...]
    s = c * jax.nn.sigmoid(c)
    s_hi, s_lo = _split_bf16(s)
    w_hi, w_lo = _split_bf16(w_ref[...])
    o_ref[...] = _dot3(s_hi, s_lo, w_hi, w_lo) + b_ref[...]


def _adaln(cond, w, b):
    rows, d = cond.shape
    n = w.shape[1]
    tn = min(n, 1024)
    return pl.pallas_call(
        _adaln_kernel,
        grid=(n // tn,),
        in_specs=[pl.BlockSpec((rows, d), lambda j: (0, 0)),
                  pl.BlockSpec((d, tn), lambda j: (0, j)),
                  pl.BlockSpec((1, tn), lambda j: (0, j))],
        out_specs=pl.BlockSpec((rows, tn), lambda j: (0, j)),
        out_shape=jax.ShapeDtypeStruct((rows, n), F32),
        compiler_params=pltpu.CompilerParams(vmem_limit_bytes=VMEM_LIMIT),
        name="adaln",
    )(cond, w, b.reshape(1, n))


def _rms_rope(t, gain, cos, sin_signed, first_half, scale):
    r = lax.rsqrt(jnp.mean(t * t, axis=-1, keepdims=True) + NORM_EPS)
    tn = t * r * gain
    if cos is not None:
        partner = jnp.where(first_half, pltpu.roll(tn, HEAD_DIM - 32, 1), pltpu.roll(tn, 32, 1))
        tn = tn * cos + partner * sin_signed
    if scale != 1.0:
        tn = tn * scale
    return tn


def _inproj_kernel(x_ref, mod_ref, w_ref, cos_ref, sin_ref, qg_ref, kg_ref, *out_refs,
                   d_s5, d_q, d_kv, latent):
    x = x_ref[0]
    shift = mod_ref[0, 0:1, :]
    scale = mod_ref[0, 1:2, :]
    h = (_layer_norm(x) * (1.0 + scale) + shift).astype(BF16)
    o_q = d_s5
    o_k = o_q + d_q
    o_v = o_k + d_kv
    o_g = o_v + d_kv
    if latent:
        u_ref, q_ref, k_ref, v_ref, g_ref = out_refs
        cos = cos_ref[...]
        sin = sin_ref[...]
        lane = lax.broadcasted_iota(jnp.int32, cos.shape, 1)
        first_half = (lane % 64) < 32
    else:
        u_ref, k_ref, v_ref = out_refs
        cos = sin = first_half = None
    k = _dot(h, w_ref[:, o_k:o_v])
    if latent:
        q = _dot(h, w_ref[:, o_q:o_k])
    for hd in range(d_kv // HEAD_DIM):
        sl = slice(hd * HEAD_DIM, (hd + 1) * HEAD_DIM)
        k_ref[0, :, sl] = _rms_rope(k[:, sl], kg_ref[...], cos, sin, first_half, 1.0).astype(BF16)
    u_ref[0] = _dot(h, w_ref[:, 0:d_s5])
    v_ref[0] = _dot(h, w_ref[:, o_v:o_g]).astype(BF16)
    if latent:
        d_half = (w_ref.shape[1] - o_g) // 2
        g_lo = _dot(h, w_ref[:, o_g:o_g + d_half])
        q_scale = HEAD_DIM ** -0.5 * math.log2(math.e)
        n_qh = d_q // HEAD_DIM
        for hd in range(n_qh // 2):
            sl = slice(hd * HEAD_DIM, (hd + 1) * HEAD_DIM)
            q_ref[0, :, sl] = _rms_rope(q[:, sl], qg_ref[...], cos, sin, first_half, q_scale).astype(BF16)
        g_hi = _dot(h, w_ref[:, o_g + d_half:])
        for hd in range(n_qh // 2, n_qh):
            sl = slice(hd * HEAD_DIM, (hd + 1) * HEAD_DIM)
            q_ref[0, :, sl] = _rms_rope(q[:, sl], qg_ref[...], cos, sin, first_half, q_scale).astype(BF16)
        g_ref[0, :, 0:d_half] = jax.nn.sigmoid(g_lo).astype(BF16)
        g_ref[0, :, d_half:] = jax.nn.sigmoid(g_hi).astype(BF16)


def _inproj(x, mod, w_in, cos, sin, q_gain, k_gain, *, d_s5, d_q, d_kv, latent):
    bsz, n, d = x.shape
    n_in = w_in.shape[1]
    d_gate = n_in - d_s5 - d_q - 2 * d_kv
    tm = min(n, 512)
    row = lambda w: pl.BlockSpec((1, tm, w), lambda b, i: (b, i, 0))
    out_shape = [jax.ShapeDtypeStruct((bsz, n, d_s5), F32)]
    out_specs = [row(d_s5)]
    if latent:
        out_shape.append(jax.ShapeDtypeStruct((bsz, n, d_q), BF16))
        out_specs.append(row(d_q))
    out_shape += [jax.ShapeDtypeStruct((bsz, n, d_kv), BF16)] * 2
    out_specs += [row(d_kv)] * 2
    if latent:
        out_shape.append(jax.ShapeDtypeStruct((bsz, n, d_gate), BF16))
        out_specs.append(row(d_gate))
    return pl.pallas_call(
        functools.partial(_inproj_kernel, d_s5=d_s5, d_q=d_q, d_kv=d_kv, latent=latent),
        grid=(bsz, n // tm),
        in_specs=[row(d),
                  pl.BlockSpec((1, 2, d), lambda b, i: (b, 0, 0)),
                  pl.BlockSpec((d, n_in), lambda b, i: (0, 0), pipeline_mode=pl.Buffered(1)),
                  pl.BlockSpec((tm, HEAD_DIM), lambda b, i: (i, 0)),
                  pl.BlockSpec((tm, HEAD_DIM), lambda b, i: (i, 0)),
                  pl.BlockSpec((1, HEAD_DIM), lambda b, i: (0, 0)),
                  pl.BlockSpec((1, HEAD_DIM), lambda b, i: (0, 0))],
        out_specs=out_specs,
        out_shape=out_shape,
        compiler_params=pltpu.CompilerParams(vmem_limit_bytes=VMEM_LIMIT),
        name="inproj_latent" if latent else "inproj_context",
    )(x, mod, w_in, cos, sin, q_gain, k_gain)


def _rope_tables(n_lat):
    rows = n_lat // GRID_W
    axis_dim = HEAD_DIM // 2
    inv = ROPE_THETA ** (-jnp.arange(0, axis_dim, 2, dtype=F32) / axis_dim)
    ang_r = jnp.arange(rows, dtype=F32)[:, None] * inv
    ang_c = jnp.arange(GRID_W, dtype=F32)[:, None] * inv
    per_row = lambda t: jnp.repeat(t, GRID_W, axis=0)
    per_col = lambda t: jnp.tile(t, (rows, 1))
    cos_r, sin_r = per_row(jnp.cos(ang_r)), per_row(jnp.sin(ang_r))
    cos_c, sin_c = per_col(jnp.cos(ang_c)), per_col(jnp.sin(ang_c))
    cos = jnp.concatenate([cos_r, cos_r, cos_c, cos_c], axis=1)
    sin = jnp.concatenate([-sin_r, sin_r, -sin_c, sin_c], axis=1)
    return cos, sin


def _s5_weights(a_re, a_im, log_dt, b_re, b_im, c_re, c_im):
    hp = lax.Precision.HIGHEST
    lc = S5_CHUNK
    dt = jnp.exp(log_dt)[..., None]
    lam_re, lam_im = a_re * dt, a_im * dt
    ea = jnp.exp(lam_re)
    ab_re, ab_im = ea * jnp.cos(lam_im), ea * jnp.sin(lam_im)
    den = a_re * a_re + a_im * a_im
    nr, ni = ab_re - 1.0, ab_im
    rr = (nr * a_re + ni * a_im) / den
    ri = (ni * a_re - nr * a_im) / den
    bb_re = rr[..., None] * b_re - ri[..., None] * b_im
    bb_im = rr[..., None] * b_im + ri[..., None] * b_re
    kk = jnp.arange(lc + 1, dtype=F32)[:, None, None, None]
    pk_mag = jnp.exp(kk * lam_re)
    pk_re, pk_im = pk_mag * jnp.cos(kk * lam_im), pk_mag * jnp.sin(kk * lam_im)
    n_g = a_re.shape[1]
    kw = lc * S5_GROUP_CH
    rows = lambda t: jnp.moveaxis(t, 0, 1).reshape(n_g, kw, t.shape[-1])
    bt_re, bt_im = jnp.swapaxes(bb_re, -1, -2), jnp.swapaxes(bb_im, -1, -2)
    pw_re, pw_im = pk_re[:lc, :, :, None, :], pk_im[:lc, :, :, None, :]
    akb_re = pw_re * bt_re - pw_im * bt_im
    akb_im = pw_re * bt_im + pw_im * bt_re
    n_lag = 2 * lc - 1
    ch = S5_GROUP_CH
    gp = lambda t: jnp.moveaxis(t, 0, -1)
    lag_pw = lambda t: jnp.stack([jnp.pad(gp(t[:lc, 0]), ((0, 0), (0, 0), (lc - 1, 0))),
                                  jnp.pad(gp(t[lc - 1::-1, 1]), ((0, 0), (0, 0), (0, lc - 1)))])
    rep_j = jnp.repeat(jnp.eye(n_lag, dtype=F32), ch, axis=1)
    til_c = jnp.tile(jnp.eye(ch, dtype=F32), (1, n_lag))
    pkx_re = jnp.einsum('dgpj,jm->dgpm', lag_pw(pk_re), rep_j, precision=hp)
    pkx_im = jnp.einsum('dgpj,jm->dgpm', lag_pw(pk_im), rep_j, precision=hp)
    cx_re = jnp.einsum('dgop,om->dgpm', c_re, til_c, precision=hp)
    cx_im = jnp.einsum('dgop,om->dgpm', c_im, til_c, precision=hp)
    cax_re = cx_re * pkx_re - cx_im * pkx_im
    cax_im = cx_re * pkx_im + cx_im * pkx_re
    strip = (jnp.einsum('dgip,dgpm->gim', bt_re, cax_re, precision=hp)
             - jnp.einsum('dgip,dgpm->gim', bt_im, cax_im, precision=hp))
    w_t = jnp.concatenate([strip[:, :, ch * (lc - 1 - s):ch * (lc - 1 - s) + kw] for s in range(lc)], axis=1)
    w_z = jnp.concatenate([rows(akb_re[::-1, 0]), rows(akb_re[:, 1]),
                           rows(akb_im[::-1, 0]), rows(akb_im[:, 1])], axis=2)
    pf_re, pf_im = pk_re[1:, 0], pk_im[1:, 0]
    pr_re, pr_im = pk_re[lc:0:-1, 1], pk_im[lc:0:-1, 1]

    def cpow(cr, ci, pr, pi):
        return cr[None] * pr[:, :, None, :] - ci[None] * pi[:, :, None, :], \
               cr[None] * pi[:, :, None, :] + ci[None] * pr[:, :, None, :]

    cf_re, cf_im = cpow(c_re[0], c_im[0], pf_re, pf_im)
    cr_re, cr_im = cpow(c_re[1], c_im[1], pr_re, pr_im)
    w_c = jnp.concatenate([rows(cf_re), rows(cr_re), rows(-cf_im), rows(-cr_im)], axis=2)
    w_c = jnp.swapaxes(w_c, 1, 2)
    a_chunk = jnp.concatenate([pk_re[lc, 0], pk_re[lc, 1], pk_im[lc, 0], pk_im[lc, 1]], axis=1)
    return w_t.astype(BF16), w_z.astype(BF16), w_c.astype(BF16), a_chunk


def _regroup_rows(n_rows):
    return min(n_rows, 64)


def _s5_kernel(ul_ref, uc_ref, wt_ref, wz_ref, wc_ref, a_ref, d_ref, y_ref,
               lhs_l, lhs_c, zl_re, zl_im, zc_re, zc_im, yg_ref, *, nb, ncl, ncc):
    gb = S5_GROUP_BLOCK
    lc, ch = S5_CHUNK, S5_GROUP_CH
    per_tile = LANE // ch
    nl = nb * ncl
    ncx = nb * ncc
    half = 2 * S5_STATE
    lane_blk = lambda rows: lax.broadcasted_iota(jnp.int32, (rows, LANE), 1) // ch

    def block_transpose(arrs, blk):
        n = len(arrs)
        rolled = []
        for k in range(n):
            w = arrs[k]
            for g in range(1, n):
                w = jnp.where(blk == g, arrs[(g + k) % n], w)
            rolled.append(w if k == 0 else pltpu.roll(w, ch * k, 1))
        outs = []
        for b in range(n):
            o = rolled[0]
            for k in range(1, n):
                o = jnp.where(blk == (b + k) % n, rolled[k], o)
            outs.append(o)
        return outs

    def gather_chunks(src_ref, dst_ref, n_chunks):
        rb = _regroup_rows(n_chunks)
        blk = lane_blk(rb)

        def step(i, carry):
            r0 = pl.multiple_of(i * rb, rb)
            for hh in range(lc // per_tile):
                ut = [src_ref[pl.ds(r0 * lc + hh * per_tile + j, rb, stride=lc), :] for j in range(per_tile)]
                for g, out in enumerate(block_transpose(ut, blk)):
                    dst_ref[g, pl.ds(r0, rb), hh * LANE:(hh + 1) * LANE] = out.astype(BF16)
            return carry

        lax.fori_loop(0, n_chunks // rb, step, 0)

    gather_chunks(ul_ref, lhs_l, nl)
    gather_chunks(uc_ref, lhs_c, ncx)

    for g in range(gb):
        zl = _dot(lhs_l[g], wz_ref[g])
        zl_re[pl.ds(g, nl, stride=gb), :] = zl[:, :half]
        zl_im[pl.ds(g, nl, stride=gb), :] = zl[:, half:]
        zc = _dot(lhs_c[g], wz_ref[g])
        zc_re[pl.ds(g, ncx, stride=gb), :] = zc[:, :half]
        zc_im[pl.ds(g, ncx, stride=gb), :] = zc[:, half:]
    a_re = a_ref[:, :half]
    a_im = a_ref[:, half:]
    fwd = lax.broadcasted_iota(jnp.int32, (gb, half), 1) < S5_STATE

    def advance(h_re, h_im, z_re, z_im):
        return a_re * h_re - a_im * h_im + z_re, a_re * h_im + a_im * h_re + z_im

    def ctx_step(i, carry):
        out = []
        for b in range(nb):
            sl_f = pl.ds(pl.multiple_of((b * ncc + i) * gb, gb), gb)
            sl_r = pl.ds(pl.multiple_of((b * ncc + ncc - 1 - i) * gb, gb), gb)
            z_re = jnp.where(fwd, zc_re[sl_f, :], zc_re[sl_r, :])
            z_im = jnp.where(fwd, zc_im[sl_f, :], zc_im[sl_r, :])
            out.extend(advance(carry[2 * b], carry[2 * b + 1], z_re, z_im))
        return tuple(out)

    def lat_step(i, carry):
        out = []
        for b in range(nb):
            h_re, h_im = carry[2 * b], carry[2 * b + 1]
            sl_f = pl.ds(pl.multiple_of((b * ncl + i) * gb, gb), gb)
            sl_r = pl.ds(pl.multiple_of((b * ncl + ncl - 1 - i) * gb, gb), gb)
            f_re, f_im, r_re, r_im = zl_re[sl_f, :], zl_im[sl_f, :], zl_re[sl_r, :], zl_im[sl_r, :]
            zl_re[sl_f, :] = jnp.where(fwd, h_re, f_re)
            zl_im[sl_f, :] = jnp.where(fwd, h_im, f_im)
            zl_re[sl_r, :] = jnp.where(fwd, r_re, h_re)
            zl_im[sl_r, :] = jnp.where(fwd, r_im, h_im)
            out.extend(advance(h_re, h_im, jnp.where(fwd, f_re, r_re), jnp.where(fwd, f_im, r_im)))
        return tuple(out)

    zero = jnp.zeros((gb, half), F32)
    carry = lax.fori_loop(0, ncc, ctx_step, (zero,) * (2 * nb))
    lax.fori_loop(0, ncl, lat_step, carry)
    for g in range(gb):
        rows = pl.ds(g, nl, stride=gb)
        h_in = jnp.concatenate([zl_re[rows, :], zl_im[rows, :]], axis=1).astype(BF16)
        yg_ref[g] = _dot(lhs_l[g], wt_ref[g]) + _dot(h_in, wc_ref[g])

    rb = _regroup_rows(nl)
    blk = lane_blk(rb)
    d_row = d_ref[0]

    def scatter_step(i, carry):
        r0 = pl.multiple_of(i * rb, rb)
        for hh in range(lc // per_tile):
            yt = [yg_ref[g, pl.ds(r0, rb), hh * LANE:(hh + 1) * LANE] for g in range(gb)]
            for j, out in enumerate(block_transpose(yt, blk)):
                rows = pl.ds(r0 * lc + hh * per_tile + j, rb, stride=lc)
                y_ref[rows, :] = out + ul_ref[rows, :] * d_row
        return carry

    lax.fori_loop(0, nl // rb, scatter_step, 0)


def _s5_branch(u, uc, w_t, w_z, w_c, a_chunk, s5_d):
    bsz, n, width = u.shape
    n_ctx = uc.shape[1]
    lc, ch = S5_CHUNK, S5_GROUP_CH
    n_g = width // ch
    ncl, ncc = n // lc, n_ctx // lc
    gb = S5_GROUP_BLOCK
    assert ncl % 2 == 0 and n_g % gb == 0 and gb * ch == LANE
    kw = lc * ch
    nl, ncx = bsz * ncl, bsz * ncc
    assert nl % _regroup_rows(nl) == 0 and ncx % _regroup_rows(ncx) == 0
    once = pl.Buffered(1)
    slab = lambda rows: pl.BlockSpec((rows, LANE), lambda i: (0, i), pipeline_mode=once)
    blk3 = lambda r, c: pl.BlockSpec((gb, r, c), lambda i: (i, 0, 0))
    half = 2 * S5_STATE
    y = pl.pallas_call(
        functools.partial(_s5_kernel, nb=bsz, ncl=ncl, ncc=ncc),
        grid=(n_g // gb,),
        in_specs=[slab(bsz * n), slab(bsz * n_ctx), blk3(kw, kw), blk3(kw, 2 * half), blk3(2 * half, kw),
                  pl.BlockSpec((gb, 2 * half), lambda i: (i, 0)),
                  pl.BlockSpec((1, 1, LANE), lambda i: (i, 0, 0))],
        out_specs=slab(bsz * n),
        out_shape=jax.ShapeDtypeStruct((bsz * n, width), F32),
        scratch_shapes=[pltpu.VMEM((gb, nl, kw), BF16), pltpu.VMEM((gb, ncx, kw), BF16),
                        pltpu.VMEM((gb * nl, half), F32), pltpu.VMEM((gb * nl, half), F32),
                        pltpu.VMEM((gb * ncx, half), F32), pltpu.VMEM((gb * ncx, half), F32),
                        pltpu.VMEM((gb, nl, kw), F32)],
        compiler_params=pltpu.CompilerParams(vmem_limit_bytes=VMEM_LIMIT),
        name="s5_chunked_scan",
    )(u.reshape(bsz * n, width), uc.reshape(bsz * n_ctx, width), w_t, w_z, w_c, a_chunk,
      s5_d.reshape(n_g // gb, 1, LANE))
    return y.reshape(bsz, n, width)


def _attn_kernel(q_ref, k_ref, v_ref, kc_ref, vc_ref, o_ref, k_all, v_all, *, tk, group):
    tq = q_ref.shape[1]
    n_lat, n_ctx = k_ref.shape[1], kc_ref.shape[1]
    n_k = n_lat + n_ctx

    @pl.when(pl.program_id(2) == 0)
    def _():
        k_all[0:n_lat, :] = k_ref[0]
        k_all[n_lat:n_k, :] = kc_ref[0]
        v_all[0:n_lat, 0:HEAD_DIM] = v_ref[0]
        v_all[n_lat:n_k, 0:HEAD_DIM] = vc_ref[0]
        v_all[:, HEAD_DIM:] = jnp.ones((n_k, HEAD_DIM), BF16)

    qs = [q_ref[0, :, h * HEAD_DIM:(h + 1) * HEAD_DIM] for h in range(group)]

    def body(c, carry):
        start = pl.multiple_of(c * tk, tk)
        ks = k_all[pl.ds(start, tk), :]
        vs = v_all[pl.ds(start, tk), :]
        out = []
        score = lambda h: lax.dot_general(qs[h], ks, (((1,), (1,)), ((), ())), preferred_element_type=F32)
        s_next = score(0)
        for h in range(group):
            m, acc = carry[2 * h:2 * h + 2]
            s = s_next
            if h + 1 < group:
                s_next = score(h + 1)
            m_new = jnp.maximum(m, jnp.max(s, axis=-1, keepdims=True))
            p = jnp.exp2(s - m_new)
            alpha = jnp.exp2(m - m_new)
            acc = alpha * acc + _dot(p.astype(BF16), vs)
            out.extend((m_new, acc))
        return tuple(out)

    init = (jnp.full((tq, 1), -jnp.inf, F32), jnp.zeros((tq, 2 * HEAD_DIM), F32)) * group
    fin = lax.fori_loop(0, n_k // tk, body, init)
    for h in range(group):
        acc = fin[2 * h + 1]
        o_ref[0, :, h * HEAD_DIM:(h + 1) * HEAD_DIM] = (acc[:, :HEAD_DIM] / acc[:, HEAD_DIM:]).astype(o_ref.dtype)


def _pick_divisor(n, pref):
    best = LANE
    for t in range(LANE, pref + 1, LANE):
        if n % t == 0:
            best = t
    return best


def _attention(q, k, v, kc, vc):
    bsz, n, dq = q.shape
    n_c, dkv = kc.shape[1], k.shape[2]
    n_kv = dkv // HEAD_DIM
    group = dq // dkv
    tq = min(n, ATTN_Q_TILE)
    tk = _pick_divisor(n + n_c, ATTN_KV_CHUNK)
    kv_spec = lambda rows: pl.BlockSpec((1, rows, HEAD_DIM), lambda b, h, i: (b, 0, h))
    return pl.pallas_call(
        functools.partial(_attn_kernel, tk=tk, group=group),
        grid=(bsz, n_kv, n // tq),
        in_specs=[pl.BlockSpec((1, tq, group * HEAD_DIM), lambda b, h, i: (b, i, h)),
                  kv_spec(n), kv_spec(n), kv_spec(n_c), kv_spec(n_c)],
        out_specs=pl.BlockSpec((1, tq, group * HEAD_DIM), lambda b, h, i: (b, i, h)),
        out_shape=jax.ShapeDtypeStruct((bsz, n, dq), BF16),
        scratch_shapes=[pltpu.VMEM((n + n_c, HEAD_DIM), BF16), pltpu.VMEM((n + n_c, 2 * HEAD_DIM), BF16)],
        compiler_params=pltpu.CompilerParams(dimension_semantics=("arbitrary",) * 3, vmem_limit_bytes=VMEM_LIMIT),
        name="gqa_flash_attention",
    )(q, k, v, kc, vc)


def _merge_kernel(x_ref, y_ref, o_ref, g_ref, mod_ref, wa_ref, wb_ref, wo_ref, wout_ref,
                  ln_ref, wrh_ref, wrl_ref, br_ref, x1_ref, h2_ref, lg_ref, *, alpha):
    d = x_ref.shape[2]
    tm = x_ref.shape[1]
    g1 = mod_ref[0, 0:1, :]
    blocks = [slice(r, r + MERGE_SUB_ROWS) for r in range(0, tm, MERGE_SUB_ROWS)]
    stage1 = []
    for rows in blocks:
        att = _dot(o_ref[0, rows, :], wo_ref[...])
        gact = jax.nn.gelu(y_ref[0, rows, :]).astype(BF16)
        stage1.append((att, _dot(gact, wa_ref[...]), _dot(gact, wb_ref[...])))
    stage2 = []
    for rows, (att, a, b) in zip(blocks, stage1):
        gate = g_ref[0, rows, :].astype(F32)
        mixed = (gate[:, :d] * (a * jax.nn.sigmoid(b)) + gate[:, d:] * att).astype(BF16)
        stage2.append(_dot(mixed, wout_ref[...]))
    for rows, mix in zip(blocks, stage2):
        x1 = _layer_norm(alpha * x_ref[0, rows, :] + g1 * mix) * ln_ref[0:1, :] + ln_ref[1:2, :]
        x1_ref[0, rows, :] = x1
        h2 = _layer_norm(x1) * (1.0 + mod_ref[0, 2:3, :]) + mod_ref[0, 1:2, :]
        _store_row_tiles(h2_ref, (0,), rows.start, h2)
        h_hi, h_lo = _split_bf16(h2)
        lg_ref[0, rows, :] = _dot3(h_hi, h_lo, wrh_ref[...], wrl_ref[...]) + br_ref[...]


def _merge(x, y_ssm, o, gates, mod, w_glu_a, w_glu_b, w_attn_o, w_out, ln, wr_hi, wr_lo, br, alpha):
    bsz, n, d = x.shape
    tm = min(n, 512)
    row = lambda w: pl.BlockSpec((1, tm, w), lambda b, i: (b, i, 0))
    const = lambda a: pl.BlockSpec(a.shape, lambda b, i: (0,) * a.ndim)
    nr = wr_hi.shape[1]
    rt = d // LANE
    return pl.pallas_call(
        functools.partial(_merge_kernel, alpha=alpha),
        grid=(bsz, n // tm),
        in_specs=[row(d), row(y_ssm.shape[2]), row(o.shape[2]), row(gates.shape[2]),
                  pl.BlockSpec((1, 3, d), lambda b, i: (b, 0, 0)),
                  const(w_glu_a), const(w_glu_b), const(w_attn_o), const(w_out), const(ln),
                  const(wr_hi), const(wr_lo), const(br)],
        out_specs=[row(d), pl.BlockSpec((1, tm * rt, LANE), lambda b, i: (b, i, 0)), row(nr)],
        out_shape=[jax.ShapeDtypeStruct((bsz, n, d), F32), jax.ShapeDtypeStruct((bsz, n * rt, LANE), F32),
                   jax.ShapeDtypeStruct((bsz, n, nr), F32)],
        compiler_params=pltpu.CompilerParams(vmem_limit_bytes=VMEM_LIMIT),
        name="merge_postnorm_router",
    )(x, y_ssm, o, gates, mod, w_glu_a, w_glu_b, w_attn_o, w_out, ln, wr_hi, wr_lo, br)


def _route_kernel(lt_ref, e_ref, w_ref, r_ref, cnt_ref, run_ref):
    i = pl.program_id(0)
    tn = lt_ref.shape[1]
    n_experts = run_ref.shape[0]

    @pl.when(i == 0)
    def _():
        run_ref[...] = jnp.zeros_like(run_ref)

    gl = lt_ref[0:N_EXPERT_GROUPS, :]
    gmax = jnp.max(gl, axis=0, keepdims=True)
    gi = lax.broadcasted_iota(jnp.int32, gl.shape, 0)
    gidx = jnp.min(jnp.where(gl == gmax, gi, N_EXPERT_GROUPS), axis=0, keepdims=True)
    gw = 1.0 / jnp.sum(jnp.exp(gl - gmax), axis=0, keepdims=True)
    epg = EXPERTS_PER_GROUP
    e_in = lt_ref[8:8 + epg, :]
    for g in range(1, N_EXPERT_GROUPS):
        e_in = jnp.where(gidx == g, lt_ref[8 + g * epg:8 + (g + 1) * epg, :], e_in)
    ei = lax.broadcasted_iota(jnp.int32, e_in.shape, 0)
    v0 = jnp.max(e_in, axis=0, keepdims=True)
    i0 = jnp.min(jnp.where(e_in == v0, ei, epg), axis=0, keepdims=True)
    rest = jnp.where(ei == i0, -jnp.inf, e_in)
    v1 = jnp.max(rest, axis=0, keepdims=True)
    i1 = jnp.min(jnp.where(rest == v1, ei, epg), axis=0, keepdims=True)
    t = jnp.exp(v1 - v0)
    w0 = gw / (1.0 + t)
    w1 = gw * t / (1.0 + t)
    e0 = gidx * epg + i0
    e1 = gidx * epg + i1
    zi = jnp.zeros_like(e0)
    e_ref[...] = jnp.concatenate([e0, e1] + [zi] * 6, axis=0)
    w_ref[...] = jnp.concatenate([w0, w1] + [jnp.zeros_like(w0)] * 6, axis=0)

    both = jnp.concatenate([e0, e1], axis=1)
    hit = lax.broadcasted_iota(jnp.int32, (n_experts, 2 * tn), 0) == both
    tri = (lax.broadcasted_iota(jnp.int32, (2 * tn, 2 * tn), 0)
           <= lax.broadcasted_iota(jnp.int32, (2 * tn, 2 * tn), 1))
    pref = _dot(jnp.where(hit, 1.0, 0.0).astype(BF16), jnp.where(tri, 1.0, 0.0).astype(BF16))
    run = run_ref[:, 0:1]
    rank = jnp.sum(jnp.where(hit, pref + run, 0.0), axis=0, keepdims=True) - 1.0
    rank = rank.astype(jnp.int32)
    r_ref[...] = jnp.concatenate([rank[:, :tn], rank[:, tn:]] + [zi] * 6, axis=0)
    run_new = jnp.broadcast_to(run + pref[:, 2 * tn - 1:2 * tn], run_ref.shape)
    run_ref[...] = run_new
    cnt_ref[...] = run_new.astype(jnp.int32)


def _route(logits_t, n_experts):
    rows, n = logits_t.shape
    tn = min(n, 256)
    tile = pl.BlockSpec((8, tn), lambda i: (0, i))
    return pl.pallas_call(
        _route_kernel,
        grid=(n // tn,),
        in_specs=[pl.BlockSpec((rows, tn), lambda i: (0, i))],
        out_specs=[tile, tile, tile, pl.BlockSpec((n_experts, LANE), lambda i: (0, 0))],
        out_shape=[jax.ShapeDtypeStruct((8, n), jnp.int32), jax.ShapeDtypeStruct((8, n), F32),
                   jax.ShapeDtypeStruct((8, n), jnp.int32), jax.ShapeDtypeStruct((n_experts, LANE), jnp.int32)],
        scratch_shapes=[pltpu.VMEM((n_experts, LANE), F32)],
        compiler_params=pltpu.CompilerParams(dimension_semantics=("arbitrary",)),
        name="route_top2_rank",
    )(logits_t)


def _dispatch_kernel(pos_ref, starts_ref, ends_ref, h_ref, x_hbm, zbuf, sem, zsem,
                     *, n_tok, n_experts, tm_e, rt):
    i = pl.program_id(0)
    tm = h_ref.shape[0] // rt
    base = i * tm
    span = tm_e * rt

    @pl.when(i == 0)
    def _():
        zbuf[...] = jnp.zeros_like(zbuf)

        def tail_copy(e):
            start = pl.multiple_of((ends_ref[e] - tm_e) * rt, span)
            return pltpu.make_async_copy(zbuf, x_hbm.at[pl.ds(start, span), :], zsem)

        def fill(e, c):
            @pl.when(ends_ref[e] > starts_ref[e])
            def _():
                tail_copy(e).start()
            return c

        def fill_wait(e, c):
            @pl.when(ends_ref[e] > starts_ref[e])
            def _():
                tail_copy(e).wait()
            return c

        def free_copy(t):
            return pltpu.make_async_copy(zbuf, x_hbm.at[pl.ds(pl.multiple_of(t * span, span), span), :], zsem)

        def free_fill(t, c):
            free_copy(t).start()
            return c

        def free_wait(t, c):
            free_copy(t).wait()
            return c

        first_free = ends_ref[n_experts - 1] // tm_e
        n_tiles = x_hbm.shape[0] // span
        lax.fori_loop(0, n_experts, fill, 0)
        lax.fori_loop(first_free, n_tiles, free_fill, 0)
        lax.fori_loop(0, n_experts, fill_wait, 0)
        lax.fori_loop(first_free, n_tiles, free_wait, 0)

    def row_copy(r, slot):
        p = pl.multiple_of(pos_ref[slot * n_tok + base + r], rt)
        src = h_ref.at[pl.ds(pl.multiple_of(r * rt, rt), rt), :]
        return pltpu.make_async_copy(src, x_hbm.at[pl.ds(p, rt), :], sem)

    def issue(r, c):
        row_copy(r, 0).start()
        row_copy(r, 1).start()
        return c

    def drain(r, c):
        row_copy(r, 0).wait()
        row_copy(r, 1).wait()
        return c

    lax.fori_loop(0, tm, issue, 0, unroll=DMA_UNROLL)
    lax.fori_loop(0, tm, drain, 0, unroll=DMA_UNROLL)


def _dispatch(h2, pos, starts, ends, n_rows, tm_e, rt):
    n_tok = h2.shape[0] // rt
    tm = min(n_tok, 512)
    grid_spec = pltpu.PrefetchScalarGridSpec(
        num_scalar_prefetch=3,
        grid=(n_tok // tm,),
        in_specs=[pl.BlockSpec((tm * rt, LANE), lambda i, *_: (i, 0))],
        out_specs=pl.BlockSpec(memory_space=pl.ANY),
        scratch_shapes=[pltpu.VMEM((tm_e * rt, LANE), F32), pltpu.SemaphoreType.DMA(()),
                        pltpu.SemaphoreType.DMA(())],
    )
    return pl.pallas_call(
        functools.partial(_dispatch_kernel, n_tok=n_tok, n_experts=starts.shape[0], tm_e=tm_e, rt=rt),
        grid_spec=grid_spec,
        out_shape=jax.ShapeDtypeStruct((n_rows * rt, LANE), F32),
        compiler_params=pltpu.CompilerParams(dimension_semantics=("arbitrary",), vmem_limit_bytes=VMEM_LIMIT),
        name="dispatch_rows",
    )(pos, starts, ends, h2)


def _expert_kernel(tile_e_ref, used_ref, x_ref, wg_ref, wu_ref, wd_ref, y_ref, wg_b, wu_b, wd_b):
    i = pl.program_id(0)
    used = i < used_ref[0]
    fresh = jnp.logical_or(i == 0, tile_e_ref[i] != tile_e_ref[jnp.maximum(i - 1, 0)])

    @pl.when(jnp.logical_and(used, fresh))
    def _():
        wg_b[...] = wg_ref[0].astype(BF16)
        wu_b[...] = wu_ref[0].astype(BF16)
        wd_b[...] = wd_ref[0].astype(BF16)

    @pl.when(used)
    def _():
        rt = wg_b.shape[0] // LANE
        tm = x_ref.shape[0] // rt
        xb = _load_row_tiles(x_ref, (), 0, tm, rt).astype(BF16)
        a = _dot(xb, wg_b[...])
        b = _dot(xb, wu_b[...])
        _store_row_tiles(y_ref, (), 0, _dot((a * jax.nn.sigmoid(a) * b).astype(BF16), wd_b[...]))

    @pl.when(jnp.logical_not(used))
    def _():
        y_ref[...] = jnp.zeros_like(y_ref)


def _experts(x_sorted, tile_expert, n_used, w_gate, w_up, w_down, tm):
    d, d_e = w_gate.shape[1], w_gate.shape[2]
    rt = d // LANE
    n_rows = x_sorted.shape[0] // rt
    last = lambda i, used: jnp.minimum(i, used[0] - 1)
    grid_spec = pltpu.PrefetchScalarGridSpec(
        num_scalar_prefetch=2,
        grid=(n_rows // tm,),
        in_specs=[pl.BlockSpec((tm * rt, LANE), lambda i, te, used: (last(i, used), 0)),
                  pl.BlockSpec((1, d, d_e), lambda i, te, used: (te[i], 0, 0)),
                  pl.BlockSpec((1, d, d_e), lambda i, te, used: (te[i], 0, 0)),
                  pl.BlockSpec((1, d_e, d), lambda i, te, used: (te[i], 0, 0))],
        out_specs=pl.BlockSpec((tm * rt, LANE), lambda i, te, used: (i, 0)),
        scratch_shapes=[pltpu.VMEM((d, d_e), BF16), pltpu.VMEM((d, d_e), BF16), pltpu.VMEM((d_e, d), BF16)],
    )
    return pl.pallas_call(
        _expert_kernel,
        grid_spec=grid_spec,
        out_shape=jax.ShapeDtypeStruct((n_rows * rt, LANE), F32),
        compiler_params=pltpu.CompilerParams(dimension_semantics=("arbitrary",), vmem_limit_bytes=VMEM_LIMIT),
        name="routed_experts",
    )(tile_expert, n_used, x_sorted, w_gate, w_up, w_down)


def _final_kernel(pos_ref, x1_ref, w_ref, mod_ref, ln_ref, y_hbm, o_ref, ybuf, sems, *, alpha, n_tok):
    i = pl.program_id(0)
    n_steps = pl.num_programs(0)
    tm, d = x1_ref.shape
    rt = d // LANE

    def row_copy(step, r, slot):
        buf = step % 2
        p = pl.multiple_of(pos_ref[slot * n_tok + step * tm + r], rt)
        dst = ybuf.at[2 * buf + slot, pl.ds(pl.multiple_of(r * rt, rt), rt), :]
        return pltpu.make_async_copy(y_hbm.at[pl.ds(p, rt), :], dst, sems.at[buf])

    def issue_tile(step):
        def issue(r, c):
            row_copy(step, r, 0).start()
            row_copy(step, r, 1).start()
            return c
        lax.fori_loop(0, tm, issue, 0, unroll=DMA_UNROLL)

    @pl.when(i == 0)
    def _():
        issue_tile(i)

    @pl.when(i + 1 < n_steps)
    def _():
        issue_tile(i + 1)

    def drain(r, c):
        row_copy(i, r, 0).wait()
        row_copy(i, r, 1).wait()
        return c

    lax.fori_loop(0, tm, drain, 0, unroll=DMA_UNROLL)
    cur = 2 * (i % 2)
    moe = (w_ref[:, 0:1] * _load_row_tiles(ybuf, (cur,), 0, tm, rt)
           + w_ref[:, 1:2] * _load_row_tiles(ybuf, (cur + 1,), 0, tm, rt))
    z = alpha * x1_ref[...] + mod_ref[0] * moe
    o_ref[...] = _layer_norm(z) * ln_ref[0:1, :] + ln_ref[1:2, :]


def _final(x1, pos, wts, y_sorted, g2, ln, alpha, seq):
    n_tok, d = x1.shape
    rt = d // LANE
    tm = min(seq, 256)
    per_b = seq // tm
    grid_spec = pltpu.PrefetchScalarGridSpec(
        num_scalar_prefetch=1,
        grid=(n_tok // tm,),
        in_specs=[pl.BlockSpec((tm, d), lambda i, *_: (i, 0)),
                  pl.BlockSpec((tm, 2), lambda i, *_: (i, 0)),
                  pl.BlockSpec((1, 1, d), lambda i, *_: (i // per_b, 0, 0)),
                  pl.BlockSpec((2, d), lambda i, *_: (0, 0)),
                  pl.BlockSpec(memory_space=pl.ANY)],
        out_specs=pl.BlockSpec((tm, d), lambda i, *_: (i, 0)),
        scratch_shapes=[pltpu.VMEM((4, tm * rt, LANE), F32), pltpu.SemaphoreType.DMA((2,))],
    )
    return pl.pallas_call(
        functools.partial(_final_kernel, alpha=alpha, n_tok=n_tok),
        grid_spec=grid_spec,
        out_shape=jax.ShapeDtypeStruct((n_tok, d), F32),
        compiler_params=pltpu.CompilerParams(dimension_semantics=("arbitrary",), vmem_limit_bytes=VMEM_LIMIT),
        name="combine_postnorm",
    )(pos, x1, wts, g2, ln, y_sorted)


def _tile_plan(counts, n_rows, tm):
    n_experts = counts.shape[0]
    padded = ((counts + tm - 1) // tm) * tm
    ends = jnp.cumsum(padded).astype(jnp.int32)
    starts = ends - padded
    tile_start = jnp.arange(n_rows // tm, dtype=jnp.int32) * tm
    tile_e = jnp.sum((tile_start[:, None] >= ends[None, :]).astype(jnp.int32), axis=1)
    n_used = ends[-1:] // tm
    tile_e = jnp.minimum(tile_e, jnp.max(jnp.where(counts > 0, jnp.arange(n_experts, dtype=jnp.int32), 0)))
    return starts, ends, tile_e, n_used


def kernel(x, c, ctx, c_ctx, w_mod, b_mod, w_in, s5_a_re, s5_a_im, s5_log_dt, s5_b_re, s5_b_im, s5_c_re, s5_c_im, s5_d, w_glu_a, w_glu_b, q_gain, k_gain, w_attn_o, w_out, ln1_g, ln1_b, w_router_group, b_router_group, w_router_expert, b_router_expert, w_exp_gate, w_exp_up, w_exp_down, ln2_g, ln2_b):
    bsz, n_lat, d = x.shape
    n_ctx = ctx.shape[1]
    assert w_mod.shape[0] == DEPTH == 1 and bsz + 1 <= 8
    alpha = (2.0 * DEPTH) ** 0.25
    d_s5 = s5_d.shape[1]
    d_q = w_attn_o.shape[1]
    d_kv = N_KV_HEADS * HEAD_DIM
    n_experts = w_exp_gate.shape[1]

    cond = jnp.concatenate([c, c_ctx[None], jnp.zeros((8 - bsz - 1, d), F32)], axis=0)
    mod = _adaln(cond, w_mod[0], b_mod[0]).reshape(8, 6, d)
    mod_lat = mod[:bsz]
    mod_ctx = jnp.broadcast_to(mod[bsz:bsz + 1], (bsz, 6, d))

    w_in_b = w_in[0].astype(BF16)
    cos, sin = _rope_tables(n_lat)
    qg, kg = q_gain[0].reshape(1, HEAD_DIM), k_gain[0].reshape(1, HEAD_DIM)
    dims = dict(d_s5=d_s5, d_q=d_q, d_kv=d_kv)
    u, q, k, v, gates = _inproj(x, mod_lat[:, 0:2], w_in_b, cos, sin, qg, kg, latent=True, **dims)
    uc, kc, vc = _inproj(ctx, mod_ctx[:, 0:2], w_in_b, cos[:n_ctx], sin[:n_ctx], qg, kg, latent=False, **dims)

    w_t, w_z, w_c, a_chunk = _s5_weights(s5_a_re[0], s5_a_im[0], s5_log_dt[0], s5_b_re[0], s5_b_im[0],
                                         s5_c_re[0], s5_c_im[0])
    y_ssm = _s5_branch(u, uc, w_t, w_z, w_c, a_chunk, s5_d[0])

    o = _attention(q, k, v, kc, vc)

    n_r = 8 + n_experts
    n_r_pad = ((n_r + LANE - 1) // LANE) * LANE
    pad_cols = lambda g, e: jnp.concatenate(
        [g, jnp.zeros(g.shape[:-1] + (8 - N_EXPERT_GROUPS,), F32), e,
         jnp.zeros(g.shape[:-1] + (n_r_pad - n_r,), F32)], axis=-1)
    w_r = pad_cols(w_router_group[0], w_router_expert[0])
    b_r = pad_cols(b_router_group[0][None], b_router_expert[0][None])
    wr_hi, wr_lo = _split_bf16(w_r)
    ln1 = jnp.stack([ln1_g[0], ln1_b[0]])
    x1, h2, logits = _merge(x, y_ssm, o, gates, mod_lat[:, 2:5], w_glu_a[0].astype(BF16),
                            w_glu_b[0].astype(BF16), w_attn_o[0].astype(BF16), w_out[0].astype(BF16),
                            ln1, wr_hi, wr_lo, b_r, alpha)

    n_tok = bsz * n_lat
    tm_e = EXPERT_TILE
    n_rows = 2 * n_tok + n_experts * tm_e
    logits_t = logits.reshape(n_tok, n_r_pad)[:, :n_r].T
    eid, wts, rank, counts = _route(logits_t, n_experts)
    starts, ends, tile_e, n_used = _tile_plan(counts[:, 0], n_rows, tm_e)
    own = eid[:2, :, None] == jnp.arange(n_experts, dtype=jnp.int32)
    rt = d // LANE
    pos = ((jnp.sum(jnp.where(own, starts, 0), axis=-1) + rank[:2]) * rt).reshape(-1)
    x_sorted = _dispatch(h2.reshape(n_tok * rt, LANE), pos, starts, ends, n_rows, tm_e, rt)
    y_sorted = _experts(x_sorted, tile_e, n_used, w_exp_gate[0], w_exp_up[0], w_exp_down[0], tm_e)
    ln2 = jnp.stack([ln2_g[0], ln2_b[0]])
    out = _final(x1.reshape(n_tok, d), pos, wts[:2].T, y_sorted, mod_lat[:, 5:6], ln2, alpha, n_lat)
    return out.reshape(bsz, n_lat, d)
```

```python
import functools
import math

import jax
import jax.numpy as jnp
from jax import lax
from jax.experimental import pallas as pl
from jax.experimental.pallas import tpu as pltpu

GRID_W = 64
S5_GROUP_CH = 16
S5_STATE = 64
HEAD_DIM = 128
N_KV_HEADS = 2
ROPE_THETA = 10000.0
N_EXPERT_GROUPS = 4
EXPERTS_PER_GROUP = 8
NORM_EPS = 1e-6
DEPTH = 1

S5_CHUNK = 16
S5_GROUP_BLOCK = 8
EXPERT_TILE = 256
DMA_UNROLL = 8
ATTN_Q_TILE = 512
ATTN_KV_CHUNK = 3072
INPROJ_SUB_ROWS = 256
MERGE_SUB_ROWS = 256
LANE = 128
VMEM_LIMIT = 56 * 1024 * 1024

F32 = jnp.float32
BF16 = jnp.bfloat16


def _layer_norm(x):
    mu = jnp.mean(x, axis=-1, keepdims=True)
    xc = x - mu
    var = jnp.mean(xc * xc, axis=-1, keepdims=True)
    return xc * lax.rsqrt(var + NORM_EPS)


def _split_bf16(a):
    hi = a.astype(BF16)
    lo = (a - hi.astype(F32)).astype(BF16)
    return hi, lo


def _dot(a, b):
    return jnp.dot(a, b, preferred_element_type=F32)


def _store_row_tiles(ref, lead, r0, val):
    n, d = val.shape
    rt = d // LANE
    for s in range(rt):
        ref[lead + (pl.ds(r0 * rt + s, n, stride=rt), slice(None))] = val[:, s * LANE:(s + 1) * LANE]


def _load_row_tiles(ref, lead, r0, n, rt):
    return jnp.concatenate([ref[lead + (pl.ds(r0 * rt + s, n, stride=rt), slice(None))] for s in range(rt)],
                           axis=1)


def _dot3(a_hi, a_lo, b_hi, b_lo):
    return _dot(a_hi, b_hi) + _dot(a_hi, b_lo) + _dot(a_lo, b_hi)


def _adaln_kernel(c_ref, w_ref, b_ref, o_ref):
    c = c_ref[...]
    s = c * jax.nn.sigmoid(c)
    s_hi, s_lo = _split_bf16(s)
    w_hi, w_lo = _split_bf16(w_ref[...])
    o_ref[...] = _dot3(s_hi, s_lo, w_hi, w_lo) + b_ref[...]


def _adaln(cond, w, b):
    rows, d = cond.shape
    n = w.shape[1]
    tn = min(n, 1024)
    return pl.pallas_call(
        _adaln_kernel,
        grid=(n // tn,),
        in_specs=[pl.BlockSpec((rows, d), lambda j: (0, 0)),
                  pl.BlockSpec((d, tn), lambda j: (0, j)),
                  pl.BlockSpec((1, tn), lambda j: (0, j))],
        out_specs=pl.BlockSpec((rows, tn), lambda j: (0, j)),
        out_shape=jax.ShapeDtypeStruct((rows, n), F32),
        compiler_params=pltpu.CompilerParams(vmem_limit_bytes=VMEM_LIMIT),
        name="adaln",
    )(cond, w, b.reshape(1, n))


def _rms_rope(t, gain, cos, sin_signed, first_half, scale):
    r = lax.rsqrt(jnp.mean(t * t, axis=-1, keepdims=True) + NORM_EPS)
    tn = t * r * gain
    if cos is not None:
        partner = jnp.where(first_half, pltpu.roll(tn, HEAD_DIM - 32, 1), pltpu.roll(tn, 32, 1))
        tn = tn * cos + partner * sin_signed
    if scale != 1.0:
        tn = tn * scale
    return tn


def _inproj_kernel(x_ref, mod_ref, w_ref, cos_ref, sin_ref, qg_ref, kg_ref, *out_refs,
                   d_s5, d_q, d_kv, latent):
    shift = mod_ref[0, 0:1, :]
    scale = mod_ref[0, 1:2, :]
    o_q = d_s5
    o_k = o_q + d_q
    o_v = o_k + d_kv
    o_g = o_v + d_kv
    tm = x_ref.shape[1]
    sub = min(tm, INPROJ_SUB_ROWS)
    for r0 in range(0, tm, sub):
        rows = slice(r0, r0 + sub)
        h = (_layer_norm(x_ref[0, rows, :]) * (1.0 + scale) + shift).astype(BF16)
        if latent:
            u_ref, q_ref, k_ref, v_ref, g_ref = out_refs
            cos = cos_ref[rows, :]
            sin = sin_ref[rows, :]
            lane = lax.broadcasted_iota(jnp.int32, cos.shape, 1)
            first_half = (lane % 64) < 32
        else:
            u_ref, k_ref, v_ref = out_refs
            cos = sin = first_half = None
        k = _dot(h, w_ref[:, o_k:o_v])
        if latent:
            q = _dot(h, w_ref[:, o_q:o_k])
        for hd in range(d_kv // HEAD_DIM):
            sl = slice(hd * HEAD_DIM, (hd + 1) * HEAD_DIM)
            k_ref[0, rows, sl] = _rms_rope(k[:, sl], kg_ref[...], cos, sin, first_half, 1.0).astype(BF16)
        u_ref[0, rows, :] = _dot(h, w_ref[:, 0:d_s5])
        v_ref[0, rows, :] = _dot(h, w_ref[:, o_v:o_g]).astype(BF16)
        if latent:
            d_half = (w_ref.shape[1] - o_g) // 2
            g_lo = _dot(h, w_ref[:, o_g:o_g + d_half])
            q_scale = HEAD_DIM ** -0.5 * math.log2(math.e)
            n_qh = d_q // HEAD_DIM
            for hd in range(n_qh // 2):
                sl = slice(hd * HEAD_DIM, (hd + 1) * HEAD_DIM)
                q_ref[0, rows, sl] = _rms_rope(q[:, sl], qg_ref[...], cos, sin, first_half, q_scale).astype(BF16)
            g_hi = _dot(h, w_ref[:, o_g + d_half:])
            for hd in range(n_qh // 2, n_qh):
                sl = slice(hd * HEAD_DIM, (hd + 1) * HEAD_DIM)
                q_ref[0, rows, sl] = _rms_rope(q[:, sl], qg_ref[...], cos, sin, first_half, q_scale).astype(BF16)
            g_ref[0, rows, 0:d_half] = jax.nn.sigmoid(g_lo).astype(BF16)
            g_ref[0, rows, d_half:] = jax.nn.sigmoid(g_hi).astype(BF16)


def _inproj(x, mod, w_in, cos, sin, q_gain, k_gain, *, d_s5, d_q, d_kv, latent):
    bsz, n, d = x.shape
    n_in = w_in.shape[1]
    d_gate = n_in - d_s5 - d_q - 2 * d_kv
    tm = min(n, 512)
    row = lambda w: pl.BlockSpec((1, tm, w), lambda b, i: (b, i, 0))
    out_shape = [jax.ShapeDtypeStruct((bsz, n, d_s5), F32)]
    out_specs = [row(d_s5)]
    if latent:
        out_shape.append(jax.ShapeDtypeStruct((bsz, n, d_q), BF16))
        out_specs.append(row(d_q))
    out_shape += [jax.ShapeDtypeStruct((bsz, n, d_kv), BF16)] * 2
    out_specs += [row(d_kv)] * 2
    if latent:
        out_shape.append(jax.ShapeDtypeStruct((bsz, n, d_gate), BF16))
        out_specs.append(row(d_gate))
    return pl.pallas_call(
        functools.partial(_inproj_kernel, d_s5=d_s5, d_q=d_q, d_kv=d_kv, latent=latent),
        grid=(bsz, n // tm),
        in_specs=[row(d),
                  pl.BlockSpec((1, 2, d), lambda b, i: (b, 0, 0)),
                  pl.BlockSpec((d, n_in), lambda b, i: (0, 0), pipeline_mode=pl.Buffered(1)),
                  pl.BlockSpec((tm, HEAD_DIM), lambda b, i: (i, 0)),
                  pl.BlockSpec((tm, HEAD_DIM), lambda b, i: (i, 0)),
                  pl.BlockSpec((1, HEAD_DIM), lambda b, i: (0, 0)),
                  pl.BlockSpec((1, HEAD_DIM), lambda b, i: (0, 0))],
        out_specs=out_specs,
        out_shape=out_shape,
        compiler_params=pltpu.CompilerParams(vmem_limit_bytes=VMEM_LIMIT),
        name="inproj_latent" if latent else "inproj_context",
    )(x, mod, w_in, cos, sin, q_gain, k_gain)


def _rope_tables(n_lat):
    rows = n_lat // GRID_W
    axis_dim = HEAD_DIM // 2
    inv = ROPE_THETA ** (-jnp.arange(0, axis_dim, 2, dtype=F32) / axis_dim)
    ang_r = jnp.arange(rows, dtype=F32)[:, None] * inv
    ang_c = jnp.arange(GRID_W, dtype=F32)[:, None] * inv
    per_row = lambda t: jnp.repeat(t, GRID_W, axis=0)
    per_col = lambda t: jnp.tile(t, (rows, 1))
    cos_r, sin_r = per_row(jnp.cos(ang_r)), per_row(jnp.sin(ang_r))
    cos_c, sin_c = per_col(jnp.cos(ang_c)), per_col(jnp.sin(ang_c))
    cos = jnp.concatenate([cos_r, cos_r, cos_c, cos_c], axis=1)
    sin = jnp.concatenate([-sin_r, sin_r, -sin_c, sin_c], axis=1)
    return cos, sin


def _s5_weights(a_re, a_im, log_dt, b_re, b_im, c_re, c_im):
    hp = lax.Precision.HIGHEST
    lc = S5_CHUNK
    dt = jnp.exp(log_dt)[..., None]
    lam_re, lam_im = a_re * dt, a_im * dt
    ea = jnp.exp(lam_re)
    ab_re, ab_im = ea * jnp.cos(lam_im), ea * jnp.sin(lam_im)
    den = a_re * a_re + a_im * a_im
    nr, ni = ab_re - 1.0, ab_im
    rr = (nr * a_re + ni * a_im) / den
    ri = (ni * a_re - nr * a_im) / den
    bb_re = rr[..., None] * b_re - ri[..., None] * b_im
    bb_im = rr[..., None] * b_im + ri[..., None] * b_re
    kk = jnp.arange(lc + 1, dtype=F32)[:, None, None, None]
    pk_mag = jnp.exp(kk * lam_re)
    pk_re, pk_im = pk_mag * jnp.cos(kk * lam_im), pk_mag * jnp.sin(kk * lam_im)
    n_g = a_re.shape[1]
    kw = lc * S5_GROUP_CH
    rows = lambda t: jnp.moveaxis(t, 0, 1).reshape(n_g, kw, t.shape[-1])
    bt_re, bt_im = jnp.swapaxes(bb_re, -1, -2), jnp.swapaxes(bb_im, -1, -2)
    pw_re, pw_im = pk_re[:lc, :, :, None, :], pk_im[:lc, :, :, None, :]
    akb_re = pw_re * bt_re - pw_im * bt_im
    akb_im = pw_re * bt_im + pw_im * bt_re
    n_lag = 2 * lc - 1
    ch = S5_GROUP_CH
    gp = lambda t: jnp.moveaxis(t, 0, -1)
    lag_pw = lambda t: jnp.stack([jnp.pad(gp(t[:lc, 0]), ((0, 0), (0, 0), (lc - 1, 0))),
                                  jnp.pad(gp(t[lc - 1::-1, 1]), ((0, 0), (0, 0), (0, lc - 1)))])
    rep_j = jnp.repeat(jnp.eye(n_lag, dtype=F32), ch, axis=1)
    til_c = jnp.tile(jnp.eye(ch, dtype=F32), (1, n_lag))
    pkx_re = jnp.einsum('dgpj,jm->dgpm', lag_pw(pk_re), rep_j, precision=hp)
    pkx_im = jnp.einsum('dgpj,jm->dgpm', lag_pw(pk_im), rep_j, precision=hp)
    cx_re = jnp.einsum('dgop,om->dgpm', c_re, til_c, precision=hp)
    cx_im = jnp.einsum('dgop,om->dgpm', c_im, til_c, precision=hp)
    cax_re = cx_re * pkx_re - cx_im * pkx_im
    cax_im = cx_re * pkx_im + cx_im * pkx_re
    strip = (jnp.einsum('dgip,dgpm->gim', bt_re, cax_re, precision=hp)
             - jnp.einsum('dgip,dgpm->gim', bt_im, cax_im, precision=hp))
    w_t = jnp.concatenate([strip[:, :, ch * (lc - 1 - s):ch * (lc - 1 - s) + kw] for s in range(lc)], axis=1)
    w_z = jnp.concatenate([rows(akb_re[::-1, 0]), rows(akb_re[:, 1]),
                           rows(akb_im[::-1, 0]), rows(akb_im[:, 1])], axis=2)
    pf_re, pf_im = pk_re[1:, 0], pk_im[1:, 0]
    pr_re, pr_im = pk_re[lc:0:-1, 1], pk_im[lc:0:-1, 1]

    def cpow(cr, ci, pr, pi):
        return cr[None] * pr[:, :, None, :] - ci[None] * pi[:, :, None, :], \
               cr[None] * pi[:, :, None, :] + ci[None] * pr[:, :, None, :]

    cf_re, cf_im = cpow(c_re[0], c_im[0], pf_re, pf_im)
    cr_re, cr_im = cpow(c_re[1], c_im[1], pr_re, pr_im)
    w_c = jnp.concatenate([rows(cf_re), rows(cr_re), rows(-cf_im), rows(-cr_im)], axis=2)
    w_c = jnp.swapaxes(w_c, 1, 2)
    a_chunk = jnp.concatenate([pk_re[lc, 0], pk_re[lc, 1], pk_im[lc, 0], pk_im[lc, 1]], axis=1)
    return w_t.astype(BF16), w_z.astype(BF16), w_c.astype(BF16), a_chunk


def _regroup_rows(n_rows):
    return min(n_rows, 64)


def _s5_kernel(ul_ref, uc_ref, wt_ref, wz_ref, wc_ref, a_ref, d_ref, y_ref,
               lhs_l, lhs_c, zl_re, zl_im, zc_re, zc_im, yg_ref, *, nb, ncl, ncc):
    gb = S5_GROUP_BLOCK
    lc, ch = S5_CHUNK, S5_GROUP_CH
    per_tile = LANE // ch
    nl = nb * ncl
    ncx = nb * ncc
    half = 2 * S5_STATE
    lane_blk = lambda rows: lax.broadcasted_iota(jnp.int32, (rows, LANE), 1) // ch

    def block_transpose(arrs, blk):
        n = len(arrs)
        rolled = []
        for k in range(n):
            w = arrs[k]
            for g in range(1, n):
                w = jnp.where(blk == g, arrs[(g + k) % n], w)
            rolled.append(w if k == 0 else pltpu.roll(w, ch * k, 1))
        outs = []
        for b in range(n):
            o = rolled[0]
            for k in range(1, n):
                o = jnp.where(blk == (b + k) % n, rolled[k], o)
            outs.append(o)
        return outs

    def gather_chunks(src_ref, dst_ref, n_chunks):
        rb = _regroup_rows(n_chunks)
        blk = lane_blk(rb)

        def step(i, carry):
            r0 = pl.multiple_of(i * rb, rb)
            for hh in range(lc // per_tile):
                ut = [src_ref[pl.ds(r0 * lc + hh * per_tile + j, rb, stride=lc), :] for j in range(per_tile)]
                for g, out in enumerate(block_transpose(ut, blk)):
                    dst_ref[g, pl.ds(r0, rb), hh * LANE:(hh + 1) * LANE] = out.astype(BF16)
            return carry

        lax.fori_loop(0, n_chunks // rb, step, 0)

    gather_chunks(ul_ref, lhs_l, nl)
    gather_chunks(uc_ref, lhs_c, ncx)

    for g in range(gb):
        zl = _dot(lhs_l[g], wz_ref[g])
        zl_re[pl.ds(g, nl, stride=gb), :] = zl[:, :half]
        zl_im[pl.ds(g, nl, stride=gb), :] = zl[:, half:]
        zc = _dot(lhs_c[g], wz_ref[g])
        zc_re[pl.ds(g, ncx, stride=gb), :] = zc[:, :half]
        zc_im[pl.ds(g, ncx, stride=gb), :] = zc[:, half:]
    a_re = a_ref[:, :half]
    a_im = a_ref[:, half:]
    fwd = lax.broadcasted_iota(jnp.int32, (gb, half), 1) < S5_STATE

    def advance(h_re, h_im, z_re, z_im):
        return a_re * h_re - a_im * h_im + z_re, a_re * h_im + a_im * h_re + z_im

    def ctx_step(i, carry):
        out = []
        for b in range(nb):
            sl_f = pl.ds(pl.multiple_of((b * ncc + i) * gb, gb), gb)
            sl_r = pl.ds(pl.multiple_of((b * ncc + ncc - 1 - i) * gb, gb), gb)
            z_re = jnp.where(fwd, zc_re[sl_f, :], zc_re[sl_r, :])
            z_im = jnp.where(fwd, zc_im[sl_f, :], zc_im[sl_r, :])
            out.extend(advance(carry[2 * b], carry[2 * b + 1], z_re, z_im))
        return tuple(out)

    def lat_step(i, carry):
        out = []
        for b in range(nb):
            h_re, h_im = carry[2 * b], carry[2 * b + 1]
            sl_f = pl.ds(pl.multiple_of((b * ncl + i) * gb, gb), gb)
            sl_r = pl.ds(pl.multiple_of((b * ncl + ncl - 1 - i) * gb, gb), gb)
            f_re, f_im, r_re, r_im = zl_re[sl_f, :], zl_im[sl_f, :], zl_re[sl_r, :], zl_im[sl_r, :]
            zl_re[sl_f, :] = jnp.where(fwd, h_re, f_re)
            zl_im[sl_f, :] = jnp.where(fwd, h_im, f_im)
            zl_re[sl_r, :] = jnp.where(fwd, r_re, h_re)
            zl_im[sl_r, :] = jnp.where(fwd, r_im, h_im)
            out.extend(advance(h_re, h_im, jnp.where(fwd, f_re, r_re), jnp.where(fwd, f_im, r_im)))
        return tuple(out)

    zero = jnp.zeros((gb, half), F32)
    carry = lax.fori_loop(0, ncc, ctx_step, (zero,) * (2 * nb))
    lax.fori_loop(0, ncl, lat_step, carry)
    for g in range(gb):
        rows = pl.ds(g, nl, stride=gb)
        h_in = jnp.concatenate([zl_re[rows, :], zl_im[rows, :]], axis=1).astype(BF16)
        yg_ref[g] = _dot(lhs_l[g], wt_ref[g]) + _dot(h_in, wc_ref[g])

    rb = _regroup_rows(nl)
    blk = lane_blk(rb)
    d_row = d_ref[0]

    def scatter_step(i, carry):
        r0 = pl.multiple_of(i * rb, rb)
        for hh in range(lc // per_tile):
            yt = [yg_ref[g, pl.ds(r0, rb), hh * LANE:(hh + 1) * LANE] for g in range(gb)]
            for j, out in enumerate(block_transpose(yt, blk)):
                rows = pl.ds(r0 * lc + hh * per_tile + j, rb, stride=lc)
                y_ref[rows, :] = out + ul_ref[rows, :] * d_row
        return carry

    lax.fori_loop(0, nl // rb, scatter_step, 0)


def _s5_branch(u, uc, w_t, w_z, w_c, a_chunk, s5_d):
    bsz, n, width = u.shape
    n_ctx = uc.shape[1]
    lc, ch = S5_CHUNK, S5_GROUP_CH
    n_g = width // ch
    ncl, ncc = n // lc, n_ctx // lc
    gb = S5_GROUP_BLOCK
    assert ncl % 2 == 0 and n_g % gb == 0 and gb * ch == LANE
    kw = lc * ch
    nl, ncx = bsz * ncl, bsz * ncc
    assert nl % _regroup_rows(nl) == 0 and ncx % _regroup_rows(ncx) == 0
    once = pl.Buffered(1)
    slab = lambda rows: pl.BlockSpec((rows, LANE), lambda i: (0, i), pipeline_mode=once)
    blk3 = lambda r, c: pl.BlockSpec((gb, r, c), lambda i: (i, 0, 0))
    half = 2 * S5_STATE
    y = pl.pallas_call(
        functools.partial(_s5_kernel, nb=bsz, ncl=ncl, ncc=ncc),
        grid=(n_g // gb,),
        in_specs=[slab(bsz * n), slab(bsz * n_ctx), blk3(kw, kw), blk3(kw, 2 * half), blk3(2 * half, kw),
                  pl.BlockSpec((gb, 2 * half), lambda i: (i, 0)),
                  pl.BlockSpec((1, 1, LANE), lambda i: (i, 0, 0))],
        out_specs=slab(bsz * n),
        out_shape=jax.ShapeDtypeStruct((bsz * n, width), F32),
        scratch_shapes=[pltpu.VMEM((gb, nl, kw), BF16), pltpu.VMEM((gb, ncx, kw), BF16),
                        pltpu.VMEM((gb * nl, half), F32), pltpu.VMEM((gb * nl, half), F32),
                        pltpu.VMEM((gb * ncx, half), F32), pltpu.VMEM((gb * ncx, half), F32),
                        pltpu.VMEM((gb, nl, kw), F32)],
        compiler_params=pltpu.CompilerParams(vmem_limit_bytes=VMEM_LIMIT),
        name="s5_chunked_scan",
    )(u.reshape(bsz * n, width), uc.reshape(bsz * n_ctx, width), w_t, w_z, w_c, a_chunk,
      s5_d.reshape(n_g // gb, 1, LANE))
    return y.reshape(bsz, n, width)


def _attn_kernel(q_ref, k_ref, v_ref, kc_ref, vc_ref, o_ref, k_all, v_all, *, tk, group):
    tq = q_ref.shape[1]
    n_lat, n_ctx = k_ref.shape[1], kc_ref.shape[1]
    n_k = n_lat + n_ctx

    @pl.when(pl.program_id(2) == 0)
    def _():
        k_all[0:n_lat, :] = k_ref[0]
        k_all[n_lat:n_k, :] = kc_ref[0]
        v_all[0:n_lat, 0:HEAD_DIM] = v_ref[0]
        v_all[n_lat:n_k, 0:HEAD_DIM] = vc_ref[0]
        v_all[:, HEAD_DIM:] = jnp.ones((n_k, HEAD_DIM), BF16)

    qs = [q_ref[0, :, h * HEAD_DIM:(h + 1) * HEAD_DIM] for h in range(group)]

    def body(c, carry):
        start = pl.multiple_of(c * tk, tk)
        ks = k_all[pl.ds(start, tk), :]
        vs = v_all[pl.ds(start, tk), :]
        out = []
        score = lambda h: lax.dot_general(qs[h], ks, (((1,), (1,)), ((), ())), preferred_element_type=F32)
        s_next = score(0)
        for h in range(group):
            m, acc = carry[2 * h:2 * h + 2]
            s = s_next
            if h + 1 < group:
                s_next = score(h + 1)
            m_new = jnp.maximum(m, jnp.max(s, axis=-1, keepdims=True))
            p = jnp.exp2(s - m_new)
            alpha = jnp.exp2(m - m_new)
            acc = alpha * acc + _dot(p.astype(BF16), vs)
            out.extend((m_new, acc))
        return tuple(out)

    init = (jnp.full((tq, 1), -jnp.inf, F32), jnp.zeros((tq, 2 * HEAD_DIM), F32)) * group
    fin = lax.fori_loop(0, n_k // tk, body, init)
    for h in range(group):
        acc = fin[2 * h + 1]
        o_ref[0, :, h * HEAD_DIM:(h + 1) * HEAD_DIM] = (acc[:, :HEAD_DIM] / acc[:, HEAD_DIM:]).astype(o_ref.dtype)


def _pick_divisor(n, pref):
    best = LANE
    for t in range(LANE, pref + 1, LANE):
        if n % t == 0:
            best = t
    return best


def _attention(q, k, v, kc, vc):
    bsz, n, dq = q.shape
    n_c, dkv = kc.shape[1], k.shape[2]
    n_kv = dkv // HEAD_DIM
    group = dq // dkv
    tq = min(n, ATTN_Q_TILE)
    tk = _pick_divisor(n + n_c, ATTN_KV_CHUNK)
    kv_spec = lambda rows: pl.BlockSpec((1, rows, HEAD_DIM), lambda b, h, i: (b, 0, h))
    return pl.pallas_call(
        functools.partial(_attn_kernel, tk=tk, group=group),
        grid=(bsz, n_kv, n // tq),
        in_specs=[pl.BlockSpec((1, tq, group * HEAD_DIM), lambda b, h, i: (b, i, h)),
                  kv_spec(n), kv_spec(n), kv_spec(n_c), kv_spec(n_c)],
        out_specs=pl.BlockSpec((1, tq, group * HEAD_DIM), lambda b, h, i: (b, i, h)),
        out_shape=jax.ShapeDtypeStruct((bsz, n, dq), BF16),
        scratch_shapes=[pltpu.VMEM((n + n_c, HEAD_DIM), BF16), pltpu.VMEM((n + n_c, 2 * HEAD_DIM), BF16)],
        compiler_params=pltpu.CompilerParams(dimension_semantics=("arbitrary",) * 3, vmem_limit_bytes=VMEM_LIMIT),
        name="gqa_flash_attention",
    )(q, k, v, kc, vc)


def _merge_kernel(x_ref, y_ref, o_ref, g_ref, mod_ref, wa_ref, wb_ref, wo_ref, wout_ref,
                  ln_ref, wrh_ref, wrl_ref, br_ref, x1_ref, h2_ref, lg_ref, *, alpha):
    d = x_ref.shape[2]
    tm = x_ref.shape[1]
    g1 = mod_ref[0, 0:1, :]
    blocks = [slice(r, r + MERGE_SUB_ROWS) for r in range(0, tm, MERGE_SUB_ROWS)]
    stage1 = []
    for rows in blocks:
        att = _dot(o_ref[0, rows, :], wo_ref[...])
        gact = jax.nn.gelu(y_ref[0, rows, :]).astype(BF16)
        stage1.append((att, _dot(gact, wa_ref[...]), _dot(gact, wb_ref[...])))
    stage2 = []
    for rows, (att, a, b) in zip(blocks, stage1):
        gate = g_ref[0, rows, :].astype(F32)
        mixed = (gate[:, :d] * (a * jax.nn.sigmoid(b)) + gate[:, d:] * att).astype(BF16)
        stage2.append(_dot(mixed, wout_ref[...]))
    for rows, mix in zip(blocks, stage2):
        x1 = _layer_norm(alpha * x_ref[0, rows, :] + g1 * mix) * ln_ref[0:1, :] + ln_ref[1:2, :]
        x1_ref[0, rows, :] = x1
        h2 = _layer_norm(x1) * (1.0 + mod_ref[0, 2:3, :]) + mod_ref[0, 1:2, :]
        _store_row_tiles(h2_ref, (0,), rows.start, h2)
        h_hi, h_lo = _split_bf16(h2)
        lg_ref[0, rows, :] = _dot3(h_hi, h_lo, wrh_ref[...], wrl_ref[...]) + br_ref[...]


def _merge(x, y_ssm, o, gates, mod, w_glu_a, w_glu_b, w_attn_o, w_out, ln, wr_hi, wr_lo, br, alpha):
    bsz, n, d = x.shape
    tm = min(n, 512)
    row = lambda w: pl.BlockSpec((1, tm, w), lambda b, i: (b, i, 0))
    const = lambda a: pl.BlockSpec(a.shape, lambda b, i: (0,) * a.ndim)
    nr = wr_hi.shape[1]
    rt = d // LANE
    return pl.pallas_call(
        functools.partial(_merge_kernel, alpha=alpha),
        grid=(bsz, n // tm),
        in_specs=[row(d), row(y_ssm.shape[2]), row(o.shape[2]), row(gates.shape[2]),
                  pl.BlockSpec((1, 3, d), lambda b, i: (b, 0, 0)),
                  const(w_glu_a), const(w_glu_b), const(w_attn_o), const(w_out), const(ln),
                  const(wr_hi), const(wr_lo), const(br)],
        out_specs=[row(d), pl.BlockSpec((1, tm * rt, LANE), lambda b, i: (b, i, 0)), row(nr)],
        out_shape=[jax.ShapeDtypeStruct((bsz, n, d), F32), jax.ShapeDtypeStruct((bsz, n * rt, LANE), F32),
                   jax.ShapeDtypeStruct((bsz, n, nr), F32)],
        compiler_params=pltpu.CompilerParams(vmem_limit_bytes=VMEM_LIMIT),
        name="merge_postnorm_router",
    )(x, y_ssm, o, gates, mod, w_glu_a, w_glu_b, w_attn_o, w_out, ln, wr_hi, wr_lo, br)


def _route_kernel(lt_ref, e_ref, w_ref, r_ref, cnt_ref, run_ref):
    i = pl.program_id(0)
    tn = lt_ref.shape[1]
    n_experts = run_ref.shape[0]

    @pl.when(i == 0)
    def _():
        run_ref[...] = jnp.zeros_like(run_ref)

    gl = lt_ref[0:N_EXPERT_GROUPS, :]
    gmax = jnp.max(gl, axis=0, keepdims=True)
    gi = lax.broadcasted_iota(jnp.int32, gl.shape, 0)
    gidx = jnp.min(jnp.where(gl == gmax, gi, N_EXPERT_GROUPS), axis=0, keepdims=True)
    gw = 1.0 / jnp.sum(jnp.exp(gl - gmax), axis=0, keepdims=True)
    epg = EXPERTS_PER_GROUP
    e_in = lt_ref[8:8 + epg, :]
    for g in range(1, N_EXPERT_GROUPS):
        e_in = jnp.where(gidx == g, lt_ref[8 + g * epg:8 + (g + 1) * epg, :], e_in)
    ei = lax.broadcasted_iota(jnp.int32, e_in.shape, 0)
    v0 = jnp.max(e_in, axis=0, keepdims=True)
    i0 = jnp.min(jnp.where(e_in == v0, ei, epg), axis=0, keepdims=True)
    rest = jnp.where(ei == i0, -jnp.inf, e_in)
    v1 = jnp.max(rest, axis=0, keepdims=True)
    i1 = jnp.min(jnp.where(rest == v1, ei, epg), axis=0, keepdims=True)
    t = jnp.exp(v1 - v0)
    w0 = gw / (1.0 + t)
    w1 = gw * t / (1.0 + t)
    e0 = gidx * epg + i0
    e1 = gidx * epg + i1
    zi = jnp.zeros_like(e0)
    e_ref[...] = jnp.concatenate([e0, e1] + [zi] * 6, axis=0)
    w_ref[...] = jnp.concatenate([w0, w1] + [jnp.zeros_like(w0)] * 6, axis=0)

    both = jnp.concatenate([e0, e1], axis=1)
    hit = lax.broadcasted_iota(jnp.int32, (n_experts, 2 * tn), 0) == both
    tri = (lax.broadcasted_iota(jnp.int32, (2 * tn, 2 * tn), 0)
           <= lax.broadcasted_iota(jnp.int32, (2 * tn, 2 * tn), 1))
    pref = _dot(jnp.where(hit, 1.0, 0.0).astype(BF16), jnp.where(tri, 1.0, 0.0).astype(BF16))
    run = run_ref[:, 0:1]
    rank = jnp.sum(jnp.where(hit, pref + run, 0.0), axis=0, keepdims=True) - 1.0
    rank = rank.astype(jnp.int32)
    r_ref[...] = jnp.concatenate([rank[:, :tn], rank[:, tn:]] + [zi] * 6, axis=0)
    run_new = jnp.broadcast_to(run + pref[:, 2 * tn - 1:2 * tn], run_ref.shape)
    run_ref[...] = run_new
    cnt_ref[...] = run_new.astype(jnp.int32)


def _route(logits_t, n_experts):
    rows, n = logits_t.shape
    tn = min(n, 256)
    tile = pl.BlockSpec((8, tn), lambda i: (0, i))
    return pl.pallas_call(
        _route_kernel,
        grid=(n // tn,),
        in_specs=[pl.BlockSpec((rows, tn), lambda i: (0, i))],
        out_specs=[tile, tile, tile, pl.BlockSpec((n_experts, LANE), lambda i: (0, 0))],
        out_shape=[jax.ShapeDtypeStruct((8, n), jnp.int32), jax.ShapeDtypeStruct((8, n), F32),
                   jax.ShapeDtypeStruct((8, n), jnp.int32), jax.ShapeDtypeStruct((n_experts, LANE), jnp.int32)],
        scratch_shapes=[pltpu.VMEM((n_experts, LANE), F32)],
        compiler_params=pltpu.CompilerParams(dimension_semantics=("arbitrary",)),
        name="route_top2_rank",
    )(logits_t)


def _dispatch_kernel(pos_ref, starts_ref, ends_ref, h_ref, x_hbm, zbuf, sem, zsem,
                     *, n_tok, n_experts, tm_e, rt):
    i = pl.program_id(0)
    tm = h_ref.shape[0] // rt
    base = i * tm
    span = tm_e * rt

    @pl.when(i == 0)
    def _():
        zbuf[...] = jnp.zeros_like(zbuf)

        def tail_copy(e):
            start = pl.multiple_of((ends_ref[e] - tm_e) * rt, span)
            return pltpu.make_async_copy(zbuf, x_hbm.at[pl.ds(start, span), :], zsem)

        def fill(e, c):
            @pl.when(ends_ref[e] > starts_ref[e])
            def _():
                tail_copy(e).start()
            return c

        def fill_wait(e, c):
            @pl.when(ends_ref[e] > starts_ref[e])
            def _():
                tail_copy(e).wait()
            return c

        def free_copy(t):
            return pltpu.make_async_copy(zbuf, x_hbm.at[pl.ds(pl.multiple_of(t * span, span), span), :], zsem)

        def free_fill(t, c):
            free_copy(t).start()
            return c

        def free_wait(t, c):
            free_copy(t).wait()
            return c

        first_free = ends_ref[n_experts - 1] // tm_e
        n_tiles = x_hbm.shape[0] // span
        lax.fori_loop(0, n_experts, fill, 0)
        lax.fori_loop(first_free, n_tiles, free_fill, 0)
        lax.fori_loop(0, n_experts, fill_wait, 0)
        lax.fori_loop(first_free, n_tiles, free_wait, 0)

    def row_copy(r, slot):
        p = pl.multiple_of(pos_ref[slot * n_tok + base + r], rt)
        src = h_ref.at[pl.ds(pl.multiple_of(r * rt, rt), rt), :]
        return pltpu.make_async_copy(src, x_hbm.at[pl.ds(p, rt), :], sem)

    def issue(r, c):
        row_copy(r, 0).start()
        row_copy(r, 1).start()
        return c

    def drain(r, c):
        row_copy(r, 0).wait()
        row_copy(r, 1).wait()
        return c

    lax.fori_loop(0, tm, issue, 0, unroll=DMA_UNROLL)
    lax.fori_loop(0, tm, drain, 0, unroll=DMA_UNROLL)


def _dispatch(h2, pos, starts, ends, n_rows, tm_e, rt):
    n_tok = h2.shape[0] // rt
    tm = min(n_tok, 512)
    grid_spec = pltpu.PrefetchScalarGridSpec(
        num_scalar_prefetch=3,
        grid=(n_tok // tm,),
        in_specs=[pl.BlockSpec((tm * rt, LANE), lambda i, *_: (i, 0))],
        out_specs=pl.BlockSpec(memory_space=pl.ANY),
        scratch_shapes=[pltpu.VMEM((tm_e * rt, LANE), F32), pltpu.SemaphoreType.DMA(()),
                        pltpu.SemaphoreType.DMA(())],
    )
    return pl.pallas_call(
        functools.partial(_dispatch_kernel, n_tok=n_tok, n_experts=starts.shape[0], tm_e=tm_e, rt=rt),
        grid_spec=grid_spec,
        out_shape=jax.ShapeDtypeStruct((n_rows * rt, LANE), F32),
        compiler_params=pltpu.CompilerParams(dimension_semantics=("arbitrary",), vmem_limit_bytes=VMEM_LIMIT),
        name="dispatch_rows",
    )(pos, starts, ends, h2)


def _expert_kernel(tile_e_ref, used_ref, x_ref, wg_ref, wu_ref, wd_ref, y_ref, wg_b, wu_b, wd_b):
    i = pl.program_id(0)
    used = i < used_ref[0]
    fresh = jnp.logical_or(i == 0, tile_e_ref[i] != tile_e_ref[jnp.maximum(i - 1, 0)])

    @pl.when(jnp.logical_and(used, fresh))
    def _():
        wg_b[...] = wg_ref[0].astype(BF16)
        wu_b[...] = wu_ref[0].astype(BF16)
        wd_b[...] = wd_ref[0].astype(BF16)

    @pl.when(used)
    def _():
        rt = wg_b.shape[0] // LANE
        tm = x_ref.shape[0] // rt
        xb = _load_row_tiles(x_ref, (), 0, tm, rt).astype(BF16)
        a = _dot(xb, wg_b[...])
        b = _dot(xb, wu_b[...])
        _store_row_tiles(y_ref, (), 0, _dot((a * jax.nn.sigmoid(a) * b).astype(BF16), wd_b[...]))

    @pl.when(jnp.logical_not(used))
    def _():
        y_ref[...] = jnp.zeros_like(y_ref)


def _experts(x_sorted, tile_expert, n_used, w_gate, w_up, w_down, tm):
    d, d_e = w_gate.shape[1], w_gate.shape[2]
    rt = d // LANE
    n_rows = x_sorted.shape[0] // rt
    last = lambda i, used: jnp.minimum(i, used[0] - 1)
    grid_spec = pltpu.PrefetchScalarGridSpec(
        num_scalar_prefetch=2,
        grid=(n_rows // tm,),
        in_specs=[pl.BlockSpec((tm * rt, LANE), lambda i, te, used: (last(i, used), 0)),
                  pl.BlockSpec((1, d, d_e), lambda i, te, used: (te[i], 0, 0)),
                  pl.BlockSpec((1, d, d_e), lambda i, te, used: (te[i], 0, 0)),
                  pl.BlockSpec((1, d_e, d), lambda i, te, used: (te[i], 0, 0))],
        out_specs=pl.BlockSpec((tm * rt, LANE), lambda i, te, used: (i, 0)),
        scratch_shapes=[pltpu.VMEM((d, d_e), BF16), pltpu.VMEM((d, d_e), BF16), pltpu.VMEM((d_e, d), BF16)],
    )
    return pl.pallas_call(
        _expert_kernel,
        grid_spec=grid_spec,
        out_shape=jax.ShapeDtypeStruct((n_rows * rt, LANE), F32),
        compiler_params=pltpu.CompilerParams(dimension_semantics=("arbitrary",), vmem_limit_bytes=VMEM_LIMIT),
        name="routed_experts",
    )(tile_expert, n_used, x_sorted, w_gate, w_up, w_down)


def _final_kernel(pos_ref, x1_ref, w_ref, mod_ref, ln_ref, y_hbm, o_ref, ybuf, sems, *, alpha, n_tok):
    i = pl.program_id(0)
    n_steps = pl.num_programs(0)
    tm, d = x1_ref.shape
    rt = d // LANE

    def row_copy(step, r, slot):
        buf = step % 2
        p = pl.multiple_of(pos_ref[slot * n_tok + step * tm + r], rt)
        dst = ybuf.at[2 * buf + slot, pl.ds(pl.multiple_of(r * rt, rt), rt), :]
        return pltpu.make_async_copy(y_hbm.at[pl.ds(p, rt), :], dst, sems.at[buf])

    def issue_tile(step):
        def issue(r, c):
            row_copy(step, r, 0).start()
            row_copy(step, r, 1).start()
            return c
        lax.fori_loop(0, tm, issue, 0, unroll=DMA_UNROLL)

    @pl.when(i == 0)
    def _():
        issue_tile(i)

    @pl.when(i + 1 < n_steps)
    def _():
        issue_tile(i + 1)

    def drain(r, c):
        row_copy(i, r, 0).wait()
        row_copy(i, r, 1).wait()
        return c

    lax.fori_loop(0, tm, drain, 0, unroll=DMA_UNROLL)
    cur = 2 * (i % 2)
    moe = (w_ref[:, 0:1] * _load_row_tiles(ybuf, (cur,), 0, tm, rt)
           + w_ref[:, 1:2] * _load_row_tiles(ybuf, (cur + 1,), 0, tm, rt))
    z = alpha * x1_ref[...] + mod_ref[0] * moe
    o_ref[...] = _layer_norm(z) * ln_ref[0:1, :] + ln_ref[1:2, :]


def _final(x1, pos, wts, y_sorted, g2, ln, alpha, seq):
    n_tok, d = x1.shape
    rt = d // LANE
    tm = min(seq, 256)
    per_b = seq // tm
    grid_spec = pltpu.PrefetchScalarGridSpec(
        num_scalar_prefetch=1,
        grid=(n_tok // tm,),
        in_specs=[pl.BlockSpec((tm, d), lambda i, *_: (i, 0)),
                  pl.BlockSpec((tm, 2), lambda i, *_: (i, 0)),
                  pl.BlockSpec((1, 1, d), lambda i, *_: (i // per_b, 0, 0)),
                  pl.BlockSpec((2, d), lambda i, *_: (0, 0)),
                  pl.BlockSpec(memory_space=pl.ANY)],
        out_specs=pl.BlockSpec((tm, d), lambda i, *_: (i, 0)),
        scratch_shapes=[pltpu.VMEM((4, tm * rt, LANE), F32), pltpu.SemaphoreType.DMA((2,))],
    )
    return pl.pallas_call(
        functools.partial(_final_kernel, alpha=alpha, n_tok=n_tok),
        grid_spec=grid_spec,
        out_shape=jax.ShapeDtypeStruct((n_tok, d), F32),
        compiler_params=pltpu.CompilerParams(dimension_semantics=("arbitrary",), vmem_limit_bytes=VMEM_LIMIT),
        name="combine_postnorm",
    )(pos, x1, wts, g2, ln, y_sorted)


def _tile_plan(counts, n_rows, tm):
    n_experts = counts.shape[0]
    padded = ((counts + tm - 1) // tm) * tm
    ends = jnp.cumsum(padded).astype(jnp.int32)
    starts = ends - padded
    tile_start = jnp.arange(n_rows // tm, dtype=jnp.int32) * tm
    tile_e = jnp.sum((tile_start[:, None] >= ends[None, :]).astype(jnp.int32), axis=1)
    n_used = ends[-1:] // tm
    tile_e = jnp.minimum(tile_e, jnp.max(jnp.where(counts > 0, jnp.arange(n_experts, dtype=jnp.int32), 0)))
    return starts, ends, tile_e, n_used


def kernel(x, c, ctx, c_ctx, w_mod, b_mod, w_in, s5_a_re, s5_a_im, s5_log_dt, s5_b_re, s5_b_im, s5_c_re, s5_c_im, s5_d, w_glu_a, w_glu_b, q_gain, k_gain, w_attn_o, w_out, ln1_g, ln1_b, w_router_group, b_router_group, w_router_expert, b_router_expert, w_exp_gate, w_exp_up, w_exp_down, ln2_g, ln2_b):
    bsz, n_lat, d = x.shape
    n_ctx = ctx.shape[1]
    assert w_mod.shape[0] == DEPTH == 1 and bsz + 1 <= 8
    alpha = (2.0 * DEPTH) ** 0.25
    d_s5 = s5_d.shape[1]
    d_q = w_attn_o.shape[1]
    d_kv = N_KV_HEADS * HEAD_DIM
    n_experts = w_exp_gate.shape[1]

    cond = jnp.concatenate([c, c_ctx[None], jnp.zeros((8 - bsz - 1, d), F32)], axis=0)
    mod = _adaln(cond, w_mod[0], b_mod[0]).reshape(8, 6, d)
    mod_lat = mod[:bsz]
    mod_ctx = jnp.broadcast_to(mod[bsz:bsz + 1], (bsz, 6, d))

    w_in_b = w_in[0].astype(BF16)
    cos, sin = _rope_tables(n_lat)
    qg, kg = q_gain[0].reshape(1, HEAD_DIM), k_gain[0].reshape(1, HEAD_DIM)
    dims = dict(d_s5=d_s5, d_q=d_q, d_kv=d_kv)
    u, q, k, v, gates = _inproj(x, mod_lat[:, 0:2], w_in_b, cos, sin, qg, kg, latent=True, **dims)
    uc, kc, vc = _inproj(ctx, mod_ctx[:, 0:2], w_in_b, cos[:n_ctx], sin[:n_ctx], qg, kg, latent=False, **dims)

    w_t, w_z, w_c, a_chunk = _s5_weights(s5_a_re[0], s5_a_im[0], s5_log_dt[0], s5_b_re[0], s5_b_im[0],
                                         s5_c_re[0], s5_c_im[0])
    y_ssm = _s5_branch(u, uc, w_t, w_z, w_c, a_chunk, s5_d[0])

    o = _attention(q, k, v, kc, vc)

    n_r = 8 + n_experts
    n_r_pad = ((n_r + LANE - 1) // LANE) * LANE
    pad_cols = lambda g, e: jnp.concatenate(
        [g, jnp.zeros(g.shape[:-1] + (8 - N_EXPERT_GROUPS,), F32), e,
         jnp.zeros(g.shape[:-1] + (n_r_pad - n_r,), F32)], axis=-1)
    w_r = pad_cols(w_router_group[0], w_router_expert[0])
    b_r = pad_cols(b_router_group[0][None], b_router_expert[0][None])
    wr_hi, wr_lo = _split_bf16(w_r)
    ln1 = jnp.stack([ln1_g[0], ln1_b[0]])
    x1, h2, logits = _merge(x, y_ssm, o, gates, mod_lat[:, 2:5], w_glu_a[0].astype(BF16),
                            w_glu_b[0].astype(BF16), w_attn_o[0].astype(BF16), w_out[0].astype(BF16),
                            ln1, wr_hi, wr_lo, b_r, alpha)

    n_tok = bsz * n_lat
    tm_e = EXPERT_TILE
    n_rows = 2 * n_tok + n_experts * tm_e
    logits_t = logits.reshape(n_tok, n_r_pad)[:, :n_r].T
    eid, wts, rank, counts = _route(logits_t, n_experts)
    starts, ends, tile_e, n_used = _tile_plan(counts[:, 0], n_rows, tm_e)
    own = eid[:2, :, None] == jnp.arange(n_experts, dtype=jnp.int32)
    rt = d // LANE
    pos = ((jnp.sum(jnp.where(own, starts, 0), axis=-1) + rank[:2]) * rt).reshape(-1)
    x_sorted = _dispatch(h2.reshape(n_tok * rt, LANE), pos, starts, ends, n_rows, tm_e, rt)
    y_sorted = _experts(x_sorted, tile_e, n_used, w_exp_gate[0], w_exp_up[0], w_exp_down[0], tm_e)
    ln2 = jnp.stack([ln2_g[0], ln2_b[0]])
    out = _final(x1.reshape(n_tok, d), pos, wts[:2].T, y_sorted, mod_lat[:, 5:6], ln2, alpha, n_lat)
    return out.reshape(bsz, n_lat, d)
```

```python
import functools
import math

import jax
import jax.numpy as jnp
from jax import lax
from jax.experimental import pallas as pl
from jax.experimental.pallas import tpu as pltpu

GRID_W = 64
S5_GROUP_CH = 16
S5_STATE = 64
HEAD_DIM = 128
N_KV_HEADS = 2
ROPE_THETA = 10000.0
N_EXPERT_GROUPS = 4
EXPERTS_PER_GROUP = 8
NORM_EPS = 1e-6
DEPTH = 1

S5_CHUNK = 16
S5_GROUP_BLOCK = 8
EXPERT_TILE = 256
DMA_UNROLL = 8
ATTN_Q_TILE = 512
ATTN_KV_CHUNK = 3072
ROW_TILE = 1024
INPROJ_SUB_ROWS = 256
MERGE_SUB_ROWS = 256
LANE = 128
VMEM_LIMIT = 56 * 1024 * 1024

F32 = jnp.float32
BF16 = jnp.bfloat16


def _layer_norm(x):
    mu = jnp.mean(x, axis=-1, keepdims=True)
    xc = x - mu
    var = jnp.mean(xc * xc, axis=-1, keepdims=True)
    return xc * lax.rsqrt(var + NORM_EPS)


def _split_bf16(a):
    hi = a.astype(BF16)
    lo = (a - hi.astype(F32)).astype(BF16)
    return hi, lo


def _dot(a, b):
    return jnp.dot(a, b, preferred_element_type=F32)


def _store_row_tiles(ref, lead, r0, val):
    n, d = val.shape
    rt = d // LANE
    for s in range(rt):
        ref[lead + (pl.ds(r0 * rt + s, n, stride=rt), slice(None))] = val[:, s * LANE:(s + 1) * LANE]


def _load_row_tiles(ref, lead, r0, n, rt):
    return jnp.concatenate([ref[lead + (pl.ds(r0 * rt + s, n, stride=rt), slice(None))] for s in range(rt)],
                           axis=1)


def _dot3(a_hi, a_lo, b_hi, b_lo):
    return _dot(a_hi, b_hi) + _dot(a_hi, b_lo) + _dot(a_lo, b_hi)


def _adaln_kernel(c_ref, w_ref, b_ref, o_ref):
    c = c_ref[...]
    s = c * jax.nn.sigmoid(c)
    s_hi, s_lo = _split_bf16(s)
    w_hi, w_lo = _split_bf16(w_ref[...])
    o_ref[...] = _dot3(s_hi, s_lo, w_hi, w_lo) + b_ref[...]


def _adaln(cond, w, b):
    rows, d = cond.shape
    n = w.shape[1]
    tn = min(n, 1024)
    return pl.pallas_call(
        _adaln_kernel,
        grid=(n // tn,),
        in_specs=[pl.BlockSpec((rows, d), lambda j: (0, 0)),
                  pl.BlockSpec((d, tn), lambda j: (0, j)),
                  pl.BlockSpec((1, tn), lambda j: (0, j))],
        out_specs=pl.BlockSpec((rows, tn), lambda j: (0, j)),
        out_shape=jax.ShapeDtypeStruct((rows, n), F32),
        compiler_params=pltpu.CompilerParams(vmem_limit_bytes=VMEM_LIMIT),
        name="adaln",
    )(cond, w, b.reshape(1, n))


def _rms_rope(t, gain, cos, sin_signed, first_half, scale):
    r = lax.rsqrt(jnp.mean(t * t, axis=-1, keepdims=True) + NORM_EPS)
    tn = t * r * gain
    if cos is not None:
        partner = jnp.where(first_half, pltpu.roll(tn, HEAD_DIM - 32, 1), pltpu.roll(tn, 32, 1))
        tn = tn * cos + partner * sin_signed
    if scale != 1.0:
        tn = tn * scale
    return tn


def _inproj_kernel(x_ref, mod_ref, w_ref, cos_ref, sin_ref, qg_ref, kg_ref, *out_refs,
                   d_s5, d_q, d_kv, latent):
    shift = mod_ref[0, 0:1, :]
    scale = mod_ref[0, 1:2, :]
    o_q = d_s5
    o_k = o_q + d_q
    o_v = o_k + d_kv
    o_g = o_v + d_kv
    tm = x_ref.shape[1]
    sub = min(tm, INPROJ_SUB_ROWS)
    for r0 in range(0, tm, sub):
        rows = slice(r0, r0 + sub)
        h = (_layer_norm(x_ref[0, rows, :]) * (1.0 + scale) + shift).astype(BF16)
        if latent:
            u_ref, q_ref, k_ref, v_ref, g_ref = out_refs
            cos = cos_ref[rows, :]
            sin = sin_ref[rows, :]
            lane = lax.broadcasted_iota(jnp.int32, cos.shape, 1)
            first_half = (lane % 64) < 32
        else:
            u_ref, k_ref, v_ref = out_refs
            cos = sin = first_half = None
        k = _dot(h, w_ref[:, o_k:o_v])
        if latent:
            q = _dot(h, w_ref[:, o_q:o_k])
        for hd in range(d_kv // HEAD_DIM):
            sl = slice(hd * HEAD_DIM, (hd + 1) * HEAD_DIM)
            k_ref[0, rows, sl] = _rms_rope(k[:, sl], kg_ref[...], cos, sin, first_half, 1.0).astype(BF16)
        u_ref[0, rows, :] = _dot(h, w_ref[:, 0:d_s5])
        v_ref[0, rows, :] = _dot(h, w_ref[:, o_v:o_g]).astype(BF16)
        if latent:
            d_half = (w_ref.shape[1] - o_g) // 2
            g_lo = _dot(h, w_ref[:, o_g:o_g + d_half])
            q_scale = HEAD_DIM ** -0.5 * math.log2(math.e)
            n_qh = d_q // HEAD_DIM
            for hd in range(n_qh // 2):
                sl = slice(hd * HEAD_DIM, (hd + 1) * HEAD_DIM)
                q_ref[0, rows, sl] = _rms_rope(q[:, sl], qg_ref[...], cos, sin, first_half, q_scale).astype(BF16)
            g_hi = _dot(h, w_ref[:, o_g + d_half:])
            for hd in range(n_qh // 2, n_qh):
                sl = slice(hd * HEAD_DIM, (hd + 1) * HEAD_DIM)
                q_ref[0, rows, sl] = _rms_rope(q[:, sl], qg_ref[...], cos, sin, first_half, q_scale).astype(BF16)
            g_ref[0, rows, 0:d_half] = jax.nn.sigmoid(g_lo).astype(BF16)
            g_ref[0, rows, d_half:] = jax.nn.sigmoid(g_hi).astype(BF16)


def _inproj(x, mod, w_in, cos, sin, q_gain, k_gain, *, d_s5, d_q, d_kv, latent):
    bsz, n, d = x.shape
    n_in = w_in.shape[1]
    d_gate = n_in - d_s5 - d_q - 2 * d_kv
    tm = min(n, ROW_TILE)
    row = lambda w: pl.BlockSpec((1, tm, w), lambda b, i: (b, i, 0))
    out_shape = [jax.ShapeDtypeStruct((bsz, n, d_s5), F32)]
    out_specs = [row(d_s5)]
    if latent:
        out_shape.append(jax.ShapeDtypeStruct((bsz, n, d_q), BF16))
        out_specs.append(row(d_q))
    out_shape += [jax.ShapeDtypeStruct((bsz, n, d_kv), BF16)] * 2
    out_specs += [row(d_kv)] * 2
    if latent:
        out_shape.append(jax.ShapeDtypeStruct((bsz, n, d_gate), BF16))
        out_specs.append(row(d_gate))
    return pl.pallas_call(
        functools.partial(_inproj_kernel, d_s5=d_s5, d_q=d_q, d_kv=d_kv, latent=latent),
        grid=(bsz, n // tm),
        in_specs=[row(d),
                  pl.BlockSpec((1, 2, d), lambda b, i: (b, 0, 0)),
                  pl.BlockSpec((d, n_in), lambda b, i: (0, 0), pipeline_mode=pl.Buffered(1)),
                  pl.BlockSpec((tm, HEAD_DIM), lambda b, i: (i, 0)),
                  pl.BlockSpec((tm, HEAD_DIM), lambda b, i: (i, 0)),
                  pl.BlockSpec((1, HEAD_DIM), lambda b, i: (0, 0)),
                  pl.BlockSpec((1, HEAD_DIM), lambda b, i: (0, 0))],
        out_specs=out_specs,
        out_shape=out_shape,
        compiler_params=pltpu.CompilerParams(vmem_limit_bytes=VMEM_LIMIT),
        name="inproj_latent" if latent else "inproj_context",
    )(x, mod, w_in, cos, sin, q_gain, k_gain)


def _rope_tables(n_lat):
    rows = n_lat // GRID_W
    axis_dim = HEAD_DIM // 2
    inv = ROPE_THETA ** (-jnp.arange(0, axis_dim, 2, dtype=F32) / axis_dim)
    ang_r = jnp.arange(rows, dtype=F32)[:, None] * inv
    ang_c = jnp.arange(GRID_W, dtype=F32)[:, None] * inv
    per_row = lambda t: jnp.repeat(t, GRID_W, axis=0)
    per_col = lambda t: jnp.tile(t, (rows, 1))
    cos_r, sin_r = per_row(jnp.cos(ang_r)), per_row(jnp.sin(ang_r))
    cos_c, sin_c = per_col(jnp.cos(ang_c)), per_col(jnp.sin(ang_c))
    cos = jnp.concatenate([cos_r, cos_r, cos_c, cos_c], axis=1)
    sin = jnp.concatenate([-sin_r, sin_r, -sin_c, sin_c], axis=1)
    return cos, sin


def _s5_weights(a_re, a_im, log_dt, b_re, b_im, c_re, c_im):
    hp = lax.Precision.HIGHEST
    lc = S5_CHUNK
    dt = jnp.exp(log_dt)[..., None]
    lam_re, lam_im = a_re * dt, a_im * dt
    ea = jnp.exp(lam_re)
    ab_re, ab_im = ea * jnp.cos(lam_im), ea * jnp.sin(lam_im)
    den = a_re * a_re + a_im * a_im
    nr, ni = ab_re - 1.0, ab_im
    rr = (nr * a_re + ni * a_im) / den
    ri = (ni * a_re - nr * a_im) / den
    bb_re = rr[..., None] * b_re - ri[..., None] * b_im
    bb_im = rr[..., None] * b_im + ri[..., None] * b_re
    kk = jnp.arange(lc + 1, dtype=F32)[:, None, None, None]
    pk_mag = jnp.exp(kk * lam_re)
    pk_re, pk_im = pk_mag * jnp.cos(kk * lam_im), pk_mag * jnp.sin(kk * lam_im)
    n_g = a_re.shape[1]
    kw = lc * S5_GROUP_CH
    rows = lambda t: jnp.moveaxis(t, 0, 1).reshape(n_g, kw, t.shape[-1])
    bt_re, bt_im = jnp.swapaxes(bb_re, -1, -2), jnp.swapaxes(bb_im, -1, -2)
    pw_re, pw_im = pk_re[:lc, :, :, None, :], pk_im[:lc, :, :, None, :]
    akb_re = pw_re * bt_re - pw_im * bt_im
    akb_im = pw_re * bt_im + pw_im * bt_re
    n_lag = 2 * lc - 1
    ch = S5_GROUP_CH
    gp = lambda t: jnp.moveaxis(t, 0, -1)
    lag_pw = lambda t: jnp.stack([jnp.pad(gp(t[:lc, 0]), ((0, 0), (0, 0), (lc - 1, 0))),
                                  jnp.pad(gp(t[lc - 1::-1, 1]), ((0, 0), (0, 0), (0, lc - 1)))])
    rep_j = jnp.repeat(jnp.eye(n_lag, dtype=F32), ch, axis=1)
    til_c = jnp.tile(jnp.eye(ch, dtype=F32), (1, n_lag))
    pkx_re = jnp.einsum('dgpj,jm->dgpm', lag_pw(pk_re), rep_j, precision=hp)
    pkx_im = jnp.einsum('dgpj,jm->dgpm', lag_pw(pk_im), rep_j, precision=hp)
    cx_re = jnp.einsum('dgop,om->dgpm', c_re, til_c, precision=hp)
    cx_im = jnp.einsum('dgop,om->dgpm', c_im, til_c, precision=hp)
    cax_re = cx_re * pkx_re - cx_im * pkx_im
    cax_im = cx_re * pkx_im + cx_im * pkx_re
    strip = (jnp.einsum('dgip,dgpm->gim', bt_re, cax_re, precision=hp)
             - jnp.einsum('dgip,dgpm->gim', bt_im, cax_im, precision=hp))
    w_t = jnp.concatenate([strip[:, :, ch * (lc - 1 - s):ch * (lc - 1 - s) + kw] for s in range(lc)], axis=1)
    w_z = jnp.concatenate([rows(akb_re[::-1, 0]), rows(akb_re[:, 1]),
                           rows(akb_im[::-1, 0]), rows(akb_im[:, 1])], axis=2)
    pf_re, pf_im = pk_re[1:, 0], pk_im[1:, 0]
    pr_re, pr_im = pk_re[lc:0:-1, 1], pk_im[lc:0:-1, 1]

    def cpow(cr, ci, pr, pi):
        return cr[None] * pr[:, :, None, :] - ci[None] * pi[:, :, None, :], \
               cr[None] * pi[:, :, None, :] + ci[None] * pr[:, :, None, :]

    cf_re, cf_im = cpow(c_re[0], c_im[0], pf_re, pf_im)
    cr_re, cr_im = cpow(c_re[1], c_im[1], pr_re, pr_im)
    w_c = jnp.concatenate([rows(cf_re), rows(cr_re), rows(-cf_im), rows(-cr_im)], axis=2)
    w_c = jnp.swapaxes(w_c, 1, 2)
    a_chunk = jnp.concatenate([pk_re[lc, 0], pk_re[lc, 1], pk_im[lc, 0], pk_im[lc, 1]], axis=1)
    return w_t.astype(BF16), w_z.astype(BF16), w_c.astype(BF16), a_chunk


def _regroup_rows(n_rows):
    return min(n_rows, 64)


def _s5_kernel(ul_ref, uc_ref, wt_ref, wz_ref, wc_ref, a_ref, d_ref, y_ref,
               lhs_l, lhs_c, zl_re, zl_im, zc_re, zc_im, yg_ref, *, nb, ncl, ncc):
    gb = S5_GROUP_BLOCK
    lc, ch = S5_CHUNK, S5_GROUP_CH
    per_tile = LANE // ch
    nl = nb * ncl
    ncx = nb * ncc
    half = 2 * S5_STATE
    lane_blk = lambda rows: lax.broadcasted_iota(jnp.int32, (rows, LANE), 1) // ch

    def block_transpose(arrs, blk):
        n = len(arrs)
        rolled = []
        for k in range(n):
            w = arrs[k]
            for g in range(1, n):
                w = jnp.where(blk == g, arrs[(g + k) % n], w)
            rolled.append(w if k == 0 else pltpu.roll(w, ch * k, 1))
        outs = []
        for b in range(n):
            o = rolled[0]
            for k in range(1, n):
                o = jnp.where(blk == (b + k) % n, rolled[k], o)
            outs.append(o)
        return outs

    def gather_chunks(src_ref, dst_ref, n_chunks):
        rb = _regroup_rows(n_chunks)
        blk = lane_blk(rb)

        def step(i, carry):
            r0 = pl.multiple_of(i * rb, rb)
            for hh in range(lc // per_tile):
                ut = [src_ref[pl.ds(r0 * lc + hh * per_tile + j, rb, stride=lc), :] for j in range(per_tile)]
                for g, out in enumerate(block_transpose(ut, blk)):
                    dst_ref[g, pl.ds(r0, rb), hh * LANE:(hh + 1) * LANE] = out.astype(BF16)
            return carry

        lax.fori_loop(0, n_chunks // rb, step, 0)

    gather_chunks(ul_ref, lhs_l, nl)
    gather_chunks(uc_ref, lhs_c, ncx)

    for g in range(gb):
        zl = _dot(lhs_l[g], wz_ref[g])
        zl_re[pl.ds(g, nl, stride=gb), :] = zl[:, :half]
        zl_im[pl.ds(g, nl, stride=gb), :] = zl[:, half:]
        zc = _dot(lhs_c[g], wz_ref[g])
        zc_re[pl.ds(g, ncx, stride=gb), :] = zc[:, :half]
        zc_im[pl.ds(g, ncx, stride=gb), :] = zc[:, half:]
    a_re = a_ref[:, :half]
    a_im = a_ref[:, half:]
    fwd = lax.broadcasted_iota(jnp.int32, (gb, half), 1) < S5_STATE

    def advance(h_re, h_im, z_re, z_im):
        return a_re * h_re - a_im * h_im + z_re, a_re * h_im + a_im * h_re + z_im

    def ctx_step(i, carry):
        out = []
        for b in range(nb):
            sl_f = pl.ds(pl.multiple_of((b * ncc + i) * gb, gb), gb)
            sl_r = pl.ds(pl.multiple_of((b * ncc + ncc - 1 - i) * gb, gb), gb)
            z_re = jnp.where(fwd, zc_re[sl_f, :], zc_re[sl_r, :])
            z_im = jnp.where(fwd, zc_im[sl_f, :], zc_im[sl_r, :])
            out.extend(advance(carry[2 * b], carry[2 * b + 1], z_re, z_im))
        return tuple(out)

    def lat_step(i, carry):
        out = []
        for b in range(nb):
            h_re, h_im = carry[2 * b], carry[2 * b + 1]
            sl_f = pl.ds(pl.multiple_of((b * ncl + i) * gb, gb), gb)
            sl_r = pl.ds(pl.multiple_of((b * ncl + ncl - 1 - i) * gb, gb), gb)
            f_re, f_im, r_re, r_im = zl_re[sl_f, :], zl_im[sl_f, :], zl_re[sl_r, :], zl_im[sl_r, :]
            zl_re[sl_f, :] = jnp.where(fwd, h_re, f_re)
            zl_im[sl_f, :] = jnp.where(fwd, h_im, f_im)
            zl_re[sl_r, :] = jnp.where(fwd, r_re, h_re)
            zl_im[sl_r, :] = jnp.where(fwd, r_im, h_im)
            out.extend(advance(h_re, h_im, jnp.where(fwd, f_re, r_re), jnp.where(fwd, f_im, r_im)))
        return tuple(out)

    zero = jnp.zeros((gb, half), F32)
    carry = lax.fori_loop(0, ncc, ctx_step, (zero,) * (2 * nb))
    lax.fori_loop(0, ncl, lat_step, carry)
    for g in range(gb):
        rows = pl.ds(g, nl, stride=gb)
        h_in = jnp.concatenate([zl_re[rows, :], zl_im[rows, :]], axis=1).astype(BF16)
        yg_ref[g] = _dot(lhs_l[g], wt_ref[g]) + _dot(h_in, wc_ref[g])

    rb = _regroup_rows(nl)
    blk = lane_blk(rb)
    d_row = d_ref[0]

    def scatter_step(i, carry):
        r0 = pl.multiple_of(i * rb, rb)
        for hh in range(lc // per_tile):
            yt = [yg_ref[g, pl.ds(r0, rb), hh * LANE:(hh + 1) * LANE] for g in range(gb)]
            for j, out in enumerate(block_transpose(yt, blk)):
                rows = pl.ds(r0 * lc + hh * per_tile + j, rb, stride=lc)
                y_ref[rows, :] = out + ul_ref[rows, :] * d_row
        return carry

    lax.fori_loop(0, nl // rb, scatter_step, 0)


def _s5_branch(u, uc, w_t, w_z, w_c, a_chunk, s5_d):
    bsz, n, width = u.shape
    n_ctx = uc.shape[1]
    lc, ch = S5_CHUNK, S5_GROUP_CH
    n_g = width // ch
    ncl, ncc = n // lc, n_ctx // lc
    gb = S5_GROUP_BLOCK
    assert ncl % 2 == 0 and n_g % gb == 0 and gb * ch == LANE
    kw = lc * ch
    nl, ncx = bsz * ncl, bsz * ncc
    assert nl % _regroup_rows(nl) == 0 and ncx % _regroup_rows(ncx) == 0
    once = pl.Buffered(1)
    slab = lambda rows: pl.BlockSpec((rows, LANE), lambda i: (0, i), pipeline_mode=once)
    blk3 = lambda r, c: pl.BlockSpec((gb, r, c), lambda i: (i, 0, 0))
    half = 2 * S5_STATE
    y = pl.pallas_call(
        functools.partial(_s5_kernel, nb=bsz, ncl=ncl, ncc=ncc),
        grid=(n_g // gb,),
        in_specs=[slab(bsz * n), slab(bsz * n_ctx), blk3(kw, kw), blk3(kw, 2 * half), blk3(2 * half, kw),
                  pl.BlockSpec((gb, 2 * half), lambda i: (i, 0)),
                  pl.BlockSpec((1, 1, LANE), lambda i: (i, 0, 0))],
        out_specs=slab(bsz * n),
        out_shape=jax.ShapeDtypeStruct((bsz * n, width), F32),
        scratch_shapes=[pltpu.VMEM((gb, nl, kw), BF16), pltpu.VMEM((gb, ncx, kw), BF16),
                        pltpu.VMEM((gb * nl, half), F32), pltpu.VMEM((gb * nl, half), F32),
                        pltpu.VMEM((gb * ncx, half), F32), pltpu.VMEM((gb * ncx, half), F32),
                        pltpu.VMEM((gb, nl, kw), F32)],
        compiler_params=pltpu.CompilerParams(vmem_limit_bytes=VMEM_LIMIT),
        name="s5_chunked_scan",
    )(u.reshape(bsz * n, width), uc.reshape(bsz * n_ctx, width), w_t, w_z, w_c, a_chunk,
      s5_d.reshape(n_g // gb, 1, LANE))
    return y.reshape(bsz, n, width)


def _attn_kernel(q_ref, k_ref, v_ref, kc_ref, vc_ref, o_ref, k_all, v_all, *, tk, group):
    tq = q_ref.shape[1]
    n_lat, n_ctx = k_ref.shape[1], kc_ref.shape[1]
    n_k = n_lat + n_ctx

    @pl.when(pl.program_id(2) == 0)
    def _():
        k_all[0:n_lat, :] = k_ref[0]
        k_all[n_lat:n_k, :] = kc_ref[0]
        v_all[0:n_lat, 0:HEAD_DIM] = v_ref[0]
        v_all[n_lat:n_k, 0:HEAD_DIM] = vc_ref[0]
        v_all[:, HEAD_DIM:] = jnp.ones((n_k, HEAD_DIM), BF16)

    qs = [q_ref[0, :, h * HEAD_DIM:(h + 1) * HEAD_DIM] for h in range(group)]

    def body(c, carry):
        start = pl.multiple_of(c * tk, tk)
        ks = k_all[pl.ds(start, tk), :]
        vs = v_all[pl.ds(start, tk), :]
        out = []
        score = lambda h: lax.dot_general(qs[h], ks, (((1,), (1,)), ((), ())), preferred_element_type=F32)
        s_next = score(0)
        for h in range(group):
            m, acc = carry[2 * h:2 * h + 2]
            s = s_next
            if h + 1 < group:
                s_next = score(h + 1)
            m_new = jnp.maximum(m, jnp.max(s, axis=-1, keepdims=True))
            p = jnp.exp2(s - m_new)
            alpha = jnp.exp2(m - m_new)
            acc = alpha * acc + _dot(p.astype(BF16), vs)
            out.extend((m_new, acc))
        return tuple(out)

    init = (jnp.full((tq, 1), -jnp.inf, F32), jnp.zeros((tq, 2 * HEAD_DIM), F32)) * group
    fin = lax.fori_loop(0, n_k // tk, body, init)
    for h in range(group):
        acc = fin[2 * h + 1]
        o_ref[0, :, h * HEAD_DIM:(h + 1) * HEAD_DIM] = (acc[:, :HEAD_DIM] / acc[:, HEAD_DIM:]).astype(o_ref.dtype)


def _pick_divisor(n, pref):
    best = LANE
    for t in range(LANE, pref + 1, LANE):
        if n % t == 0:
            best = t
    return best


def _attention(q, k, v, kc, vc):
    bsz, n, dq = q.shape
    n_c, dkv = kc.shape[1], k.shape[2]
    n_kv = dkv // HEAD_DIM
    group = dq // dkv
    tq = min(n, ATTN_Q_TILE)
    tk = _pick_divisor(n + n_c, ATTN_KV_CHUNK)
    kv_spec = lambda rows: pl.BlockSpec((1, rows, HEAD_DIM), lambda b, h, i: (b, 0, h))
    return pl.pallas_call(
        functools.partial(_attn_kernel, tk=tk, group=group),
        grid=(bsz, n_kv, n // tq),
        in_specs=[pl.BlockSpec((1, tq, group * HEAD_DIM), lambda b, h, i: (b, i, h)),
                  kv_spec(n), kv_spec(n), kv_spec(n_c), kv_spec(n_c)],
        out_specs=pl.BlockSpec((1, tq, group * HEAD_DIM), lambda b, h, i: (b, i, h)),
        out_shape=jax.ShapeDtypeStruct((bsz, n, dq), BF16),
        scratch_shapes=[pltpu.VMEM((n + n_c, HEAD_DIM), BF16), pltpu.VMEM((n + n_c, 2 * HEAD_DIM), BF16)],
        compiler_params=pltpu.CompilerParams(dimension_semantics=("arbitrary",) * 3, vmem_limit_bytes=VMEM_LIMIT),
        name="gqa_flash_attention",
    )(q, k, v, kc, vc)


def _merge_kernel(x_ref, y_ref, o_ref, g_ref, mod_ref, wa_ref, wb_ref, wo_ref, wout_ref,
                  ln_ref, wrh_ref, wrl_ref, br_ref, x1_ref, h2_ref, lg_ref, *, alpha):
    d = x_ref.shape[2]
    tm = x_ref.shape[1]
    g1 = mod_ref[0, 0:1, :]
    blocks = [slice(r, r + MERGE_SUB_ROWS) for r in range(0, tm, MERGE_SUB_ROWS)]
    stage1 = []
    for rows in blocks:
        att = _dot(o_ref[0, rows, :], wo_ref[...])
        gact = jax.nn.gelu(y_ref[0, rows, :]).astype(BF16)
        stage1.append((att, _dot(gact, wa_ref[...]), _dot(gact, wb_ref[...])))
    stage2 = []
    for rows, (att, a, b) in zip(blocks, stage1):
        gate = g_ref[0, rows, :].astype(F32)
        mixed = (gate[:, :d] * (a * jax.nn.sigmoid(b)) + gate[:, d:] * att).astype(BF16)
        stage2.append(_dot(mixed, wout_ref[...]))
    for rows, mix in zip(blocks, stage2):
        x1 = _layer_norm(alpha * x_ref[0, rows, :] + g1 * mix) * ln_ref[0:1, :] + ln_ref[1:2, :]
        x1_ref[0, rows, :] = x1
        h2 = _layer_norm(x1) * (1.0 + mod_ref[0, 2:3, :]) + mod_ref[0, 1:2, :]
        _store_row_tiles(h2_ref, (0,), rows.start, h2)
        h_hi, h_lo = _split_bf16(h2)
        lg_ref[0, rows, :] = _dot3(h_hi, h_lo, wrh_ref[...], wrl_ref[...]) + br_ref[...]


def _merge(x, y_ssm, o, gates, mod, w_glu_a, w_glu_b, w_attn_o, w_out, ln, wr_hi, wr_lo, br, alpha):
    bsz, n, d = x.shape
    tm = min(n, ROW_TILE)
    row = lambda w: pl.BlockSpec((1, tm, w), lambda b, i: (b, i, 0))
    const = lambda a: pl.BlockSpec(a.shape, lambda b, i: (0,) * a.ndim)
    nr = wr_hi.shape[1]
    rt = d // LANE
    return pl.pallas_call(
        functools.partial(_merge_kernel, alpha=alpha),
        grid=(bsz, n // tm),
        in_specs=[row(d), row(y_ssm.shape[2]), row(o.shape[2]), row(gates.shape[2]),
                  pl.BlockSpec((1, 3, d), lambda b, i: (b, 0, 0)),
                  const(w_glu_a), const(w_glu_b), const(w_attn_o), const(w_out), const(ln),
                  const(wr_hi), const(wr_lo), const(br)],
        out_specs=[row(d), pl.BlockSpec((1, tm * rt, LANE), lambda b, i: (b, i, 0)), row(nr)],
        out_shape=[jax.ShapeDtypeStruct((bsz, n, d), F32), jax.ShapeDtypeStruct((bsz, n * rt, LANE), F32),
                   jax.ShapeDtypeStruct((bsz, n, nr), F32)],
        compiler_params=pltpu.CompilerParams(vmem_limit_bytes=VMEM_LIMIT),
        name="merge_postnorm_router",
    )(x, y_ssm, o, gates, mod, w_glu_a, w_glu_b, w_attn_o, w_out, ln, wr_hi, wr_lo, br)


def _route_kernel(lt_ref, e_ref, w_ref, r_ref, cnt_ref, run_ref):
    i = pl.program_id(0)
    tn = lt_ref.shape[1]
    n_experts = run_ref.shape[0]

    @pl.when(i == 0)
    def _():
        run_ref[...] = jnp.zeros_like(run_ref)

    gl = lt_ref[0:N_EXPERT_GROUPS, :]
    gmax = jnp.max(gl, axis=0, keepdims=True)
    gi = lax.broadcasted_iota(jnp.int32, gl.shape, 0)
    gidx = jnp.min(jnp.where(gl == gmax, gi, N_EXPERT_GROUPS), axis=0, keepdims=True)
    gw = 1.0 / jnp.sum(jnp.exp(gl - gmax), axis=0, keepdims=True)
    epg = EXPERTS_PER_GROUP
    e_in = lt_ref[8:8 + epg, :]
    for g in range(1, N_EXPERT_GROUPS):
        e_in = jnp.where(gidx == g, lt_ref[8 + g * epg:8 + (g + 1) * epg, :], e_in)
    ei = lax.broadcasted_iota(jnp.int32, e_in.shape, 0)
    v0 = jnp.max(e_in, axis=0, keepdims=True)
    i0 = jnp.min(jnp.where(e_in == v0, ei, epg), axis=0, keepdims=True)
    rest = jnp.where(ei == i0, -jnp.inf, e_in)
    v1 = jnp.max(rest, axis=0, keepdims=True)
    i1 = jnp.min(jnp.where(rest == v1, ei, epg), axis=0, keepdims=True)
    t = jnp.exp(v1 - v0)
    w0 = gw / (1.0 + t)
    w1 = gw * t / (1.0 + t)
    e0 = gidx * epg + i0
    e1 = gidx * epg + i1
    zi = jnp.zeros_like(e0)
    e_ref[...] = jnp.concatenate([e0, e1] + [zi] * 6, axis=0)
    w_ref[...] = jnp.concatenate([w0, w1] + [jnp.zeros_like(w0)] * 6, axis=0)

    both = jnp.concatenate([e0, e1], axis=1)
    hit = lax.broadcasted_iota(jnp.int32, (n_experts, 2 * tn), 0) == both
    tri = (lax.broadcasted_iota(jnp.int32, (2 * tn, 2 * tn), 0)
           <= lax.broadcasted_iota(jnp.int32, (2 * tn, 2 * tn), 1))
    pref = _dot(jnp.where(hit, 1.0, 0.0).astype(BF16), jnp.where(tri, 1.0, 0.0).astype(BF16))
    run = run_ref[:, 0:1]
    rank = jnp.sum(jnp.where(hit, pref + run, 0.0), axis=0, keepdims=True) - 1.0
    rank = rank.astype(jnp.int32)
    r_ref[...] = jnp.concatenate([rank[:, :tn], rank[:, tn:]] + [zi] * 6, axis=0)
    run_new = jnp.broadcast_to(run + pref[:, 2 * tn - 1:2 * tn], run_ref.shape)
    run_ref[...] = run_new
    cnt_ref[...] = run_new.astype(jnp.int32)


def _route(logits_t, n_experts):
    rows, n = logits_t.shape
    tn = min(n, 256)
    tile = pl.BlockSpec((8, tn), lambda i: (0, i))
    return pl.pallas_call(
        _route_kernel,
        grid=(n // tn,),
        in_specs=[pl.BlockSpec((rows, tn), lambda i: (0, i))],
        out_specs=[tile, tile, tile, pl.BlockSpec((n_experts, LANE), lambda i: (0, 0))],
        out_shape=[jax.ShapeDtypeStruct((8, n), jnp.int32), jax.ShapeDtypeStruct((8, n), F32),
                   jax.ShapeDtypeStruct((8, n), jnp.int32), jax.ShapeDtypeStruct((n_experts, LANE), jnp.int32)],
        scratch_shapes=[pltpu.VMEM((n_experts, LANE), F32)],
        compiler_params=pltpu.CompilerParams(dimension_semantics=("arbitrary",)),
        name="route_top2_rank",
    )(logits_t)


def _dispatch_kernel(pos_ref, starts_ref, ends_ref, h_ref, x_hbm, zbuf, sem, zsem,
                     *, n_tok, n_experts, tm_e, rt):
    i = pl.program_id(0)
    tm = h_ref.shape[0] // rt
    base = i * tm
    span = tm_e * rt

    @pl.when(i == 0)
    def _():
        zbuf[...] = jnp.zeros_like(zbuf)

        def tail_copy(e):
            start = pl.multiple_of((ends_ref[e] - tm_e) * rt, span)
            return pltpu.make_async_copy(zbuf, x_hbm.at[pl.ds(start, span), :], zsem)

        def fill(e, c):
            @pl.when(ends_ref[e] > starts_ref[e])
            def _():
                tail_copy(e).start()
            return c

        def fill_wait(e, c):
            @pl.when(ends_ref[e] > starts_ref[e])
            def _():
                tail_copy(e).wait()
            return c

        def free_copy(t):
            return pltpu.make_async_copy(zbuf, x_hbm.at[pl.ds(pl.multiple_of(t * span, span), span), :], zsem)

        def free_fill(t, c):
            free_copy(t).start()
            return c

        def free_wait(t, c):
            free_copy(t).wait()
            return c

        first_free = ends_ref[n_experts - 1] // tm_e
        n_tiles = x_hbm.shape[0] // span
        lax.fori_loop(0, n_experts, fill, 0)
        lax.fori_loop(first_free, n_tiles, free_fill, 0)
        lax.fori_loop(0, n_experts, fill_wait, 0)
        lax.fori_loop(first_free, n_tiles, free_wait, 0)

    def row_copy(r, slot):
        p = pl.multiple_of(pos_ref[slot * n_tok + base + r], rt)
        src = h_ref.at[pl.ds(pl.multiple_of(r * rt, rt), rt), :]
        return pltpu.make_async_copy(src, x_hbm.at[pl.ds(p, rt), :], sem)

    def issue(r, c):
        row_copy(r, 0).start()
        row_copy(r, 1).start()
        return c

    def drain(r, c):
        row_copy(r, 0).wait()
        row_copy(r, 1).wait()
        return c

    lax.fori_loop(0, tm, issue, 0, unroll=DMA_UNROLL)
    lax.fori_loop(0, tm, drain, 0, unroll=DMA_UNROLL)


def _dispatch(h2, pos, starts, ends, n_rows, tm_e, rt):
    n_tok = h2.shape[0] // rt
    tm = min(n_tok, 512)
    grid_spec = pltpu.PrefetchScalarGridSpec(
        num_scalar_prefetch=3,
        grid=(n_tok // tm,),
        in_specs=[pl.BlockSpec((tm * rt, LANE), lambda i, *_: (i, 0))],
        out_specs=pl.BlockSpec(memory_space=pl.ANY),
        scratch_shapes=[pltpu.VMEM((tm_e * rt, LANE), F32), pltpu.SemaphoreType.DMA(()),
                        pltpu.SemaphoreType.DMA(())],
    )
    return pl.pallas_call(
        functools.partial(_dispatch_kernel, n_tok=n_tok, n_experts=starts.shape[0], tm_e=tm_e, rt=rt),
        grid_spec=grid_spec,
        out_shape=jax.ShapeDtypeStruct((n_rows * rt, LANE), F32),
        compiler_params=pltpu.CompilerParams(dimension_semantics=("arbitrary",), vmem_limit_bytes=VMEM_LIMIT),
        name="dispatch_rows",
    )(pos, starts, ends, h2)


def _expert_kernel(tile_e_ref, used_ref, x_ref, wg_ref, wu_ref, wd_ref, y_ref, wg_b, wu_b, wd_b):
    i = pl.program_id(0)
    used = i < used_ref[0]
    fresh = jnp.logical_or(i == 0, tile_e_ref[i] != tile_e_ref[jnp.maximum(i - 1, 0)])

    @pl.when(jnp.logical_and(used, fresh))
    def _():
        wg_b[...] = wg_ref[0].astype(BF16)
        wu_b[...] = wu_ref[0].astype(BF16)
        wd_b[...] = wd_ref[0].astype(BF16)

    @pl.when(used)
    def _():
        rt = wg_b.shape[0] // LANE
        tm = x_ref.shape[0] // rt
        xb = _load_row_tiles(x_ref, (), 0, tm, rt).astype(BF16)
        a = _dot(xb, wg_b[...])
        b = _dot(xb, wu_b[...])
        _store_row_tiles(y_ref, (), 0, _dot((a * jax.nn.sigmoid(a) * b).astype(BF16), wd_b[...]))

    @pl.when(jnp.logical_not(used))
    def _():
        y_ref[...] = jnp.zeros_like(y_ref)


def _experts(x_sorted, tile_expert, n_used, w_gate, w_up, w_down, tm):
    d, d_e = w_gate.shape[1], w_gate.shape[2]
    rt = d // LANE
    n_rows = x_sorted.shape[0] // rt
    last = lambda i, used: jnp.minimum(i, used[0] - 1)
    grid_spec = pltpu.PrefetchScalarGridSpec(
        num_scalar_prefetch=2,
        grid=(n_rows // tm,),
        in_specs=[pl.BlockSpec((tm * rt, LANE), lambda i, te, used: (last(i, used), 0)),
                  pl.BlockSpec((1, d, d_e), lambda i, te, used: (te[i], 0, 0)),
                  pl.BlockSpec((1, d, d_e), lambda i, te, used: (te[i], 0, 0)),
                  pl.BlockSpec((1, d_e, d), lambda i, te, used: (te[i], 0, 0))],
        out_specs=pl.BlockSpec((tm * rt, LANE), lambda i, te, used: (i, 0)),
        scratch_shapes=[pltpu.VMEM((d, d_e), BF16), pltpu.VMEM((d, d_e), BF16), pltpu.VMEM((d_e, d), BF16)],
    )
    return pl.pallas_call(
        _expert_kernel,
        grid_spec=grid_spec,
        out_shape=jax.ShapeDtypeStruct((n_rows * rt, LANE), F32),
        compiler_params=pltpu.CompilerParams(dimension_semantics=("arbitrary",), vmem_limit_bytes=VMEM_LIMIT),
        name="routed_experts",
    )(tile_expert, n_used, x_sorted, w_gate, w_up, w_down)


def _final_kernel(pos_ref, x1_ref, w_ref, mod_ref, ln_ref, y_hbm, o_ref, ybuf, sems, *, alpha, n_tok):
    i = pl.program_id(0)
    n_steps = pl.num_programs(0)
    tm, d = x1_ref.shape
    rt = d // LANE

    def row_copy(step, r, slot):
        buf = step % 2
        p = pl.multiple_of(pos_ref[slot * n_tok + step * tm + r], rt)
        dst = ybuf.at[2 * buf + slot, pl.ds(pl.multiple_of(r * rt, rt), rt), :]
        return pltpu.make_async_copy(y_hbm.at[pl.ds(p, rt), :], dst, sems.at[buf])

    def issue_tile(step):
        def issue(r, c):
            row_copy(step, r, 0).start()
            row_copy(step, r, 1).start()
            return c
        lax.fori_loop(0, tm, issue, 0, unroll=DMA_UNROLL)

    @pl.when(i == 0)
    def _():
        issue_tile(i)

    @pl.when(i + 1 < n_steps)
    def _():
        issue_tile(i + 1)

    def drain(r, c):
        row_copy(i, r, 0).wait()
        row_copy(i, r, 1).wait()
        return c

    lax.fori_loop(0, tm, drain, 0, unroll=DMA_UNROLL)
    cur = 2 * (i % 2)
    moe = (w_ref[:, 0:1] * _load_row_tiles(ybuf, (cur,), 0, tm, rt)
           + w_ref[:, 1:2] * _load_row_tiles(ybuf, (cur + 1,), 0, tm, rt))
    z = alpha * x1_ref[...] + mod_ref[0] * moe
    o_ref[...] = _layer_norm(z) * ln_ref[0:1, :] + ln_ref[1:2, :]


def _final(x1, pos, wts, y_sorted, g2, ln, alpha, seq):
    n_tok, d = x1.shape
    rt = d // LANE
    tm = min(seq, 256)
    per_b = seq // tm
    grid_spec = pltpu.PrefetchScalarGridSpec(
        num_scalar_prefetch=1,
        grid=(n_tok // tm,),
        in_specs=[pl.BlockSpec((tm, d), lambda i, *_: (i, 0)),
                  pl.BlockSpec((tm, 2), lambda i, *_: (i, 0)),
                  pl.BlockSpec((1, 1, d), lambda i, *_: (i // per_b, 0, 0)),
                  pl.BlockSpec((2, d), lambda i, *_: (0, 0)),
                  pl.BlockSpec(memory_space=pl.ANY)],
        out_specs=pl.BlockSpec((tm, d), lambda i, *_: (i, 0)),
        scratch_shapes=[pltpu.VMEM((4, tm * rt, LANE), F32), pltpu.SemaphoreType.DMA((2,))],
    )
    return pl.pallas_call(
        functools.partial(_final_kernel, alpha=alpha, n_tok=n_tok),
        grid_spec=grid_spec,
        out_shape=jax.ShapeDtypeStruct((n_tok, d), F32),
        compiler_params=pltpu.CompilerParams(dimension_semantics=("arbitrary",), vmem_limit_bytes=VMEM_LIMIT),
        name="combine_postnorm",
    )(pos, x1, wts, g2, ln, y_sorted)


def _tile_plan(counts, n_rows, tm):
    n_experts = counts.shape[0]
    padded = ((counts + tm - 1) // tm) * tm
    ends = jnp.cumsum(padded).astype(jnp.int32)
    starts = ends - padded
    tile_start = jnp.arange(n_rows // tm, dtype=jnp.int32) * tm
    tile_e = jnp.sum((tile_start[:, None] >= ends[None, :]).astype(jnp.int32), axis=1)
    n_used = ends[-1:] // tm
    tile_e = jnp.minimum(tile_e, jnp.max(jnp.where(counts > 0, jnp.arange(n_experts, dtype=jnp.int32), 0)))
    return starts, ends, tile_e, n_used


def kernel(x, c, ctx, c_ctx, w_mod, b_mod, w_in, s5_a_re, s5_a_im, s5_log_dt, s5_b_re, s5_b_im, s5_c_re, s5_c_im, s5_d, w_glu_a, w_glu_b, q_gain, k_gain, w_attn_o, w_out, ln1_g, ln1_b, w_router_group, b_router_group, w_router_expert, b_router_expert, w_exp_gate, w_exp_up, w_exp_down, ln2_g, ln2_b):
    bsz, n_lat, d = x.shape
    n_ctx = ctx.shape[1]
    assert w_mod.shape[0] == DEPTH == 1 and bsz + 1 <= 8
    alpha = (2.0 * DEPTH) ** 0.25
    d_s5 = s5_d.shape[1]
    d_q = w_attn_o.shape[1]
    d_kv = N_KV_HEADS * HEAD_DIM
    n_experts = w_exp_gate.shape[1]

    cond = jnp.concatenate([c, c_ctx[None], jnp.zeros((8 - bsz - 1, d), F32)], axis=0)
    mod = _adaln(cond, w_mod[0], b_mod[0]).reshape(8, 6, d)
    mod_lat = mod[:bsz]
    mod_ctx = jnp.broadcast_to(mod[bsz:bsz + 1], (bsz, 6, d))

    w_in_b = w_in[0].astype(BF16)
    cos, sin = _rope_tables(n_lat)
    qg, kg = q_gain[0].reshape(1, HEAD_DIM), k_gain[0].reshape(1, HEAD_DIM)
    dims = dict(d_s5=d_s5, d_q=d_q, d_kv=d_kv)
    u, q, k, v, gates = _inproj(x, mod_lat[:, 0:2], w_in_b, cos, sin, qg, kg, latent=True, **dims)
    uc, kc, vc = _inproj(ctx, mod_ctx[:, 0:2], w_in_b, cos[:n_ctx], sin[:n_ctx], qg, kg, latent=False, **dims)

    w_t, w_z, w_c, a_chunk = _s5_weights(s5_a_re[0], s5_a_im[0], s5_log_dt[0], s5_b_re[0], s5_b_im[0],
                                         s5_c_re[0], s5_c_im[0])
    y_ssm = _s5_branch(u, uc, w_t, w_z, w_c, a_chunk, s5_d[0])

    o = _attention(q, k, v, kc, vc)

    n_r = 8 + n_experts
    n_r_pad = ((n_r + LANE - 1) // LANE) * LANE
    pad_cols = lambda g, e: jnp.concatenate(
        [g, jnp.zeros(g.shape[:-1] + (8 - N_EXPERT_GROUPS,), F32), e,
         jnp.zeros(g.shape[:-1] + (n_r_pad - n_r,), F32)], axis=-1)
    w_r = pad_cols(w_router_group[0], w_router_expert[0])
    b_r = pad_cols(b_router_group[0][None], b_router_expert[0][None])
    wr_hi, wr_lo = _split_bf16(w_r)
    ln1 = jnp.stack([ln1_g[0], ln1_b[0]])
    x1, h2, logits = _merge(x, y_ssm, o, gates, mod_lat[:, 2:5], w_glu_a[0].astype(BF16),
                            w_glu_b[0].astype(BF16), w_attn_o[0].astype(BF16), w_out[0].astype(BF16),
                            ln1, wr_hi, wr_lo, b_r, alpha)

    n_tok = bsz * n_lat
    tm_e = EXPERT_TILE
    n_rows = 2 * n_tok + n_experts * tm_e
    logits_t = logits.reshape(n_tok, n_r_pad)[:, :n_r].T
    eid, wts, rank, counts = _route(logits_t, n_experts)
    starts, ends, tile_e, n_used = _tile_plan(counts[:, 0], n_rows, tm_e)
    own = eid[:2, :, None] == jnp.arange(n_experts, dtype=jnp.int32)
    rt = d // LANE
    pos = ((jnp.sum(jnp.where(own, starts, 0), axis=-1) + rank[:2]) * rt).reshape(-1)
    x_sorted = _dispatch(h2.reshape(n_tok * rt, LANE), pos, starts, ends, n_rows, tm_e, rt)
    y_sorted = _experts(x_sorted, tile_e, n_used, w_exp_gate[0], w_exp_up[0], w_exp_down[0], tm_e)
    ln2 = jnp.stack([ln2_g[0], ln2_b[0]])
    out = _final(x1.reshape(n_tok, d), pos, wts[:2].T, y_sorted, mod_lat[:, 5:6], ln2, alpha, n_lat)
    return out.reshape(bsz, n_lat, d)
```
